```python
import math
import jax, jax.numpy as jnp
from jax import lax
import numpy as np

D_MODEL = 4096
BATCH = 8
SEQ = 4096
DEPTH = 2

D_MIX = D_MODEL
D_POOL = D_MIX // 4
D_ATTN = D_MIX // 2
D_SSM = D_MIX - D_POOL - D_ATTN
POOL_WINDOWS = (2, 4, 8, 16)
N_POOL_GROUPS = len(POOL_WINDOWS)
POOL_GROUP = D_POOL // N_POOL_GROUPS
HEAD_DIM = 128
N_HEADS = D_ATTN // HEAD_DIM
Q_BLOCK = 128
SSM_GROUP = 16
N_SSM_GROUPS = D_SSM // SSM_GROUP
SSM_STATE = 64
D_IN = 2 * D_POOL + 4 * D_ATTN + 2 * D_SSM
EPS = 1e-6

kernel_name = "hybrid_pool_stickbreak_s5_parallel"

_SPLIT_SIZES = (D_POOL, D_POOL, D_ATTN, D_ATTN, D_ATTN, D_ATTN, D_SSM, D_SSM)
_SPLIT_OFFSETS = tuple(int(v) for v in np.cumsum(_SPLIT_SIZES)[:-1])


def rmsnorm(x, g):
    xf = x.astype(jnp.float32)
    y = xf * lax.rsqrt(jnp.mean(xf * xf, axis=-1, keepdims=True) + EPS)
    return y * g.astype(jnp.float32)


def pool_mixer(xp, w_pool, pool_scale):
    bsz, L, _ = xp.shape
    xg = xp.astype(jnp.float32).reshape(bsz, L, N_POOL_GROUPS, POOL_GROUP)
    csum = jnp.cumsum(xg, axis=1)
    pos = jnp.arange(1, L + 1, dtype=jnp.float32)[None, :, None]
    outs = []
    for g, w in enumerate(POOL_WINDOWS):
        cg = csum[:, :, g]
        c_lag = jnp.pad(cg, ((0, 0), (w, 0), (0, 0)))[:, :L]
        mean = (cg - c_lag) / jnp.minimum(pos, float(w))
        outs.append(mean - xg[:, :, g])
    pooled = jnp.stack(outs, axis=2)
    mixed = jnp.einsum('blgc,gcd->blgd', pooled, w_pool.astype(jnp.float32))
    return mixed.reshape(bsz, L, D_POOL) * pool_scale.astype(jnp.float32)


def stick_breaking_attention(q, k, v):
    bsz, L = q.shape[:2]
    nb = L // Q_BLOCK
    qf = q.astype(jnp.float32) * (HEAD_DIM ** -0.5)
    kh = k.astype(jnp.float32).transpose(0, 2, 1, 3)
    vh = v.astype(jnp.float32).transpose(0, 2, 1, 3)
    q_blocks = qf.reshape(bsz, nb, Q_BLOCK, N_HEADS, HEAD_DIM).transpose(1, 0, 3, 2, 4)
    k_pos = jnp.arange(L)

    def one_block(args):
        qb, start = args
        z = jnp.einsum('bhqd,bhkd->bhqk', qb, kh)
        q_pos = start + jnp.arange(Q_BLOCK)
        causal = k_pos[None, :] < q_pos[:, None]
        log_1m = jnp.where(causal, jax.nn.log_sigmoid(-z), 0.0)
        suffix = lax.cumsum(log_1m, axis=3, reverse=True) - log_1m
        wts = jnp.where(causal, jnp.exp(jax.nn.log_sigmoid(z) + suffix), 0.0)
        return jnp.einsum('bhqk,bhkd->bhqd', wts, vh)

    starts = jnp.arange(nb, dtype=jnp.int32) * Q_BLOCK
    out = lax.map(one_block, (q_blocks, starts))
    return out.transpose(1, 0, 3, 2, 4).reshape(bsz, L, D_ATTN)


def s5_mixer(u, lam_re, lam_im, log_dt, b_re, b_im, c_re, c_im, d_skip, w_glu, b_glu):
    bsz, L, _ = u.shape
    uf = u.astype(jnp.float32)
    lam = lax.complex(lam_re.astype(jnp.float32), lam_im.astype(jnp.float32))
    dt = jnp.exp(log_dt.astype(jnp.float32))[:, None]
    lam_bar = jnp.exp(lam * dt)
    b_mat = lax.complex(b_re.astype(jnp.float32), b_im.astype(jnp.float32))
    b_bar = ((lam_bar - 1.0) / lam)[..., None] * b_mat
    c_mat = lax.complex(c_re.astype(jnp.float32), c_im.astype(jnp.float32))
    ug = uf.reshape(bsz, L, N_SSM_GROUPS, SSM_GROUP)
    bu = jnp.einsum('blgc,gpc->blgp', ug.astype(jnp.complex64), b_bar)
    a = jnp.broadcast_to(lam_bar, bu.shape)

    def combine(e_prev, e_next):
        a1, x1 = e_prev
        a2, x2 = e_next
        return a2 * a1, a2 * x1 + x2

    _, states = lax.associative_scan(combine, (a, bu), axis=1)
    y = jnp.einsum('blgp,gcp->blgc', states, c_mat).real.reshape(bsz, L, D_SSM)
    y = y + d_skip.astype(jnp.float32) * uf
    h = jax.nn.gelu(y)
    val, gate = jnp.split(h @ w_glu.astype(jnp.float32) + b_glu.astype(jnp.float32), 2, axis=-1)
    return val * jax.nn.sigmoid(gate)


def hybrid_layer(x, ln_g, w_in, w_pool, pool_scale, lam_re, lam_im, log_dt,
                 b_re, b_im, c_re, c_im, d_skip, w_glu, b_glu, branch_g, w_out):
    bsz, L, _ = x.shape
    h = rmsnorm(x, ln_g).astype(x.dtype)
    proj = h @ w_in
    p_x, p_gate, q, k, v, a_gate, s_u, s_gate = jnp.split(proj, _SPLIT_OFFSETS, axis=-1)

    y_pool = pool_mixer(p_x, w_pool, pool_scale)
    shp = (bsz, L, N_HEADS, HEAD_DIM)
    y_attn = stick_breaking_attention(q.reshape(shp), k.reshape(shp), v.reshape(shp))
    y_ssm = s5_mixer(s_u, lam_re, lam_im, log_dt, b_re, b_im, c_re, c_im, d_skip, w_glu, b_glu)

    g_pool, g_attn, g_ssm = jnp.split(branch_g, [D_POOL, D_POOL + D_ATTN])
    y_pool = rmsnorm(y_pool, g_pool) * jax.nn.silu(p_gate.astype(jnp.float32))
    y_attn = rmsnorm(y_attn, g_attn) * jax.nn.silu(a_gate.astype(jnp.float32))
    y_ssm = rmsnorm(y_ssm, g_ssm) * jax.nn.silu(s_gate.astype(jnp.float32))
    y = jnp.concatenate([y_pool, y_attn, y_ssm], axis=-1).astype(x.dtype)
    return x + y @ w_out


def _fwd_setup_inputs(seed: int = 0) -> dict:
    key = jax.random.key(seed)
    ks = jax.random.split(key, 20)
    f32 = jnp.float32
    nrm = lambda k, shape, s: jax.random.normal(k, shape, f32) * s
    n_idx = jnp.arange(SSM_STATE, dtype=f32)
    return {
        "x": jax.random.normal(ks[0], (BATCH, SEQ, D_MODEL), f32),
        "ln_g": 1.0 + nrm(ks[1], (DEPTH, D_MODEL), 0.02),
        "w_in": nrm(ks[2], (DEPTH, D_MODEL, D_IN), D_MODEL ** -0.5),
        "w_pool": nrm(ks[3], (DEPTH, N_POOL_GROUPS, POOL_GROUP, POOL_GROUP), POOL_GROUP ** -0.5),
        "pool_scale": 1.0 + nrm(ks[4], (DEPTH, D_POOL), 0.1),
        "lam_re": -0.5 + nrm(ks[5], (DEPTH, N_SSM_GROUPS, SSM_STATE), 0.01),
        "lam_im": math.pi * n_idx + nrm(ks[6], (DEPTH, N_SSM_GROUPS, SSM_STATE), 0.01),
        "log_dt": jax.random.uniform(ks[7], (DEPTH, N_SSM_GROUPS), f32,
                                     math.log(1e-3), math.log(1e-1)),
        "b_re": nrm(ks[8], (DEPTH, N_SSM_GROUPS, SSM_STATE, SSM_GROUP), (2 * SSM_GROUP) ** -0.5),
        "b_im": nrm(ks[9], (DEPTH, N_SSM_GROUPS, SSM_STATE, SSM_GROUP), (2 * SSM_GROUP) ** -0.5),
        "c_re": nrm(ks[10], (DEPTH, N_SSM_GROUPS, SSM_GROUP, SSM_STATE), (2 * SSM_STATE) ** -0.5),
        "c_im": nrm(ks[11], (DEPTH, N_SSM_GROUPS, SSM_GROUP, SSM_STATE), (2 * SSM_STATE) ** -0.5),
        "d_skip": nrm(ks[12], (DEPTH, D_SSM), 1.0),
        "w_glu": nrm(ks[13], (DEPTH, D_SSM, 2 * D_SSM), D_SSM ** -0.5),
        "b_glu": nrm(ks[14], (DEPTH, 2 * D_SSM), 0.01),
        "branch_g": 1.0 + nrm(ks[15], (DEPTH, D_MIX), 0.02),
        "w_out": nrm(ks[16], (DEPTH, D_MIX, D_MODEL), (2 * DEPTH * D_MIX) ** -0.5),
        "final_g": 1.0 + nrm(ks[17], (D_MODEL,), 0.02),
    }


def _fwd_reference(x, ln_g, w_in, w_pool, pool_scale, lam_re, lam_im, log_dt,
              b_re, b_im, c_re, c_im, d_skip, w_glu, b_glu, branch_g, w_out, final_g):
    h = x
    for l in range(DEPTH):
        h = hybrid_layer(h, ln_g[l], w_in[l], w_pool[l], pool_scale[l], lam_re[l], lam_im[l],
                         log_dt[l], b_re[l], b_im[l], c_re[l], c_im[l], d_skip[l],
                         w_glu[l], b_glu[l], branch_g[l], w_out[l])
    return rmsnorm(h, final_g).astype(x.dtype)


import jax as _jax
import jax.numpy as _jnp

TWIN_FORMAT = 'train_step'
FWD_PARAMS = ['x', 'ln_g', 'w_in', 'w_pool', 'pool_scale', 'lam_re', 'lam_im', 'log_dt', 'b_re', 'b_im', 'c_re', 'c_im', 'd_skip', 'w_glu', 'b_glu', 'branch_g', 'w_out', 'final_g']
TWIN_WEIGHTS = ['ln_g', 'w_in', 'w_pool', 'pool_scale', 'lam_re', 'lam_im', 'log_dt', 'b_re', 'b_im', 'c_re', 'c_im', 'd_skip', 'w_glu', 'b_glu', 'branch_g', 'w_out', 'final_g']
TWIN_DIFF_INPUT = 'x'
TWIN_INPUTS = ['x', 'ln_g', 'w_in', 'w_pool', 'pool_scale', 'lam_re', 'lam_im', 'log_dt', 'b_re', 'b_im', 'c_re', 'c_im', 'd_skip', 'w_glu', 'b_glu', 'branch_g', 'w_out', 'final_g', 'loss_target', 'm_ln_g', 'm_w_in', 'm_w_pool', 'm_pool_scale', 'm_lam_re', 'm_lam_im', 'm_log_dt', 'm_b_re', 'm_b_im', 'm_c_re', 'm_c_im', 'm_d_skip', 'm_w_glu', 'm_b_glu', 'm_branch_g', 'm_w_out', 'm_final_g', 'v_ln_g', 'v_w_in', 'v_w_pool', 'v_pool_scale', 'v_lam_re', 'v_lam_im', 'v_log_dt', 'v_b_re', 'v_b_im', 'v_c_re', 'v_c_im', 'v_d_skip', 'v_w_glu', 'v_b_glu', 'v_branch_g', 'v_w_out', 'v_final_g']
TWIN_OUTPUTS = ['loss', 'grad_x', 'grad_ln_g', 'grad_w_in', 'grad_w_pool', 'grad_pool_scale', 'grad_lam_re', 'grad_lam_im', 'grad_log_dt', 'grad_b_re', 'grad_b_im', 'grad_c_re', 'grad_c_im', 'grad_d_skip', 'grad_w_glu', 'grad_b_glu', 'grad_branch_g', 'grad_w_out', 'grad_final_g', 'delta_ln_g', 'delta_w_in', 'delta_w_pool', 'delta_pool_scale', 'delta_lam_re', 'delta_lam_im', 'delta_log_dt', 'delta_b_re', 'delta_b_im', 'delta_c_re', 'delta_c_im', 'delta_d_skip', 'delta_w_glu', 'delta_b_glu', 'delta_branch_g', 'delta_w_out', 'delta_final_g', 'new_m_ln_g', 'new_m_w_in', 'new_m_w_pool', 'new_m_pool_scale', 'new_m_lam_re', 'new_m_lam_im', 'new_m_log_dt', 'new_m_b_re', 'new_m_b_im', 'new_m_c_re', 'new_m_c_im', 'new_m_d_skip', 'new_m_w_glu', 'new_m_b_glu', 'new_m_branch_g', 'new_m_w_out', 'new_m_final_g', 'new_v_ln_g', 'new_v_w_in', 'new_v_w_pool', 'new_v_pool_scale', 'new_v_lam_re', 'new_v_lam_im', 'new_v_log_dt', 'new_v_b_re', 'new_v_b_im', 'new_v_c_re', 'new_v_c_im', 'new_v_d_skip', 'new_v_w_glu', 'new_v_b_glu', 'new_v_branch_g', 'new_v_w_out', 'new_v_final_g']
TWIN_LEAF_KINDS = {'loss': 'loss', 'grad_x': 'grad_x', 'grad_ln_g': 'grad_w', 'grad_w_in': 'grad_w', 'grad_w_pool': 'grad_w', 'grad_pool_scale': 'grad_w', 'grad_lam_re': 'grad_w', 'grad_lam_im': 'grad_w', 'grad_log_dt': 'grad_w', 'grad_b_re': 'grad_w', 'grad_b_im': 'grad_w', 'grad_c_re': 'grad_w', 'grad_c_im': 'grad_w', 'grad_d_skip': 'grad_w', 'grad_w_glu': 'grad_w', 'grad_b_glu': 'grad_w', 'grad_branch_g': 'grad_w', 'grad_w_out': 'grad_w', 'grad_final_g': 'grad_w', 'delta_ln_g': 'delta_w', 'delta_w_in': 'delta_w', 'delta_w_pool': 'delta_w', 'delta_pool_scale': 'delta_w', 'delta_lam_re': 'delta_w', 'delta_lam_im': 'delta_w', 'delta_log_dt': 'delta_w', 'delta_b_re': 'delta_w', 'delta_b_im': 'delta_w', 'delta_c_re': 'delta_w', 'delta_c_im': 'delta_w', 'delta_d_skip': 'delta_w', 'delta_w_glu': 'delta_w', 'delta_b_glu': 'delta_w', 'delta_branch_g': 'delta_w', 'delta_w_out': 'delta_w', 'delta_final_g': 'delta_w', 'new_m_ln_g': 'new_m', 'new_m_w_in': 'new_m', 'new_m_w_pool': 'new_m', 'new_m_pool_scale': 'new_m', 'new_m_lam_re': 'new_m', 'new_m_lam_im': 'new_m', 'new_m_log_dt': 'new_m', 'new_m_b_re': 'new_m', 'new_m_b_im': 'new_m', 'new_m_c_re': 'new_m', 'new_m_c_im': 'new_m', 'new_m_d_skip': 'new_m', 'new_m_w_glu': 'new_m', 'new_m_b_glu': 'new_m', 'new_m_branch_g': 'new_m', 'new_m_w_out': 'new_m', 'new_m_final_g': 'new_m', 'new_v_ln_g': 'new_v', 'new_v_w_in': 'new_v', 'new_v_w_pool': 'new_v', 'new_v_pool_scale': 'new_v', 'new_v_lam_re': 'new_v', 'new_v_lam_im': 'new_v', 'new_v_log_dt': 'new_v', 'new_v_b_re': 'new_v', 'new_v_b_im': 'new_v', 'new_v_c_re': 'new_v', 'new_v_c_im': 'new_v', 'new_v_d_skip': 'new_v', 'new_v_w_glu': 'new_v', 'new_v_b_glu': 'new_v', 'new_v_branch_g': 'new_v', 'new_v_w_out': 'new_v', 'new_v_final_g': 'new_v'}


def _forward(args):
    return _fwd_reference(*[args[k] for k in FWD_PARAMS])


def _output_shape():
    out = _jax.eval_shape(lambda: _forward(_fwd_setup_inputs(0)))
    return out.shape, out.dtype

N_MICROBATCH = 1
ADAM_LR = 0.001
ADAM_B1 = 0.9
ADAM_B2 = 0.999
ADAM_EPS = 1e-08
ADAM_WD = 0.01
ADAM_STEP = 10
PER_EXAMPLE_BATCH_AXIS = {'x': 0, 'loss_target': 0}
SHARED_INPUTS = []
_WEIGHT_DTYPES = {'ln_g': _jnp.float32, 'w_in': _jnp.float32, 'w_pool': _jnp.float32, 'pool_scale': _jnp.float32, 'lam_re': _jnp.float32, 'lam_im': _jnp.float32, 'log_dt': _jnp.float32, 'b_re': _jnp.float32, 'b_im': _jnp.float32, 'c_re': _jnp.float32, 'c_im': _jnp.float32, 'd_skip': _jnp.float32, 'w_glu': _jnp.float32, 'b_glu': _jnp.float32, 'branch_g': _jnp.float32, 'w_out': _jnp.float32, 'final_g': _jnp.float32}
MOMENT_SCALE = {'ln_g': 1.961213e-02, 'w_in': 1.124101e-02, 'w_pool': 1.276336e-02, 'pool_scale': 1.293630e-02, 'lam_re': 7.300952e-04, 'lam_im': 6.989320e-04, 'log_dt': 4.115697e-01, 'b_re': 4.530064e-04, 'b_im': 4.472355e-04, 'c_re': 8.995335e-04, 'c_im': 9.028503e-04, 'd_skip': 1.397664e-02, 'w_glu': 9.445352e-03, 'b_glu': 1.478175e-02, 'branch_g': 1.285865e-02, 'w_out': 2.554297e-02, 'final_g': 7.982240e+00}


def _to_microbatches(a, axis):
    t = _jnp.moveaxis(a, axis, 0)
    t = t.reshape((N_MICROBATCH, t.shape[0] // N_MICROBATCH) + t.shape[1:])
    return _jnp.moveaxis(t, 1, axis + 1)


def setup_inputs(seed: int = 0) -> dict:
    inp = _fwd_setup_inputs(seed)
    key = _jax.random.fold_in(_jax.random.key(seed), 7919)
    shape, _ = _output_shape()
    out = dict(inp)
    out["loss_target"] = _jax.random.normal(_jax.random.fold_in(key, 0), shape, _jnp.float32)
    for i, name in enumerate(TWIN_WEIGHTS):
        w = inp[name].astype(_jnp.float32)
        if MOMENT_SCALE is None:
            s = _jnp.sqrt(_jnp.mean(_jnp.square(w)) + 1e-30)
        else:
            s = MOMENT_SCALE[name]
        km, kv = _jax.random.split(_jax.random.fold_in(key, i + 1))
        out[name] = w
        out["m_" + name] = s * _jax.random.normal(km, w.shape, _jnp.float32)
        out["v_" + name] = (s * s) * _jax.random.uniform(kv, w.shape, _jnp.float32, 0.5, 1.5)
    if N_MICROBATCH > 1:
        for name, axis in PER_EXAMPLE_BATCH_AXIS.items():
            out[name] = _to_microbatches(out[name], axis)
    return {'x': out['x'], 'ln_g': out['ln_g'], 'w_in': out['w_in'], 'w_pool': out['w_pool'], 'pool_scale': out['pool_scale'], 'lam_re': out['lam_re'], 'lam_im': out['lam_im'], 'log_dt': out['log_dt'], 'b_re': out['b_re'], 'b_im': out['b_im'], 'c_re': out['c_re'], 'c_im': out['c_im'], 'd_skip': out['d_skip'], 'w_glu': out['w_glu'], 'b_glu': out['b_glu'], 'branch_g': out['branch_g'], 'w_out': out['w_out'], 'final_g': out['final_g'], 'loss_target': out['loss_target'], 'm_ln_g': out['m_ln_g'], 'm_w_in': out['m_w_in'], 'm_w_pool': out['m_w_pool'], 'm_pool_scale': out['m_pool_scale'], 'm_lam_re': out['m_lam_re'], 'm_lam_im': out['m_lam_im'], 'm_log_dt': out['m_log_dt'], 'm_b_re': out['m_b_re'], 'm_b_im': out['m_b_im'], 'm_c_re': out['m_c_re'], 'm_c_im': out['m_c_im'], 'm_d_skip': out['m_d_skip'], 'm_w_glu': out['m_w_glu'], 'm_b_glu': out['m_b_glu'], 'm_branch_g': out['m_branch_g'], 'm_w_out': out['m_w_out'], 'm_final_g': out['m_final_g'], 'v_ln_g': out['v_ln_g'], 'v_w_in': out['v_w_in'], 'v_w_pool': out['v_w_pool'], 'v_pool_scale': out['v_pool_scale'], 'v_lam_re': out['v_lam_re'], 'v_lam_im': out['v_lam_im'], 'v_log_dt': out['v_log_dt'], 'v_b_re': out['v_b_re'], 'v_b_im': out['v_b_im'], 'v_c_re': out['v_c_re'], 'v_c_im': out['v_c_im'], 'v_d_skip': out['v_d_skip'], 'v_w_glu': out['v_w_glu'], 'v_b_glu': out['v_b_glu'], 'v_branch_g': out['v_branch_g'], 'v_w_out': out['v_w_out'], 'v_final_g': out['v_final_g']}


def _loss(weights, diff, rest, loss_target):
    with _jax.named_scope("forward"):
        args = {**rest, TWIN_DIFF_INPUT: diff, **{k: w.astype(_WEIGHT_DTYPES[k]) for k, w in weights.items()}}
        y = _forward(args)
    with _jax.named_scope("loss_head"):
        err = _jnp.square(y.astype(_jnp.float32) - loss_target)
        return 0.5 * _jnp.sum(_jnp.mean(err, axis=-1)) if err.ndim else 0.5 * err


def _adamw(w, g, m, v):
    m = ADAM_B1 * m + (1.0 - ADAM_B1) * g
    v = ADAM_B2 * v + (1.0 - ADAM_B2) * _jnp.square(g)
    m_hat = m / (1.0 - ADAM_B1 ** ADAM_STEP)
    v_hat = v / (1.0 - ADAM_B2 ** ADAM_STEP)
    delta = -ADAM_LR * (m_hat / (_jnp.sqrt(v_hat) + ADAM_EPS) + ADAM_WD * w)
    return delta, m, v


def reference(x, ln_g, w_in, w_pool, pool_scale, lam_re, lam_im, log_dt, b_re, b_im, c_re, c_im, d_skip, w_glu, b_glu, branch_g, w_out, final_g, loss_target, m_ln_g, m_w_in, m_w_pool, m_pool_scale, m_lam_re, m_lam_im, m_log_dt, m_b_re, m_b_im, m_c_re, m_c_im, m_d_skip, m_w_glu, m_b_glu, m_branch_g, m_w_out, m_final_g, v_ln_g, v_w_in, v_w_pool, v_pool_scale, v_lam_re, v_lam_im, v_log_dt, v_b_re, v_b_im, v_c_re, v_c_im, v_d_skip, v_w_glu, v_b_glu, v_branch_g, v_w_out, v_final_g):
    given = dict(x=x, ln_g=ln_g, w_in=w_in, w_pool=w_pool, pool_scale=pool_scale, lam_re=lam_re, lam_im=lam_im, log_dt=log_dt, b_re=b_re, b_im=b_im, c_re=c_re, c_im=c_im, d_skip=d_skip, w_glu=w_glu, b_glu=b_glu, branch_g=branch_g, w_out=w_out, final_g=final_g, loss_target=loss_target, m_ln_g=m_ln_g, m_w_in=m_w_in, m_w_pool=m_w_pool, m_pool_scale=m_pool_scale, m_lam_re=m_lam_re, m_lam_im=m_lam_im, m_log_dt=m_log_dt, m_b_re=m_b_re, m_b_im=m_b_im, m_c_re=m_c_re, m_c_im=m_c_im, m_d_skip=m_d_skip, m_w_glu=m_w_glu, m_b_glu=m_b_glu, m_branch_g=m_branch_g, m_w_out=m_w_out, m_final_g=m_final_g, v_ln_g=v_ln_g, v_w_in=v_w_in, v_w_pool=v_w_pool, v_pool_scale=v_pool_scale, v_lam_re=v_lam_re, v_lam_im=v_lam_im, v_log_dt=v_log_dt, v_b_re=v_b_re, v_b_im=v_b_im, v_c_re=v_c_re, v_c_im=v_c_im, v_d_skip=v_d_skip, v_w_glu=v_w_glu, v_b_glu=v_b_glu, v_branch_g=v_branch_g, v_w_out=v_w_out, v_final_g=v_final_g)
    weights = {n: given[n] for n in TWIN_WEIGHTS}
    shared = {n: given[n] for n in SHARED_INPUTS}
    per_example = {n: given[n] for n in ['x']}
    grad_fn = _jax.value_and_grad(_loss, argnums=(0, 1))

    def one_microbatch(ex, loss_target):
        ex = dict(ex)
        diff = ex.pop(TWIN_DIFF_INPUT)
        return grad_fn(weights, diff, {**shared, **ex}, loss_target)

    if N_MICROBATCH == 1:
        loss, (grad_w, grad_x) = one_microbatch(per_example, given["loss_target"])
    else:
        def body(carry, xs):
            loss_sum, grad_sum = carry
            l_k, (gw_k, gx_k) = one_microbatch(xs[0], xs[1])
            with _jax.named_scope("update"):
                return (loss_sum + l_k, _jax.tree.map(_jnp.add, grad_sum, gw_k)), gx_k

        init = (_jnp.zeros((), _jnp.float32), _jax.tree.map(_jnp.zeros_like, weights))
        (loss, grad_w), grad_x = _jax.lax.scan(body, init, (per_example, given["loss_target"]))
    with _jax.named_scope("update"):
        delta_w, new_m, new_v = {}, {}, {}
        for n in TWIN_WEIGHTS:
            delta_w[n], new_m[n], new_v[n] = _adamw(weights[n], grad_w[n], given["m_" + n], given["v_" + n])
    return (loss, grad_x, *[grad_w[n] for n in TWIN_WEIGHTS], *[delta_w[n] for n in TWIN_WEIGHTS],
            *[new_m[n] for n in TWIN_WEIGHTS], *[new_v[n] for n in TWIN_WEIGHTS])
```

```python
import functools
import math

import jax
import jax.numpy as jnp
from jax import lax
from jax.experimental import pallas as pl
from jax.experimental.pallas import tpu as pltpu

F32 = jnp.float32
BF16 = jnp.bfloat16
EPS = 1e-6
POOL_WINDOWS = (2, 4, 8, 16)
POOL_HALO = 16
HEAD_DIM = 128
SSM_GROUP = 16
SSM_STATE = 64
GROUPS_PER_CHUNK = 16
LANES = 128
VMEM_LIMIT_BYTES = 56 * 1024 * 1024
ROW_TILE_ELEMS = 2 * 1024 * 1024
ADAM_LR, ADAM_B1, ADAM_B2, ADAM_EPS, ADAM_WD, ADAM_STEP = 0.001, 0.9, 0.999, 1e-08, 0.01, 10
MESH = pl.DeviceIdType.MESH
AXES = ("x", "y", "c")
WEIGHTS = ("ln_g", "w_in", "w_pool", "pool_scale", "lam_re", "lam_im", "log_dt", "b_re", "b_im",
           "c_re", "c_im", "d_skip", "w_glu", "b_glu", "branch_g", "w_out", "final_g")
SHARDED = ("w_in", "w_pool", "w_glu", "w_out")


def _params(*sem):
    return pltpu.CompilerParams(dimension_semantics=sem or None, vmem_limit_bytes=VMEM_LIMIT_BYTES)


def _dot(a, b, dims=((1,), (0,))):
    return lax.dot_general(a, b, (dims, ((), ())), preferred_element_type=F32)


NN, NT, TN = ((1,), (0,)), ((1,), (1,)), ((0,), (0,))


def _split(x):
    hi = x.astype(BF16)
    return hi, (x - hi.astype(F32)).astype(BF16)


def _dot3(a, b, dims):
    ah, al = _split(a)
    bh, bl = _split(b)
    return _dot(ah, bh, dims) + _dot(ah, bl, dims) + _dot(al, bh, dims)


def _sigmoid(x):
    return 1.0 / (1.0 + jnp.exp(-x))


def _gelu(x):
    return 0.5 * x * (1.0 + jnp.tanh(0.7978845608028654 * (x + 0.044715 * x * x * x)))


def _rms(x, g):
    return x * lax.rsqrt(jnp.mean(x * x, axis=-1, keepdims=True) + EPS) * g


def _fit(tile, dim):
    tile = min(tile, dim)
    step = LANES if tile >= LANES else 8
    tile -= tile % step
    while dim % tile:
        tile -= step
    return tile


def _mm(a, b, dims, out_dtype, name, *, n=None, b_quarters=None, out_quarters=None, add=None, tm=512, tn=512, tk=4096):
    if dims == TN:
        k_dim, m_dim = a.shape
    else:
        m_dim, k_dim = a.shape
    if n is None:
        n = b.shape[0] if dims == NT else b.shape[1]
    m_unit = {None: m_dim, "cols": m_dim // 2, "rows": m_dim // 8}[out_quarters]
    n_unit = n // 4 if (b_quarters == "n" or out_quarters == "cols") else n
    k_unit = k_dim // 4 if b_quarters == "k" else k_dim
    tm, tn, tk = _fit(tm, m_unit), _fit(tn, n_unit), _fit(tk, k_unit)
    gm, gn, gk = m_dim // tm, n // tn, k_dim // tk
    mb, nb, kb = m_unit // tm, n_unit // tn, k_unit // tk
    if dims == TN:
        a_spec = pl.BlockSpec((tk, tm), lambda i, j, k: (k, i))
    else:
        a_spec = pl.BlockSpec((tm, tk), lambda i, j, k: (i, k))
    if b_quarters == "n":
        bspec = pl.BlockSpec((None, tk, tn), lambda i, j, k: (j // nb, k, j % nb))
    elif b_quarters == "k":
        bspec = pl.BlockSpec((None, tn, tk), lambda i, j, k: (k // kb, j, k % kb))
    elif dims == NT:
        bspec = pl.BlockSpec((tn, tk), lambda i, j, k: (j, k))
    else:
        bspec = pl.BlockSpec((tk, tn), lambda i, j, k: (k, j))
    if out_quarters == "cols":
        out_shape = (2, 4, m_unit, n_unit)
        out_spec = pl.BlockSpec((None, None, tm, tn), lambda i, j, k: (i // mb, j // nb, i % mb, j % nb))
    elif out_quarters == "rows":
        out_shape = (2, 4, m_unit, n)
        out_spec = pl.BlockSpec((None, None, tm, tn), lambda i, j, k: ((i // mb) % 2, i // (2 * mb), i % mb, j))
    else:
        out_shape, out_spec = (m_dim, n), pl.BlockSpec((tm, tn), lambda i, j, k: (i, j))
    in_specs, operands = [a_spec, bspec], [a, b]
    if add is not None:
        in_specs.append(pl.BlockSpec((tm, tn), lambda i, j, k: (i, j)))
        operands.append(add)

    def body(*refs):
        a_ref, b_ref = refs[0], refs[1]
        add_ref = refs[2] if add is not None else None
        o_ref = refs[3] if add is not None else refs[2]

        def finish(r):
            if add_ref is not None:
                r = r + add_ref[...]
            o_ref[...] = r.astype(o_ref.dtype)

        if gk == 1:
            finish(_dot(a_ref[...], b_ref[...], dims))
        else:
            acc = refs[-1]
            k = pl.program_id(2)

            @pl.when(k == 0)
            def _():
                acc[...] = jnp.zeros_like(acc)

            acc[...] += _dot(a_ref[...], b_ref[...], dims)

            @pl.when(k == gk - 1)
            def _():
                finish(acc[...])

    return pl.pallas_call(
        body, name=name, grid=(gm, gn, gk), in_specs=in_specs, out_specs=out_spec,
        out_shape=jax.ShapeDtypeStruct(out_shape, out_dtype),
        scratch_shapes=[pltpu.VMEM((tm, tn), F32)] if gk > 1 else [],
        compiler_params=_params("parallel", "parallel", "arbitrary"),
    )(*operands)


def _mm3(a, b, dims, name, nc=None, tm=512):
    rows = a.shape[0]
    tm = min(tm, rows)
    gm = rows // tm
    if dims == TN:
        nc_ka, nc_nb = a.shape[1], b.shape[1]
        ka, nb = nc_ka // nc, nc_nb // nc

        def body(a_ref, b_ref, o_ref):
            @pl.when(pl.program_id(1) == 0)
            def _():
                o_ref[...] = jnp.zeros_like(o_ref)
            o_ref[...] += _dot3(a_ref[...], b_ref[...], TN)

        return pl.pallas_call(
            body, name=name, grid=(nc, gm),
            in_specs=[pl.BlockSpec((tm, ka), lambda c, i: (i, c)), pl.BlockSpec((tm, nb), lambda c, i: (i, c))],
            out_specs=pl.BlockSpec((None, ka, nb), lambda c, i: (c, 0, 0)),
            out_shape=jax.ShapeDtypeStruct((nc, ka, nb), F32),
            compiler_params=_params("parallel", "arbitrary"),
        )(a, b)
    nc, ka, nb = b.shape
    wa, wo = (ka, nb) if dims == NN else (nb, ka)

    def body(a_ref, b_ref, o_ref):
        o_ref[...] = _dot3(a_ref[...], b_ref[...], dims)

    return pl.pallas_call(
        body, name=name, grid=(nc, gm),
        in_specs=[pl.BlockSpec((tm, wa), lambda c, i: (i, c)), pl.BlockSpec((None, ka, nb), lambda c, i: (c, 0, 0))],
        out_specs=pl.BlockSpec((tm, wo), lambda c, i: (i, c)),
        out_shape=jax.ShapeDtypeStruct((rows, nc * wo), F32),
        compiler_params=_params("parallel", "parallel"),
    )(a, b)


def _rowwise(fn, name, rows, vecs=(), outs=(), sums=()):
    length = rows[0][0].shape[0]
    total = sum(w for _, w, _ in rows) + sum(w for w, _ in outs)
    tile = 8
    while tile * 2 <= min(length, 512) and tile * 2 * total <= ROW_TILE_ELEMS:
        tile *= 2
    assert length % tile == 0
    n_r, n_v, n_o = len(rows), len(vecs), len(outs)

    def body(*refs):
        vals = [r[...] for r in refs[:n_r + n_v]]
        o_refs = refs[n_r + n_v:n_r + n_v + n_o]
        s_refs = refs[n_r + n_v + n_o:]
        res_o, res_s = fn(*vals)
        for ref, val in zip(o_refs, res_o):
            ref[...] = val.astype(ref.dtype)
        if s_refs:
            @pl.when(pl.program_id(0) == 0)
            def _():
                for ref in s_refs:
                    ref[...] = jnp.zeros_like(ref)
            for ref, val in zip(s_refs, res_s):
                ref[...] += val

    def row_spec(w, cb):
        return pl.BlockSpec((tile, w), lambda i: (i, cb))

    res = pl.pallas_call(
        body, name=name, grid=(length // tile,),
        in_specs=[row_spec(w, cb) for _, w, cb in rows] + [pl.BlockSpec(v.shape, lambda i: (0, 0)) for v in vecs],
        out_specs=[row_spec(w, 0) for w, _ in outs] + [pl.BlockSpec((1, w), lambda i: (0, 0)) for w in sums],
        out_shape=[jax.ShapeDtypeStruct((length, w), dt) for w, dt in outs]
        + [jax.ShapeDtypeStruct((1, w), F32) for w in sums],
        compiler_params=_params("arbitrary" if sums else "parallel"),
    )(*[a for a, _, _ in rows], *vecs)
    return res[:n_o], res[n_o:]


def _elementwise(fn, name, arrays, out_dtypes):
    shape = arrays[0].shape
    cols = shape[-1]
    flat = [a.reshape(-1, cols) for a in arrays]
    rows = flat[0].shape[0]
    tile = 8
    while tile * 2 <= rows and rows % (tile * 2) == 0 and tile * 2 * cols * (len(arrays) + len(out_dtypes)) <= ROW_TILE_ELEMS:
        tile *= 2
    assert rows % tile == 0
    n_in = len(flat)

    def body(*refs):
        res = fn(*[r[...] for r in refs[:n_in]])
        for ref, val in zip(refs[n_in:], res):
            ref[...] = val.astype(ref.dtype)

    spec = pl.BlockSpec((tile, cols), lambda i: (i, 0))
    res = pl.pallas_call(
        body, name=name, grid=(rows // tile,), in_specs=[spec] * n_in, out_specs=[spec] * len(out_dtypes),
        out_shape=[jax.ShapeDtypeStruct((rows, cols), dt) for dt in out_dtypes],
        compiler_params=_params("parallel"),
    )(*flat)
    return [r.reshape(shape) for r in res]


def _whole(fn, name, arrays, out_shapes):
    n_in = len(arrays)

    def body(*refs):
        res = fn(*[r[...] for r in refs[:n_in]])
        for ref, val in zip(refs[n_in:], res):
            ref[...] = val

    return pl.pallas_call(
        body, name=name, out_shape=[jax.ShapeDtypeStruct(s, F32) for s in out_shapes],
        compiler_params=_params(),
    )(*arrays)


HBM_SPEC = pl.BlockSpec(memory_space=pltpu.HBM)


def _place():
    x, y, c = lax.axis_index("x"), lax.axis_index("y"), lax.axis_index("c")
    chips = [(1 - x, y), (x, 1 - y), (1 - x, 1 - y)]
    return x, y, c, chips


def _remote(src, dst, send_sem, recv_sem, target):
    return pltpu.make_async_remote_copy(src_ref=src, dst_ref=dst, send_sem=send_sem, recv_sem=recv_sem,
                                        device_id=target, device_id_type=MESH)


def _allgather_quarters(shards):
    n = len(shards)

    def body(*refs):
        ins, outs = refs[:n], refs[n:2 * n]
        send_sems, recv_sems, local_sems = refs[2 * n:]
        x, y, c, chips = _place()
        mine, sibling = 2 * x + y, (x, y, 1 - c)
        local = [pltpu.make_async_copy(ins[t], outs[t].at[mine], local_sems.at[t]) for t in range(n)]
        first, passed = [], []
        for t in range(n):
            local[t].start()
            for k, (cx, cy) in enumerate(chips):
                cp = _remote(ins[t].at[c], outs[t].at[mine, c], send_sems.at[t, k], recv_sems.at[t, k], (cx, cy, c))
                cp.start()
                first.append(cp)
        for t in range(n):
            for k, (cx, cy) in enumerate(chips):
                landed = outs[t].at[2 * cx + cy, c]
                _remote(landed, landed, send_sems.at[t, k], recv_sems.at[t, k], (cx, cy, c)).wait_recv()
                cp = _remote(landed, landed, send_sems.at[t, 3 + k], recv_sems.at[t, 3 + k], sibling)
                cp.start()
                passed.append(cp)
        for t in range(n):
            for k, (cx, cy) in enumerate(chips):
                got = outs[t].at[2 * cx + cy, 1 - c]
                _remote(got, got, send_sems.at[t, 3 + k], recv_sems.at[t, 3 + k], sibling).wait_recv()
        for cp in first + passed:
            cp.wait_send()
        for cp in local:
            cp.wait()

    return pl.pallas_call(
        body, name="allgather_weights",
        in_specs=[HBM_SPEC] * n, out_specs=[HBM_SPEC] * n,
        out_shape=[jax.ShapeDtypeStruct((4,) + s.shape, s.dtype) for s in shards],
        scratch_shapes=[pltpu.SemaphoreType.DMA((n, 6)), pltpu.SemaphoreType.DMA((n, 6)), pltpu.SemaphoreType.DMA((n,))],
        compiler_params=pltpu.CompilerParams(has_side_effects=True),
    )(*shards)


def _to_sibling(arrays, name):
    n = len(arrays)

    def body(*refs):
        ins, outs = refs[:n], refs[n:2 * n]
        send_sems, recv_sems = refs[2 * n:]
        x, y, c, _ = _place()
        copies = [_remote(ins[t], outs[t], send_sems.at[t], recv_sems.at[t], (x, y, 1 - c)) for t in range(n)]
        for cp in copies:
            cp.start()
        for cp in copies:
            cp.wait()

    return pl.pallas_call(
        body, name=name, in_specs=[HBM_SPEC] * n, out_specs=[HBM_SPEC] * n,
        out_shape=[jax.ShapeDtypeStruct(a.shape, a.dtype) for a in arrays],
        scratch_shapes=[pltpu.SemaphoreType.DMA((n,)), pltpu.SemaphoreType.DMA((n,))],
        compiler_params=pltpu.CompilerParams(has_side_effects=True),
    )(*arrays)


def _to_owner_chips(arrays):
    n = len(arrays)

    def body(*refs):
        ins, outs = refs[:n], refs[n:2 * n]
        send_sems, recv_sems = refs[2 * n:]
        x, y, c, chips = _place()
        copies = []
        for t in range(n):
            for k, (cx, cy) in enumerate(chips):
                cp = _remote(ins[t].at[2 * cx + cy], outs[t].at[k], send_sems.at[t, k], recv_sems.at[t, k], (cx, cy, c))
                cp.start()
                copies.append(cp)
        for cp in copies:
            cp.wait()

    return pl.pallas_call(
        body, name="grads_to_owner_chips", in_specs=[HBM_SPEC] * n, out_specs=[HBM_SPEC] * n,
        out_shape=[jax.ShapeDtypeStruct((3,) + a.shape[1:], a.dtype) for a in arrays],
        scratch_shapes=[pltpu.SemaphoreType.DMA((n, 3)), pltpu.SemaphoreType.DMA((n, 3))],
        compiler_params=pltpu.CompilerParams(has_side_effects=True),
    )(*arrays)


def _share_halves(arrays):
    n = len(arrays)

    def body(*refs):
        ins, outs = refs[:n], refs[n:2 * n]
        send_sems, recv_sems, local_sems = refs[2 * n:]
        x, y, c, _ = _place()
        local = [pltpu.make_async_copy(ins[t], outs[t].at[c], local_sems.at[t]) for t in range(n)]
        sends = [_remote(ins[t], outs[t].at[c], send_sems.at[t], recv_sems.at[t], (x, y, 1 - c)) for t in range(n)]
        for cp in local + sends:
            cp.start()
        for t in range(n):
            got = outs[t].at[1 - c]
            _remote(got, got, send_sems.at[t], recv_sems.at[t], (x, y, 1 - c)).wait_recv()
        for cp in sends:
            cp.wait_send()
        for cp in local:
            cp.wait()

    return pl.pallas_call(
        body, name="share_reduced_halves", in_specs=[HBM_SPEC] * n, out_specs=[HBM_SPEC] * n,
        out_shape=[jax.ShapeDtypeStruct((2,) + a.shape, a.dtype) for a in arrays],
        scratch_shapes=[pltpu.SemaphoreType.DMA((n,)), pltpu.SemaphoreType.DMA((n,)), pltpu.SemaphoreType.DMA((n,))],
        compiler_params=pltpu.CompilerParams(has_side_effects=True),
    )(*arrays)


def _allreduce_small(flat):
    rows = flat.shape[0]

    def body(in_ref, out_ref, all_ref, send_sems, recv_sems):
        x, y, c, _ = _place()
        me = 4 * x + 2 * y + c
        all_ref[me] = in_ref[...]
        copies = []
        for r in range(1, 8):
            peer = (x ^ (r >> 2), y ^ ((r >> 1) & 1), c ^ (r & 1))
            cp = _remote(in_ref, all_ref.at[me], send_sems.at[r - 1], recv_sems.at[r - 1], peer)
            cp.start()
            copies.append(cp)
        for r in range(1, 8):
            slot = all_ref.at[me ^ r]
            _remote(slot, slot, send_sems.at[r - 1], recv_sems.at[r - 1], (x, y, c)).wait_recv()
        for cp in copies:
            cp.wait_send()
        acc = all_ref[0]
        for d in range(1, 8):
            acc = acc + all_ref[d]
        out_ref[...] = acc

    return pl.pallas_call(
        body, name="allreduce_small_grads",
        in_specs=[pl.BlockSpec(memory_space=pltpu.VMEM)], out_specs=pl.BlockSpec(memory_space=pltpu.VMEM),
        out_shape=jax.ShapeDtypeStruct(flat.shape, F32),
        scratch_shapes=[pltpu.VMEM((8, rows, LANES), F32), pltpu.SemaphoreType.DMA((7,)), pltpu.SemaphoreType.DMA((7,))],
        compiler_params=pltpu.CompilerParams(has_side_effects=True, vmem_limit_bytes=VMEM_LIMIT_BYTES),
    )(flat)


def _pool_tile(length):
    return min(256, length)


def _pool_fwd(proj, w_pool):
    length = proj.shape[0]
    ngroups, ch, _ = w_pool.shape
    width, tile = ngroups * ch, _pool_tile(length)

    def body(cur_ref, prev_ref, w_ref, pooled_ref, mixed_ref):
        i = pl.program_id(0)
        cur = cur_ref[...]
        tail = jnp.where(i > 0, prev_ref[tile - POOL_HALO:tile, :], 0.0)
        padded = jnp.concatenate([tail, cur], axis=0)
        pos = (lax.broadcasted_iota(jnp.int32, (tile, 1), 0) + i * tile + 1).astype(F32)
        for g, window in enumerate(POOL_WINDOWS):
            cols = slice(g * ch, (g + 1) * ch)
            run, shift = padded[:, cols], 1
            while shift < window:
                run = run + pltpu.roll(run, shift, 0)
                shift *= 2
            pooled = (run[POOL_HALO:, :] / jnp.minimum(pos, float(window)) - cur[:, cols]).astype(BF16)
            pooled_ref[:, cols] = pooled
            mixed_ref[:, cols] = _dot(pooled, w_ref[g])

    return pl.pallas_call(
        body, name="pool_fwd", grid=(length // tile,),
        in_specs=[pl.BlockSpec((tile, width), lambda i: (i, 0)),
                  pl.BlockSpec((tile, width), lambda i: (jnp.maximum(i - 1, 0), 0)),
                  pl.BlockSpec(w_pool.shape, lambda i: (0, 0, 0))],
        out_specs=[pl.BlockSpec((tile, width), lambda i: (i, 0))] * 2,
        out_shape=[jax.ShapeDtypeStruct((length, width), BF16), jax.ShapeDtypeStruct((length, width), F32)],
        compiler_params=_params("parallel"),
    )(proj, proj, w_pool)


def _pool_bwd_mix(d_mixed, pooled, w_pool):
    length, width = d_mixed.shape
    ngroups, ch, _ = w_pool.shape
    tile = _pool_tile(length)

    def body(dm_ref, pooled_ref, w_ref, dp_ref, dw_ref):
        @pl.when(pl.program_id(0) == 0)
        def _():
            dw_ref[...] = jnp.zeros_like(dw_ref)
        for g in range(ngroups):
            cols = slice(g * ch, (g + 1) * ch)
            dm = dm_ref[:, cols].astype(BF16)
            dp_ref[:, cols] = _dot(dm, w_ref[g], NT)
            dw_ref[g] += _dot(pooled_ref[:, cols], dm, TN)

    return pl.pallas_call(
        body, name="pool_bwd_mix", grid=(length // tile,),
        in_specs=[pl.BlockSpec((tile, width), lambda i: (i, 0)), pl.BlockSpec((tile, width), lambda i: (i, 0)),
                  pl.BlockSpec(w_pool.shape, lambda i: (0, 0, 0))],
        out_specs=[pl.BlockSpec((tile, width), lambda i: (i, 0)), pl.BlockSpec(w_pool.shape, lambda i: (0, 0, 0))],
        out_shape=[jax.ShapeDtypeStruct((length, width), F32), jax.ShapeDtypeStruct(w_pool.shape, F32)],
        compiler_params=_params("arbitrary"),
    )(d_mixed, pooled, w_pool)


def _pool_bwd_window(d_pooled, ngroups):
    length, width = d_pooled.shape
    ch, tile = width // ngroups, _pool_tile(length)
    last = length // tile - 1

    def body(cur_ref, next_ref, dx_ref):
        i = pl.program_id(0)
        cur = cur_ref[...]
        head = jnp.where(i < last, next_ref[0:POOL_HALO, :], 0.0)
        padded = jnp.concatenate([cur, head], axis=0)
        rows = tile + POOL_HALO
        pos = (lax.broadcasted_iota(jnp.int32, (rows, 1), 0) + i * tile + 1).astype(F32)
        for g, window in enumerate(POOL_WINDOWS):
            cols = slice(g * ch, (g + 1) * ch)
            run, shift = padded[:, cols] / jnp.minimum(pos, float(window)), 1
            while shift < window:
                run = run + pltpu.roll(run, rows - shift, 0)
                shift *= 2
            dx_ref[:, cols] = (run[0:tile, :] - cur[:, cols]).astype(BF16)

    return pl.pallas_call(
        body, name="pool_bwd_window", grid=(length // tile,),
        in_specs=[pl.BlockSpec((tile, width), lambda i: (i, 0)),
                  pl.BlockSpec((tile, width), lambda i: (jnp.minimum(i + 1, last), 0))],
        out_specs=pl.BlockSpec((tile, width), lambda i: (i, 0)),
        out_shape=jax.ShapeDtypeStruct((length, width), BF16),
        compiler_params=_params("parallel"),
    )(d_pooled, d_pooled)


ATTN_TILE = 256


def _stick_weights(q, kc, upper, run_log, mask):
    z = _dot(q, kc, NT)
    e = jnp.exp(-jnp.abs(z))
    softplus = jnp.maximum(z, 0.0) + jnp.log(1.0 + e)
    log_sig = z - softplus
    log_1m = -softplus if mask is None else jnp.where(mask, -softplus, 0.0)
    hi, lo = _split(log_1m)
    suffix = _dot(hi, upper) + _dot(lo, upper) + run_log
    w = jnp.exp(log_sig + suffix)
    if mask is not None:
        w = jnp.where(mask, w, 0.0)
    return w, log_sig, suffix[:, 0:1] + log_1m[:, 0:1]


def _attn_consts(tile):
    jj = lax.broadcasted_iota(jnp.int32, (tile, tile), 0)
    ss = lax.broadcasted_iota(jnp.int32, (tile, tile), 1)
    return (jj > ss).astype(BF16), (jj >= ss).astype(BF16), ss < jj


def _attn_fwd(proj, n_heads, q_blk, k_blk, v_blk):
    length = proj.shape[0]
    tile = min(ATTN_TILE, length)
    scale = HEAD_DIM ** -0.5

    def body(q_ref, k_ref, v_ref, o_ref):
        i = pl.program_id(1)
        q = (q_ref[...] * scale).astype(BF16)
        upper, _, diag_mask = _attn_consts(tile)

        def chunk(j, carry, mask):
            run_log, acc = carry
            start = pl.multiple_of(j * tile, tile)
            kc = k_ref[pl.ds(start, tile), :].astype(BF16)
            vc = v_ref[pl.ds(start, tile), :].astype(BF16)
            w, _, run_log = _stick_weights(q, kc, upper, run_log, mask)
            return run_log, acc + _dot(w.astype(BF16), vc)

        carry = (jnp.zeros((tile, 1), F32), jnp.zeros((tile, HEAD_DIM), F32))
        carry = chunk(i, carry, diag_mask)
        carry = lax.fori_loop(0, i, lambda it, cr: chunk(i - 1 - it, cr, None), carry)
        o_ref[...] = carry[1]

    return pl.pallas_call(
        body, name="attn_fwd", grid=(n_heads, length // tile),
        in_specs=[pl.BlockSpec((tile, HEAD_DIM), lambda h, i: (i, q_blk + h)),
                  pl.BlockSpec((length, HEAD_DIM), lambda h, i: (0, k_blk + h)),
                  pl.BlockSpec((length, HEAD_DIM), lambda h, i: (0, v_blk + h))],
        out_specs=pl.BlockSpec((tile, HEAD_DIM), lambda h, i: (i, h)),
        out_shape=jax.ShapeDtypeStruct((length, n_heads * HEAD_DIM), F32),
        compiler_params=_params("parallel", "parallel"),
    )(proj, proj, proj)


def _attn_bwd(proj, out, d_out, n_heads, q_blk, k_blk, v_blk):
    length = proj.shape[0]
    tile = min(ATTN_TILE, length)
    scale = HEAD_DIM ** -0.5

    def body(q_ref, k_ref, v_ref, o_ref, do_ref, dq_ref, dk_ref, dv_ref):
        i = pl.program_id(1)

        @pl.when(i == 0)
        def _():
            dk_ref[...] = jnp.zeros_like(dk_ref)
            dv_ref[...] = jnp.zeros_like(dv_ref)

        q = (q_ref[...] * scale).astype(BF16)
        do = do_ref[...].astype(BF16)
        total = jnp.sum(do.astype(F32) * o_ref[...], axis=1, keepdims=True)
        upper, upper_incl, diag_mask = _attn_consts(tile)

        def chunk(j, carry, mask):
            run_log, run_g, dq = carry
            start = pl.multiple_of(j * tile, tile)
            kc = k_ref[pl.ds(start, tile), :].astype(BF16)
            vc = v_ref[pl.ds(start, tile), :].astype(BF16)
            w, log_sig, run_log = _stick_weights(q, kc, upper, run_log, mask)
            wb = w.astype(BF16)
            g = wb.astype(F32) * _dot(do, vc, NT)
            g_hi, g_lo = _split(g)
            g_suffix = _dot(g_hi, upper_incl) + _dot(g_lo, upper_incl) + run_g
            dz = g - jnp.exp(log_sig) * (g + (total - g_suffix))
            if mask is not None:
                dz = jnp.where(mask, dz, 0.0)
            dzb = dz.astype(BF16)
            dk_ref[pl.ds(start, tile), :] += _dot(dzb, q, TN)
            dv_ref[pl.ds(start, tile), :] += _dot(wb, do, TN)
            return run_log, g_suffix[:, 0:1], dq + _dot(dzb, kc)

        carry = (jnp.zeros((tile, 1), F32), jnp.zeros((tile, 1), F32), jnp.zeros((tile, HEAD_DIM), F32))
        carry = chunk(i, carry, diag_mask)
        carry = lax.fori_loop(0, i, lambda it, cr: chunk(i - 1 - it, cr, None), carry)
        dq_ref[...] = (carry[2] * scale).astype(BF16)

    width = n_heads * HEAD_DIM
    tile_spec = pl.BlockSpec((tile, HEAD_DIM), lambda h, i: (i, h))
    head_spec = pl.BlockSpec((length, HEAD_DIM), lambda h, i: (0, h))
    return pl.pallas_call(
        body, name="attn_bwd", grid=(n_heads, length // tile),
        in_specs=[pl.BlockSpec((tile, HEAD_DIM), lambda h, i: (i, q_blk + h)),
                  pl.BlockSpec((length, HEAD_DIM), lambda h, i: (0, k_blk + h)),
                  pl.BlockSpec((length, HEAD_DIM), lambda h, i: (0, v_blk + h)),
                  tile_spec, tile_spec],
        out_specs=[tile_spec, head_spec, head_spec],
        out_shape=[jax.ShapeDtypeStruct((length, width), BF16), jax.ShapeDtypeStruct((length, width), F32),
                   jax.ShapeDtypeStruct((length, width), F32)],
        compiler_params=_params("parallel", "arbitrary"),
    )(proj, proj, proj, out, d_out)


SCAN_CHUNK = 128


def _disc_lam(lam_re, lam_im, log_dt):
    dt = jnp.exp(log_dt)
    mag, phase = jnp.exp(lam_re * dt), lam_im * dt
    bar_re, bar_im = mag * jnp.cos(phase), mag * jnp.sin(phase)
    num_re, den = bar_re - 1.0, lam_re * lam_re + lam_im * lam_im
    return (bar_re, bar_im, (num_re * lam_re + bar_im * lam_im) / den, (bar_im * lam_re - num_re * lam_im) / den)


def _disc_b(cf_re, cf_im, b_re, b_im):
    return cf_re * b_re - cf_im * b_im, cf_re * b_im + cf_im * b_re


def _halves(re, im):
    return jnp.concatenate([re, im], axis=1)


def _scan_fwd(bu, bar_re, bar_im):
    length, groups, width = bu.shape
    chunk = min(SCAN_CHUNK, length)

    def body(bu_ref, re_ref, im_ref, st_ref, carry):
        @pl.when(pl.program_id(0) == 0)
        def _():
            carry[...] = jnp.zeros_like(carry)
        a_same = _halves(re_ref[...], re_ref[...])
        a_swap = _halves(-im_ref[...], im_ref[...])

        def step(t, x):
            x = a_same * x + a_swap * pltpu.roll(x, width // 2, 1) + bu_ref[t]
            st_ref[t] = x
            return x

        carry[...] = lax.fori_loop(0, chunk, step, carry[...], unroll=8)

    blk = pl.BlockSpec((chunk, groups, width), lambda i: (i, 0, 0))
    par = pl.BlockSpec(bar_re.shape, lambda i: (0, 0))
    return pl.pallas_call(
        body, name="s5_scan_fwd", grid=(length // chunk,), in_specs=[blk, par, par], out_specs=blk,
        out_shape=jax.ShapeDtypeStruct(bu.shape, F32), scratch_shapes=[pltpu.VMEM((groups, width), F32)],
        compiler_params=_params("arbitrary"),
    )(bu, bar_re, bar_im)


def _scan_bwd(d_states, states, bar_re, bar_im):
    length, groups, width = states.shape
    chunk = min(SCAN_CHUNK, length)
    last = length // chunk - 1

    def body(g_ref, st_ref, re_ref, im_ref, out_ref, same_ref, swap_ref, carry):
        @pl.when(pl.program_id(0) == 0)
        def _():
            carry[...] = jnp.zeros_like(carry)
            same_ref[...] = jnp.zeros_like(same_ref)
            swap_ref[...] = jnp.zeros_like(swap_ref)
        a_same = _halves(re_ref[...], re_ref[...])
        a_swap = _halves(im_ref[...], -im_ref[...])

        def step(k, cr):
            adj, acc_same, acc_swap = cr
            t = chunk - 1 - k
            s = st_ref[t]
            acc_same = acc_same + adj * s
            acc_swap = acc_swap + adj * pltpu.roll(s, width // 2, 1)
            adj = g_ref[t] + a_same * adj + a_swap * pltpu.roll(adj, width // 2, 1)
            out_ref[t] = adj
            return adj, acc_same, acc_swap

        adj, acc_same, acc_swap = lax.fori_loop(0, chunk, step, (carry[...], same_ref[...], swap_ref[...]), unroll=4)
        carry[...] = adj
        same_ref[...] = acc_same
        swap_ref[...] = acc_swap

    blk = pl.BlockSpec((chunk, groups, width), lambda i: (last - i, 0, 0))
    par = pl.BlockSpec(bar_re.shape, lambda i: (0, 0))
    acc = pl.BlockSpec((groups, width), lambda i: (0, 0))
    return pl.pallas_call(
        body, name="s5_scan_bwd", grid=(length // chunk,), in_specs=[blk, blk, par, par], out_specs=[blk, acc, acc],
        out_shape=[jax.ShapeDtypeStruct(states.shape, F32), jax.ShapeDtypeStruct((groups, width), F32),
                   jax.ShapeDtypeStruct((groups, width), F32)],
        scratch_shapes=[pltpu.VMEM((groups, width), F32)],
        compiler_params=_params("arbitrary"),
    )(d_states, states, bar_re, bar_im)


def _block_diag(per_group):
    groups, a, b = per_group.shape
    nc = groups // GROUPS_PER_CHUNK
    eye = jnp.eye(GROUPS_PER_CHUNK, dtype=per_group.dtype)
    x = per_group.reshape(nc, GROUPS_PER_CHUNK, a, 1, b) * eye[None, :, None, :, None]
    return x.reshape(nc, GROUPS_PER_CHUNK * a, GROUPS_PER_CHUNK * b)


def _block_diag_part(chunks, a, b):
    nc = chunks.shape[0]
    x = chunks.reshape(nc, GROUPS_PER_CHUNK, a, GROUPS_PER_CHUNK, b)
    x = jnp.diagonal(x, axis1=1, axis2=3)
    return jnp.moveaxis(x, 3, 1).reshape(nc * GROUPS_PER_CHUNK, a, b)


def _epilogue(raw, gate, scale, g):
    return _rms(raw * scale, g) * (gate * _sigmoid(gate))


def _ssm_mid(y, u, d_skip):
    return _gelu(y + d_skip * u)


def _adamw(w, g, m, v):
    m = ADAM_B1 * m + (1.0 - ADAM_B1) * g
    v = ADAM_B2 * v + (1.0 - ADAM_B2) * (g * g)
    m_hat = m / (1.0 - ADAM_B1 ** ADAM_STEP)
    v_hat = v / (1.0 - ADAM_B2 ** ADAM_STEP)
    return -ADAM_LR * (m_hat / (jnp.sqrt(v_hat) + ADAM_EPS) + ADAM_WD * w), m, v


class _Dims:
    def __init__(self, d_model, length):
        self.d, self.length = d_model, length
        self.d_pool, self.d_attn = d_model // 4, d_model // 2
        self.d_ssm = d_model - self.d_pool - self.d_attn
        self.heads = self.d_attn // HEAD_DIM
        self.groups = self.d_ssm // SSM_GROUP
        self.d_in = 2 * self.d_pool + 4 * self.d_attn + 2 * self.d_ssm
        sizes = (self.d_pool, self.d_pool, self.d_attn, self.d_attn, self.d_attn, self.d_attn, self.d_ssm, self.d_ssm)
        offs = [0]
        for s in sizes[:-1]:
            offs.append(offs[-1] + s)
        (self.o_px, self.o_pgate, self.o_q, self.o_k, self.o_v, self.o_agate, self.o_u, self.o_sgate) = offs


def _ssm_operands(dm, p):
    groups, states = dm.groups, SSM_STATE
    bar_re, bar_im, cf_re, cf_im = _whole(_disc_lam, "s5_disc_lam", [p["lam_re"], p["lam_im"], p["log_dt"].reshape(groups, 1)],
                                          [(groups, states)] * 4)
    b_re2, b_im2 = p["b_re"].reshape(groups * states, SSM_GROUP), p["b_im"].reshape(groups * states, SSM_GROUP)
    bb_re, bb_im = _whole(_disc_b, "s5_disc_b", [cf_re.reshape(-1, 1), cf_im.reshape(-1, 1), b_re2, b_im2],
                          [(groups * states, SSM_GROUP)] * 2)
    bcat = jnp.concatenate([bb_re.reshape(groups, states, SSM_GROUP), bb_im.reshape(groups, states, SSM_GROUP)], axis=1)
    b_blk = _block_diag(jnp.swapaxes(bcat, 1, 2))
    ccat = jnp.concatenate([p["c_re"], -p["c_im"]], axis=2)
    c_blk = _block_diag(jnp.swapaxes(ccat, 1, 2))
    return dict(bar_re=bar_re, bar_im=bar_im, cf_re=cf_re, cf_im=cf_im, b_re2=b_re2, b_im2=b_im2, b_blk=b_blk, c_blk=c_blk)


def _layer_fwd(dm, x_in, p, gw):
    length = dm.length
    blk = lambda off, w: off // w
    (h,), _ = _rowwise(lambda x, g: ((_rms(x, g),), ()), "rms_fwd", [(x_in, dm.d, 0)], [p["ln_g"]], [(dm.d, BF16)])
    proj = _mm(h, gw["w_in"], NN, F32, "in_proj", n=dm.d_in, b_quarters="n")
    pooled, mixed = _pool_fwd(proj, gw["w_pool"])
    qb, kb, vb = dm.o_q // HEAD_DIM, dm.o_k // HEAD_DIM, dm.o_v // HEAD_DIM
    attn = _attn_fwd(proj, dm.heads, qb, kb, vb)
    so = _ssm_operands(dm, p)
    u_row = (proj, dm.d_ssm, blk(dm.o_u, dm.d_ssm))
    (u,), _ = _rowwise(lambda v: ((v,), ()), "take_u", [u_row], [], [(dm.d_ssm, F32)])
    bu = _mm3(u, so["b_blk"], NN, "s5_bu")
    states3 = _scan_fwd(bu.reshape(length, dm.groups, 2 * SSM_STATE), so["bar_re"], so["bar_im"])
    states = states3.reshape(length, dm.groups * 2 * SSM_STATE)
    y = _mm3(states, so["c_blk"], NN, "s5_y")
    (hg,), _ = _rowwise(lambda yy, uu, dsk: ((_ssm_mid(yy, uu, dsk),), ()), "s5_mid_fwd",
                        [(y, dm.d_ssm, 0), (u, dm.d_ssm, 0)], [p["d_skip"]], [(dm.d_ssm, BF16)])
    z = _mm(hg, gw["w_glu"], NN, F32, "glu_proj", n=2 * dm.d_ssm, b_quarters="n")

    def glu(zz, bias):
        zz = zz + bias
        return (zz[:, :dm.d_ssm] * _sigmoid(zz[:, dm.d_ssm:]),), ()

    (ssm,), _ = _rowwise(glu, "glu_fwd", [(z, 2 * dm.d_ssm, 0)], [p["b_glu"]], [(dm.d_ssm, F32)])
    g_pool, g_attn, g_ssm = (p["branch_g"][:, :dm.d_pool], p["branch_g"][:, dm.d_pool:dm.d_pool + dm.d_attn],
                             p["branch_g"][:, dm.d_pool + dm.d_attn:])
    ones_attn, ones_ssm = jnp.ones((1, dm.d_attn), F32), jnp.ones((1, dm.d_ssm), F32)
    epi = lambda raw, gate, scale, g: ((_epilogue(raw, gate, scale, g),), ())
    branches = [("pool", mixed, dm.d_pool, dm.o_pgate, p["pool_scale"], g_pool),
                ("attn", attn, dm.d_attn, dm.o_agate, ones_attn, g_attn),
                ("ssm", ssm, dm.d_ssm, dm.o_sgate, ones_ssm, g_ssm)]
    ys = []
    for nm, raw, w, off, scale, g in branches:
        (yb,), _ = _rowwise(epi, "epilogue_fwd_" + nm, [(raw, w, 0), (proj, w, blk(off, w))], [scale, g], [(w, BF16)])
        ys.append(yb)
    y_cat = jnp.concatenate(ys, axis=1)
    x_out = _mm(y_cat, gw["w_out"], NN, F32, "out_proj", add=x_in)
    saved = dict(x_in=x_in, h=h, proj=proj, pooled=pooled, mixed=mixed, attn=attn, so=so, u=u, states3=states3, states=states,
                 y=y, hg=hg, z=z, ssm=ssm, y_cat=y_cat, scales=(p["pool_scale"], ones_attn, ones_ssm), gs=(g_pool, g_attn, g_ssm))
    return x_out, saved


def _layer_bwd(dm, d_out, d_out_bf, p, gw, sv, want_bf):
    length = dm.length
    blk = lambda off, w: off // w
    proj = sv["proj"]
    d_y = _mm(d_out_bf, gw["w_out"], NT, F32, "out_proj_dgrad")
    g_w_out = _mm(sv["y_cat"], d_out_bf, TN, F32, "out_proj_wgrad", out_quarters="rows")

    def epi_bwd(nseg):
        def fn(*vals):
            dys, (raw, gate, scale, g) = vals[:nseg], vals[nseg:]
            dyb = dys[0] if nseg == 1 else jnp.concatenate(dys, axis=1)
            _, vjp = jax.vjp(_epilogue, raw, gate, scale, g)
            d_raw, d_gate, d_scale, d_g = vjp(dyb)
            return (d_raw, d_gate), (d_scale, d_g)
        return fn

    branch = [("pool", sv["mixed"], dm.d_pool, dm.o_pgate, 0), ("attn", sv["attn"], dm.d_attn, dm.o_agate, dm.d_pool),
              ("ssm", sv["ssm"], dm.d_ssm, dm.o_sgate, dm.d_pool + dm.d_attn)]
    d_raws, d_gates, d_scales, d_gs = [], [], [], []
    for (nm, raw, w, off, yoff), scale, g in zip(branch, sv["scales"], sv["gs"]):
        seg = math.gcd(w, yoff) if yoff else w
        dy_rows = [(d_y, seg, yoff // seg + s) for s in range(w // seg)]
        (d_raw, d_gate), (d_scale, d_g) = _rowwise(
            epi_bwd(len(dy_rows)), "epilogue_bwd_" + nm, dy_rows + [(raw, w, 0), (proj, w, blk(off, w))], [scale, g],
            [(w, F32), (w, BF16)], [w, w])
        d_raws.append(d_raw); d_gates.append(d_gate); d_scales.append(d_scale); d_gs.append(d_g)
    d_pooled, g_w_pool = _pool_bwd_mix(d_raws[0], sv["pooled"], gw["w_pool"])
    d_px = _pool_bwd_window(d_pooled, len(POOL_WINDOWS))
    qb, kb, vb = dm.o_q // HEAD_DIM, dm.o_k // HEAD_DIM, dm.o_v // HEAD_DIM
    d_q, d_k, d_v = _attn_bwd(proj, sv["attn"], d_raws[1], dm.heads, qb, kb, vb)
    so = sv["so"]

    def glu_bwd(d_ssm, zz, bias):
        zz = zz + bias
        val, sg = zz[:, :dm.d_ssm], _sigmoid(zz[:, dm.d_ssm:])
        dz = jnp.concatenate([d_ssm * sg, d_ssm * val * sg * (1.0 - sg)], axis=1)
        return (dz,), (jnp.sum(dz, axis=0, keepdims=True),)

    (d_z,), (g_b_glu,) = _rowwise(glu_bwd, "glu_bwd", [(d_raws[2], dm.d_ssm, 0), (sv["z"], 2 * dm.d_ssm, 0)], [p["b_glu"]],
                                  [(2 * dm.d_ssm, BF16)], [2 * dm.d_ssm])
    d_hg = _mm(d_z, gw["w_glu"], NT, F32, "glu_dgrad", n=dm.d_ssm, b_quarters="k")
    g_w_glu = _mm(sv["hg"], d_z, TN, F32, "glu_wgrad", out_quarters="cols")

    def mid_bwd(dh, yy, uu, dsk):
        _, vjp = jax.vjp(_ssm_mid, yy, uu, dsk)
        dy_, du_, ddsk = vjp(dh)
        return (dy_, du_), (ddsk,)

    (d_yssm, d_u_direct), (g_d_skip,) = _rowwise(mid_bwd, "s5_mid_bwd", [(d_hg, dm.d_ssm, 0), (sv["y"], dm.d_ssm, 0), (sv["u"], dm.d_ssm, 0)],
                                                 [p["d_skip"]], [(dm.d_ssm, F32), (dm.d_ssm, F32)], [dm.d_ssm])
    d_states = _mm3(d_yssm, so["c_blk"], NT, "s5_y_dgrad")
    d_c_blk = _mm3(sv["states"], d_yssm, TN, "s5_y_wgrad", nc=so["c_blk"].shape[0])
    d_bu3, acc_same, acc_swap = _scan_bwd(d_states.reshape(length, dm.groups, 2 * SSM_STATE), sv["states3"], so["bar_re"], so["bar_im"])
    d_bu = d_bu3.reshape(length, dm.groups * 2 * SSM_STATE)
    d_u_scan = _mm3(d_bu, so["b_blk"], NT, "s5_bu_dgrad")
    d_b_blk = _mm3(sv["u"], d_bu, TN, "s5_bu_wgrad", nc=so["b_blk"].shape[0])
    (d_u,) = _elementwise(lambda a, b: (a + b,), "s5_du", [d_u_direct, d_u_scan], [BF16])
    groups, states = dm.groups, SSM_STATE
    d_ccat = jnp.swapaxes(_block_diag_part(d_c_blk, 2 * states, SSM_GROUP), 1, 2)
    d_bcat = jnp.swapaxes(_block_diag_part(d_b_blk, SSM_GROUP, 2 * states), 1, 2)
    d_bb_re, d_bb_im = d_bcat[:, :states].reshape(-1, SSM_GROUP), d_bcat[:, states:].reshape(-1, SSM_GROUP)

    def disc_b_bwd(cf_re, cf_im, b_re, b_im, g_re, g_im):
        _, vjp = jax.vjp(_disc_b, cf_re, cf_im, b_re, b_im)
        return vjp((g_re, g_im))

    d_cf_re, d_cf_im, g_b_re, g_b_im = _whole(
        disc_b_bwd, "s5_disc_b_bwd", [so["cf_re"].reshape(-1, 1), so["cf_im"].reshape(-1, 1), so["b_re2"], so["b_im2"], d_bb_re, d_bb_im],
        [(groups * states, 1)] * 2 + [(groups * states, SSM_GROUP)] * 2)

    def disc_lam_bwd(lam_re, lam_im, log_dt, same, swap, g_cf_re, g_cf_im):
        g_bar_re = same[:, :states] + same[:, states:]
        g_bar_im = swap[:, states:] - swap[:, :states]
        _, vjp = jax.vjp(_disc_lam, lam_re, lam_im, log_dt)
        return vjp((g_bar_re, g_bar_im, g_cf_re, g_cf_im))

    g_lam_re, g_lam_im, g_log_dt = _whole(
        disc_lam_bwd, "s5_disc_lam_bwd", [p["lam_re"], p["lam_im"], p["log_dt"].reshape(groups, 1), acc_same, acc_swap,
                                          d_cf_re.reshape(groups, states), d_cf_im.reshape(groups, states)],
        [(groups, states)] * 2 + [(groups, 1)])
    (g_c_im,) = _elementwise(lambda a: (-a,), "s5_neg_c_im", [d_ccat[:, :, states:].reshape(groups * SSM_GROUP, states)], [F32])
    d_proj = jnp.concatenate([d_px, d_gates[0], d_q, d_k.astype(BF16), d_v.astype(BF16), d_gates[1], d_u, d_gates[2]], axis=1)
    d_h = _mm(d_proj, gw["w_in"], NT, F32, "in_proj_dgrad", n=dm.d, b_quarters="k")
    g_w_in = _mm(sv["h"], d_proj, TN, F32, "in_proj_wgrad", out_quarters="cols")

    def rms_bwd(dh, xx, dres, g):
        _, vjp = jax.vjp(_rms, xx, g)
        dx, dg = vjp(dh)
        dx = dx + dres
        return ((dx, dx) if want_bf else (dx,)), (dg,)

    d_xs, (g_ln_g,) = _rowwise(rms_bwd, "rms_bwd", [(d_h, dm.d, 0), (sv["x_in"], dm.d, 0), (d_out, dm.d, 0)], [p["ln_g"]],
                               [(dm.d, F32), (dm.d, BF16)] if want_bf else [(dm.d, F32)], [dm.d])
    ngr, ch = gw["w_pool"].shape[0], gw["w_pool"].shape[1]
    q_rows = ch // 4
    g_w_pool2 = g_w_pool.reshape(ngr, 4, 2, q_rows // 2, ch).transpose(2, 1, 0, 3, 4).reshape(2, 4, ngr * q_rows // 2, ch)
    big = dict(w_in=g_w_in, w_out=g_w_out, w_glu=g_w_glu, w_pool=g_w_pool2)
    small = dict(ln_g=g_ln_g, pool_scale=d_scales[0], lam_re=g_lam_re, lam_im=g_lam_im, log_dt=g_log_dt.reshape(1, groups),
                 b_re=g_b_re, b_im=g_b_im, c_re=d_ccat[:, :, :states], c_im=g_c_im, d_skip=g_d_skip, b_glu=g_b_glu,
                 branch_g=jnp.concatenate(d_gs, axis=1))
    return d_xs[0], (d_xs[1] if want_bf else None), big, small


SMALL_ROWS = 8


def _pack(arrays):
    parts = []
    for a in arrays:
        flat = a.reshape(-1)
        pad = (-flat.shape[0]) % (SMALL_ROWS * LANES)
        parts.append(jnp.pad(flat, (0, pad)).reshape(-1, LANES))
    return jnp.concatenate(parts, axis=0)


def _unpack(buf, like):
    res, row = [], 0
    for a in like:
        size = math.prod(a.shape)
        rows = -(-size // (SMALL_ROWS * LANES)) * SMALL_ROWS
        res.append(buf[row:row + rows].reshape(-1)[:size].reshape(a.shape))
        row += rows
    return res


def kernel(x, ln_g, w_in, w_pool, pool_scale, lam_re, lam_im, log_dt, b_re, b_im, c_re, c_im, d_skip, w_glu, b_glu, branch_g, w_out, final_g, loss_target, m_ln_g, m_w_in, m_w_pool, m_pool_scale, m_lam_re, m_lam_im, m_log_dt, m_b_re, m_b_im, m_c_re, m_c_im, m_d_skip, m_w_glu, m_b_glu, m_branch_g, m_w_out, m_final_g, v_ln_g, v_w_in, v_w_pool, v_pool_scale, v_lam_re, v_lam_im, v_log_dt, v_b_re, v_b_im, v_c_re, v_c_im, v_d_skip, v_w_glu, v_b_glu, v_branch_g, v_w_out, v_final_g):
    given = dict(locals())
    weights = {n: given[n] for n in WEIGHTS}
    depth = ln_g.shape[0]
    _, length, d_model = x.shape
    dm = _Dims(d_model, length)
    x0, target = x[0], loss_target[0]
    c_idx = lax.axis_index("c")
    my_quarter = 2 * lax.axis_index("x") + lax.axis_index("y")

    shard2d = {}
    for l in range(depth):
        for n in SHARDED:
            w = weights[n][l]
            w2 = w.reshape(-1, w.shape[-1])
            shard2d[(n, l)] = w2
    keys = list(shard2d)
    gathered = _allgather_quarters([shard2d[k].astype(BF16).reshape(2, shard2d[k].shape[0] // 2, -1) for k in keys])
    gw = [dict() for _ in range(depth)]
    for (n, l), g in zip(keys, gathered):
        rows, cols = shard2d[(n, l)].shape
        g = g.reshape(4, rows, cols)
        if n == "w_out":
            g = g.reshape(4 * rows, cols)
        if n == "w_pool":
            ngr = w_pool.shape[1]
            g = g.reshape(4, ngr, rows // ngr, cols).transpose(1, 0, 2, 3).reshape(ngr, 4 * rows // ngr, cols)
        gw[l][n] = g
    small_names = [n for n in WEIGHTS if n not in SHARDED and n != "final_g"]
    ps = [{n: (weights[n][l].reshape(1, -1) if weights[n][l].ndim == 1 else weights[n][l]) for n in small_names} for l in range(depth)]

    acts, saved = x0, []
    for l in range(depth):
        acts, sv = _layer_fwd(dm, acts, ps[l], gw[l])
        saved.append(sv)

    def final(xx, tt, g):
        def loss_fn(xv, gv):
            err = _rms(xv, gv) - tt
            return 0.5 * jnp.sum(jnp.mean(err * err, axis=-1))
        val, (dx, dg) = jax.value_and_grad(loss_fn, argnums=(0, 1))(xx, g)
        return (dx, dx), (val.reshape(1, 1), dg)

    (d_act, d_act_bf), (loss_part, g_final_g) = _rowwise(
        final, "final_norm_loss", [(acts, dm.d, 0), (target, dm.d, 0)], [final_g.reshape(1, -1)], [(dm.d, F32), (dm.d, BF16)], [1, dm.d])
    loss = lax.psum(loss_part[0, 0], AXES)

    big, small = [None] * depth, [None] * depth
    for l in reversed(range(depth)):
        d_act, d_act_bf, big[l], small[l] = _layer_bwd(dm, d_act, d_act_bf, ps[l], gw[l], saved[l], want_bf=l > 0)
    grad_x = d_act[None]

    partial = [big[l][n] for (n, l) in keys]
    from_sibling = _to_sibling([lax.dynamic_index_in_dim(g, 1 - c_idx, 0, keepdims=False) for g in partial], "grads_to_sibling")
    chip32, chip16 = [], []
    for g, r in zip(partial, from_sibling):
        s32, s16 = _elementwise(lambda a, b: (a + b, a + b), "chip_sum", [lax.dynamic_index_in_dim(g, c_idx, 0, keepdims=False), r], [F32, BF16])
        chip32.append(s32); chip16.append(s16)
    from_chips = _to_owner_chips(chip16)
    halves = []
    for s32, r in zip(chip32, from_chips):
        own = lax.dynamic_index_in_dim(s32, my_quarter, 0, keepdims=False)
        (tot,) = _elementwise(lambda a, b0, b1, b2: (a + b0.astype(F32) + b1.astype(F32) + b2.astype(F32),), "owner_sum",
                              [own, r[0], r[1], r[2]], [F32])
        halves.append(tot)
    reduced = _share_halves(halves)
    grads = {n: [None] * depth for n in WEIGHTS}
    for (n, l), g in zip(keys, reduced):
        rows, cols = shard2d[(n, l)].shape
        if n == "w_pool":
            ngr = w_pool.shape[1]
            g = g.reshape(2, ngr, rows // ngr // 2, cols).transpose(1, 0, 2, 3)
        grads[n][l] = g.reshape(weights[n][l].shape)

    small_like = [weights[n][l] for l in range(depth) for n in small_names] + [final_g]
    packed = _pack([small[l][n] for l in range(depth) for n in small_names] + [g_final_g])
    summed = _unpack(_allreduce_small(packed), small_like)
    it = iter(summed)
    for l in range(depth):
        for n in small_names:
            grads[n][l] = next(it)
    grad_final = next(it)

    def adam(w, g, m, v, name):
        return _elementwise(lambda *a: _adamw(*a), name, [w, g, m, v], [F32, F32, F32])

    out_g, out_d, out_m, out_v = {}, {}, {}, {}
    for n in SHARDED:
        g = jnp.stack(grads[n])
        shape = g.shape
        flat = lambda a: a.reshape(-1, shape[-1])
        d, m, v = adam(flat(weights[n]), flat(g), flat(given["m_" + n]), flat(given["v_" + n]), "adamw_" + n)
        out_g[n], out_d[n], out_m[n], out_v[n] = g, d.reshape(shape), m.reshape(shape), v.reshape(shape)
    small_all = [n for n in WEIGHTS if n not in SHARDED]
    g_small = {n: (jnp.stack(grads[n]) if n != "final_g" else grad_final) for n in small_all}
    d, m, v = adam(_pack([weights[n] for n in small_all]), _pack([g_small[n] for n in small_all]),
                   _pack([given["m_" + n] for n in small_all]), _pack([given["v_" + n] for n in small_all]), "adamw_small")
    like = [weights[n] for n in small_all]
    for n, dd, mm, vv in zip(small_all, _unpack(d, like), _unpack(m, like), _unpack(v, like)):
        out_g[n], out_d[n], out_m[n], out_v[n] = g_small[n], dd, mm, vv
    return (loss, grad_x, *[out_g[n] for n in WEIGHTS], *[out_d[n] for n in WEIGHTS],
            *[out_m[n] for n in WEIGHTS], *[out_v[n] for n in WEIGHTS])
```

```python
import functools
import math

import jax
import jax.numpy as jnp
from jax import lax
from jax.experimental import pallas as pl
from jax.experimental.pallas import tpu as pltpu

F32 = jnp.float32
BF16 = jnp.bfloat16
EPS = 1e-6
POOL_WINDOWS = (2, 4, 8, 16)
POOL_HALO = 16
HEAD_DIM = 128
SSM_GROUP = 16
SSM_STATE = 64
GROUPS_PER_CHUNK = 16
LANES = 128
VMEM_LIMIT_BYTES = 56 * 1024 * 1024
ROW_TILE_ELEMS = 2 * 1024 * 1024
ADAM_LR, ADAM_B1, ADAM_B2, ADAM_EPS, ADAM_WD, ADAM_STEP = 0.001, 0.9, 0.999, 1e-08, 0.01, 10
MESH = pl.DeviceIdType.MESH
AXES = ("x", "y", "c")
WEIGHTS = ("ln_g", "w_in", "w_pool", "pool_scale", "lam_re", "lam_im", "log_dt", "b_re", "b_im",
           "c_re", "c_im", "d_skip", "w_glu", "b_glu", "branch_g", "w_out", "final_g")
SHARDED = ("w_in", "w_pool", "w_glu", "w_out")


def _params(*sem):
    return pltpu.CompilerParams(dimension_semantics=sem or None, vmem_limit_bytes=VMEM_LIMIT_BYTES)


def _dot(a, b, dims=((1,), (0,))):
    return lax.dot_general(a, b, (dims, ((), ())), preferred_element_type=F32)


NN, NT, TN = ((1,), (0,)), ((1,), (1,)), ((0,), (0,))


def _split(x):
    hi = x.astype(BF16)
    return hi, (x - hi.astype(F32)).astype(BF16)


def _dot3(a, b, dims):
    ah, al = _split(a)
    bh, bl = _split(b)
    return _dot(ah, bh, dims) + _dot(ah, bl, dims) + _dot(al, bh, dims)


def _sigmoid(x):
    return 1.0 / (1.0 + jnp.exp(-x))


def _gelu(x):
    return 0.5 * x * (1.0 + jnp.tanh(0.7978845608028654 * (x + 0.044715 * x * x * x)))


def _rms(x, g):
    return x * lax.rsqrt(jnp.mean(x * x, axis=-1, keepdims=True) + EPS) * g


def _fit(tile, dim):
    tile = min(tile, dim)
    step = LANES if tile >= LANES else 8
    tile -= tile % step
    while dim % tile:
        tile -= step
    return tile


def _mm(a, b, dims, out_dtype, name, *, n=None, b_quarters=None, out_quarters=None, add=None, tm=512, tn=512, tk=4096):
    if dims == TN:
        k_dim, m_dim = a.shape
    else:
        m_dim, k_dim = a.shape
    if n is None:
        n = b.shape[0] if dims == NT else b.shape[1]
    m_unit = {None: m_dim, "cols": m_dim // 2, "rows": m_dim // 8}[out_quarters]
    n_unit = n // 4 if (b_quarters == "n" or out_quarters == "cols") else n
    k_unit = k_dim // 4 if b_quarters == "k" else k_dim
    tm, tn, tk = _fit(tm, m_unit), _fit(tn, n_unit), _fit(tk, k_unit)
    gm, gn, gk = m_dim // tm, n // tn, k_dim // tk
    mb, nb, kb = m_unit // tm, n_unit // tn, k_unit // tk
    if dims == TN:
        a_spec = pl.BlockSpec((tk, tm), lambda i, j, k: (k, i))
    else:
        a_spec = pl.BlockSpec((tm, tk), lambda i, j, k: (i, k))
    if b_quarters == "n":
        bspec = pl.BlockSpec((None, tk, tn), lambda i, j, k: (j // nb, k, j % nb))
    elif b_quarters == "k":
        bspec = pl.BlockSpec((None, tn, tk), lambda i, j, k: (k // kb, j, k % kb))
    elif dims == NT:
        bspec = pl.BlockSpec((tn, tk), lambda i, j, k: (j, k))
    else:
        bspec = pl.BlockSpec((tk, tn), lambda i, j, k: (k, j))
    if out_quarters == "cols":
        out_shape = (2, 4, m_unit, n_unit)
        out_spec = pl.BlockSpec((None, None, tm, tn), lambda i, j, k: (i // mb, j // nb, i % mb, j % nb))
    elif out_quarters == "rows":
        out_shape = (2, 4, m_unit, n)
        out_spec = pl.BlockSpec((None, None, tm, tn), lambda i, j, k: ((i // mb) % 2, i // (2 * mb), i % mb, j))
    else:
        out_shape, out_spec = (m_dim, n), pl.BlockSpec((tm, tn), lambda i, j, k: (i, j))
    in_specs, operands = [a_spec, bspec], [a, b]
    if add is not None:
        in_specs.append(pl.BlockSpec((tm, tn), lambda i, j, k: (i, j)))
        operands.append(add)

    def body(*refs):
        a_ref, b_ref = refs[0], refs[1]
        add_ref = refs[2] if add is not None else None
        o_ref = refs[3] if add is not None else refs[2]

        def finish(r):
            if add_ref is not None:
                r = r + add_ref[...]
            o_ref[...] = r.astype(o_ref.dtype)

        if gk == 1:
            finish(_dot(a_ref[...], b_ref[...], dims))
        else:
            acc = refs[-1]
            k = pl.program_id(2)

            @pl.when(k == 0)
            def _():
                acc[...] = jnp.zeros_like(acc)

            acc[...] += _dot(a_ref[...], b_ref[...], dims)

            @pl.when(k == gk - 1)
            def _():
                finish(acc[...])

    return pl.pallas_call(
        body, name=name, grid=(gm, gn, gk), in_specs=in_specs, out_specs=out_spec,
        out_shape=jax.ShapeDtypeStruct(out_shape, out_dtype),
        scratch_shapes=[pltpu.VMEM((tm, tn), F32)] if gk > 1 else [],
        compiler_params=_params("parallel", "parallel", "arbitrary"),
    )(*operands)


def _mm3(a, b, dims, name, nc=None, tm=512):
    rows = a.shape[0]
    tm = min(tm, rows)
    gm = rows // tm
    if dims == TN:
        nc_ka, nc_nb = a.shape[1], b.shape[1]
        ka, nb = nc_ka // nc, nc_nb // nc

        def body(a_ref, b_ref, o_ref):
            @pl.when(pl.program_id(1) == 0)
            def _():
                o_ref[...] = jnp.zeros_like(o_ref)
            o_ref[...] += _dot3(a_ref[...], b_ref[...], TN)

        return pl.pallas_call(
            body, name=name, grid=(nc, gm),
            in_specs=[pl.BlockSpec((tm, ka), lambda c, i: (i, c)), pl.BlockSpec((tm, nb), lambda c, i: (i, c))],
            out_specs=pl.BlockSpec((None, ka, nb), lambda c, i: (c, 0, 0)),
            out_shape=jax.ShapeDtypeStruct((nc, ka, nb), F32),
            compiler_params=_params("parallel", "arbitrary"),
        )(a, b)
    nc, ka, nb = b.shape
    wa, wo = (ka, nb) if dims == NN else (nb, ka)

    def body(a_ref, b_ref, o_ref):
        o_ref[...] = _dot3(a_ref[...], b_ref[...], dims)

    return pl.pallas_call(
        body, name=name, grid=(nc, gm),
        in_specs=[pl.BlockSpec((tm, wa), lambda c, i: (i, c)), pl.BlockSpec((None, ka, nb), lambda c, i: (c, 0, 0))],
        out_specs=pl.BlockSpec((tm, wo), lambda c, i: (i, c)),
        out_shape=jax.ShapeDtypeStruct((rows, nc * wo), F32),
        compiler_params=_params("parallel", "parallel"),
    )(a, b)


def _rowwise(fn, name, rows, vecs=(), outs=(), sums=()):
    length = rows[0][0].shape[0]
    total = sum(w for _, w, _ in rows) + sum(w for w, _ in outs)
    tile = 8
    while tile * 2 <= min(length, 512) and tile * 2 * total <= ROW_TILE_ELEMS:
        tile *= 2
    assert length % tile == 0
    n_r, n_v, n_o = len(rows), len(vecs), len(outs)

    def body(*refs):
        vals = [r[...] for r in refs[:n_r + n_v]]
        o_refs = refs[n_r + n_v:n_r + n_v + n_o]
        s_refs = refs[n_r + n_v + n_o:]
        res_o, res_s = fn(*vals)
        for ref, val in zip(o_refs, res_o):
            ref[...] = val.astype(ref.dtype)
        if s_refs:
            @pl.when(pl.program_id(0) == 0)
            def _():
                for ref in s_refs:
                    ref[...] = jnp.zeros_like(ref)
            for ref, val in zip(s_refs, res_s):
                ref[...] += val

    def row_spec(w, cb):
        return pl.BlockSpec((tile, w), lambda i: (i, cb))

    res = pl.pallas_call(
        body, name=name, grid=(length // tile,),
        in_specs=[row_spec(w, cb) for _, w, cb in rows] + [pl.BlockSpec(v.shape, lambda i: (0, 0)) for v in vecs],
        out_specs=[row_spec(w, 0) for w, _ in outs] + [pl.BlockSpec((1, w), lambda i: (0, 0)) for w in sums],
        out_shape=[jax.ShapeDtypeStruct((length, w), dt) for w, dt in outs]
        + [jax.ShapeDtypeStruct((1, w), F32) for w in sums],
        compiler_params=_params("arbitrary" if sums else "parallel"),
    )(*[a for a, _, _ in rows], *vecs)
    return res[:n_o], res[n_o:]


def _elementwise(fn, name, arrays, out_dtypes):
    shape = arrays[0].shape
    cols = shape[-1]
    flat = [a.reshape(-1, cols) for a in arrays]
    rows = flat[0].shape[0]
    tile = 8
    while tile * 2 <= rows and rows % (tile * 2) == 0 and tile * 2 * cols * (len(arrays) + len(out_dtypes)) <= ROW_TILE_ELEMS:
        tile *= 2
    assert rows % tile == 0
    n_in = len(flat)

    def body(*refs):
        res = fn(*[r[...] for r in refs[:n_in]])
        for ref, val in zip(refs[n_in:], res):
            ref[...] = val.astype(ref.dtype)

    spec = pl.BlockSpec((tile, cols), lambda i: (i, 0))
    res = pl.pallas_call(
        body, name=name, grid=(rows // tile,), in_specs=[spec] * n_in, out_specs=[spec] * len(out_dtypes),
        out_shape=[jax.ShapeDtypeStruct((rows, cols), dt) for dt in out_dtypes],
        compiler_params=_params("parallel"),
    )(*flat)
    return [r.reshape(shape) for r in res]


def _whole(fn, name, arrays, out_shapes):
    n_in = len(arrays)

    def body(*refs):
        res = fn(*[r[...] for r in refs[:n_in]])
        for ref, val in zip(refs[n_in:], res):
            ref[...] = val

    return pl.pallas_call(
        body, name=name, out_shape=[jax.ShapeDtypeStruct(s, F32) for s in out_shapes],
        compiler_params=_params(),
    )(*arrays)


HBM_SPEC = pl.BlockSpec(memory_space=pltpu.HBM)


def _place():
    x, y, c = lax.axis_index("x"), lax.axis_index("y"), lax.axis_index("c")
    chips = [(1 - x, y), (x, 1 - y), (1 - x, 1 - y)]
    return x, y, c, chips


def _remote(src, dst, send_sem, recv_sem, target):
    return pltpu.make_async_remote_copy(src_ref=src, dst_ref=dst, send_sem=send_sem, recv_sem=recv_sem,
                                        device_id=target, device_id_type=MESH)


def _allgather_quarters(shards):
    n = len(shards)
    own = 6

    def body(*refs):
        ins, outs = refs[:n], refs[n:2 * n]
        send_sems, recv_sems = refs[2 * n:]
        x, y, c, chips = _place()
        mine, sibling = 2 * x + y, (x, y, 1 - c)
        first, passed = [], []
        for t in range(n):
            for k, (cx, cy) in enumerate(chips):
                cp = _remote(ins[t].at[c], outs[t].at[mine, c], send_sems.at[t, k], recv_sems.at[t, k], (cx, cy, c))
                cp.start()
                first.append(cp)
        for t in range(n):
            cp = _remote(ins[t], outs[t].at[mine], send_sems.at[t, own], recv_sems.at[t, own], sibling)
            cp.start()
            first.append(cp)
        for t in range(n):
            for k, (cx, cy) in enumerate(chips):
                landed = outs[t].at[2 * cx + cy, c]
                _remote(landed, landed, send_sems.at[t, k], recv_sems.at[t, k], (cx, cy, c)).wait_recv()
                cp = _remote(landed, landed, send_sems.at[t, 3 + k], recv_sems.at[t, 3 + k], sibling)
                cp.start()
                passed.append(cp)
        for t in range(n):
            got = outs[t].at[mine]
            _remote(got, got, send_sems.at[t, own], recv_sems.at[t, own], sibling).wait_recv()
            for k, (cx, cy) in enumerate(chips):
                got = outs[t].at[2 * cx + cy, 1 - c]
                _remote(got, got, send_sems.at[t, 3 + k], recv_sems.at[t, 3 + k], sibling).wait_recv()
        for cp in first + passed:
            cp.wait_send()

    return pl.pallas_call(
        body, name="allgather_weights",
        in_specs=[HBM_SPEC] * n, out_specs=[HBM_SPEC] * n,
        out_shape=[jax.ShapeDtypeStruct((4,) + s.shape, s.dtype) for s in shards],
        scratch_shapes=[pltpu.SemaphoreType.DMA((n, 7)), pltpu.SemaphoreType.DMA((n, 7))],
        compiler_params=pltpu.CompilerParams(has_side_effects=True),
    )(*shards)


def _to_sibling(arrays, name, other_half=False):
    n = len(arrays)

    def body(*refs):
        ins, outs = refs[:n], refs[n:2 * n]
        send_sems, recv_sems = refs[2 * n:]
        x, y, c, _ = _place()
        copies = [_remote(ins[t].at[1 - c] if other_half else ins[t], outs[t], send_sems.at[t], recv_sems.at[t], (x, y, 1 - c))
                  for t in range(n)]
        for cp in copies:
            cp.start()
        for cp in copies:
            cp.wait()

    return pl.pallas_call(
        body, name=name, in_specs=[HBM_SPEC] * n, out_specs=[HBM_SPEC] * n,
        out_shape=[jax.ShapeDtypeStruct(a.shape[1:] if other_half else a.shape, a.dtype) for a in arrays],
        scratch_shapes=[pltpu.SemaphoreType.DMA((n,)), pltpu.SemaphoreType.DMA((n,))],
        compiler_params=pltpu.CompilerParams(has_side_effects=True),
    )(*arrays)


def _to_owner_chips(arrays):
    n = len(arrays)

    def body(*refs):
        ins, outs = refs[:n], refs[n:2 * n]
        send_sems, recv_sems = refs[2 * n:]
        x, y, c, chips = _place()
        copies = []
        for t in range(n):
            for k, (cx, cy) in enumerate(chips):
                cp = _remote(ins[t].at[2 * cx + cy], outs[t].at[k], send_sems.at[t, k], recv_sems.at[t, k], (cx, cy, c))
                cp.start()
                copies.append(cp)
        for cp in copies:
            cp.wait()

    return pl.pallas_call(
        body, name="grads_to_owner_chips", in_specs=[HBM_SPEC] * n, out_specs=[HBM_SPEC] * n,
        out_shape=[jax.ShapeDtypeStruct((3,) + a.shape[1:], a.dtype) for a in arrays],
        scratch_shapes=[pltpu.SemaphoreType.DMA((n, 3)), pltpu.SemaphoreType.DMA((n, 3))],
        compiler_params=pltpu.CompilerParams(has_side_effects=True),
    )(*arrays)


def _grad_tile(rows, cols, arrays_per_step):
    tile = 8
    while tile * 2 <= rows and rows % (tile * 2) == 0 and tile * 2 * cols * arrays_per_step <= ROW_TILE_ELEMS:
        tile *= 2
    return tile


def _chip_sum(partial, from_sibling, place):
    _, _, rows, cols = partial.shape
    tile = _grad_tile(rows, cols, 4)

    def body(place_ref, a_ref, b_ref, o16_ref, o32_ref):
        total = a_ref[...] + b_ref[...]
        o16_ref[...] = total.astype(BF16)

        @pl.when(pl.program_id(1) == place_ref[1])
        def _():
            o32_ref[...] = total

    return pl.pallas_call(
        body, name="chip_sum",
        grid_spec=pltpu.PrefetchScalarGridSpec(
            num_scalar_prefetch=1, grid=(rows // tile, 4),
            in_specs=[pl.BlockSpec((None, None, tile, cols), lambda i, q, p: (p[0], q, i, 0)),
                      pl.BlockSpec((None, tile, cols), lambda i, q, p: (q, i, 0))],
            out_specs=[pl.BlockSpec((None, tile, cols), lambda i, q, p: (q, i, 0)),
                       pl.BlockSpec((tile, cols), lambda i, q, p: (i, 0))]),
        out_shape=[jax.ShapeDtypeStruct((4, rows, cols), BF16), jax.ShapeDtypeStruct((rows, cols), F32)],
        compiler_params=_params("parallel", "arbitrary"),
    )(place, partial, from_sibling)


def _owner_sum(own, from_chips):
    rows, cols = own.shape
    tile = _grad_tile(rows, cols, 4)

    def body(a_ref, r0_ref, r1_ref, r2_ref, o_ref):
        o_ref[...] = a_ref[...] + r0_ref[...].astype(F32) + r1_ref[...].astype(F32) + r2_ref[...].astype(F32)

    spec = pl.BlockSpec((tile, cols), lambda i: (i, 0))
    return pl.pallas_call(
        body, name="owner_sum", grid=(rows // tile,),
        in_specs=[spec] + [pl.BlockSpec((None, tile, cols), functools.partial(lambda i, k: (k, i, 0), k=k)) for k in range(3)],
        out_specs=spec, out_shape=jax.ShapeDtypeStruct((rows, cols), F32), compiler_params=_params("parallel"),
    )(own, from_chips, from_chips, from_chips)


def _adamw_halves(w, m, v, mine, theirs, place, name):
    depth, _, rows, cols = w.shape
    tile = _grad_tile(rows, cols, 9)

    def body(place_ref, w_ref, m_ref, v_ref, *refs):
        g_refs, outs = refs[:2 * depth], refs[2 * depth:]
        layer, half = pl.program_id(0), pl.program_id(1)
        g = None
        for d in range(depth):
            gd = jnp.where(half == place_ref[0], g_refs[2 * d][...], g_refs[2 * d + 1][...])
            g = gd if g is None else jnp.where(layer == d, gd, g)
        delta, m_new, v_new = _adamw(w_ref[...], g, m_ref[...], v_ref[...])
        for ref, val in zip(outs, (g, delta, m_new, v_new)):
            ref[...] = val

    full = pl.BlockSpec((None, None, tile, cols), lambda l, h, i, p: (l, h, i, 0))
    g_specs = []
    for d in range(depth):
        g_specs += [pl.BlockSpec((tile, cols), functools.partial(lambda l, h, i, p, d: (jnp.where(l == d, i, 0), 0), d=d))] * 2
    operands = [x for pair in zip(mine, theirs) for x in pair]
    return pl.pallas_call(
        body, name=name,
        grid_spec=pltpu.PrefetchScalarGridSpec(
            num_scalar_prefetch=1, grid=(depth, 2, rows // tile),
            in_specs=[full] * 3 + g_specs, out_specs=[full] * 4),
        out_shape=[jax.ShapeDtypeStruct(w.shape, F32)] * 4,
        compiler_params=_params("arbitrary", "arbitrary", "arbitrary"),
    )(place, w, m, v, *operands)


def _allreduce_small(flat):
    rows = flat.shape[0]

    def body(in_ref, out_ref, all_ref, send_sems, recv_sems):
        x, y, c, _ = _place()
        me = 4 * x + 2 * y + c
        all_ref[me] = in_ref[...]
        copies = []
        for r in range(1, 8):
            peer = (x ^ (r >> 2), y ^ ((r >> 1) & 1), c ^ (r & 1))
            cp = _remote(in_ref, all_ref.at[me], send_sems.at[r - 1], recv_sems.at[r - 1], peer)
            cp.start()
            copies.append(cp)
        for r in range(1, 8):
            slot = all_ref.at[me ^ r]
            _remote(slot, slot, send_sems.at[r - 1], recv_sems.at[r - 1], (x, y, c)).wait_recv()
        for cp in copies:
            cp.wait_send()
        acc = all_ref[0]
        for d in range(1, 8):
            acc = acc + all_ref[d]
        out_ref[...] = acc

    return pl.pallas_call(
        body, name="allreduce_small_grads",
        in_specs=[pl.BlockSpec(memory_space=pltpu.VMEM)], out_specs=pl.BlockSpec(memory_space=pltpu.VMEM),
        out_shape=jax.ShapeDtypeStruct(flat.shape, F32),
        scratch_shapes=[pltpu.VMEM((8, rows, LANES), F32), pltpu.SemaphoreType.DMA((7,)), pltpu.SemaphoreType.DMA((7,))],
        compiler_params=pltpu.CompilerParams(has_side_effects=True, vmem_limit_bytes=VMEM_LIMIT_BYTES),
    )(flat)


def _pool_tile(length):
    return min(256, length)


def _pool_fwd(proj, w_pool):
    length = proj.shape[0]
    ngroups, ch, _ = w_pool.shape
    width, tile = ngroups * ch, _pool_tile(length)

    def body(cur_ref, prev_ref, w_ref, pooled_ref, mixed_ref):
        i = pl.program_id(0)
        cur = cur_ref[...]
        tail = jnp.where(i > 0, prev_ref[tile - POOL_HALO:tile, :], 0.0)
        padded = jnp.concatenate([tail, cur], axis=0)
        pos = (lax.broadcasted_iota(jnp.int32, (tile, 1), 0) + i * tile + 1).astype(F32)
        for g, window in enumerate(POOL_WINDOWS):
            cols = slice(g * ch, (g + 1) * ch)
            run, shift = padded[:, cols], 1
            while shift < window:
                run = run + pltpu.roll(run, shift, 0)
                shift *= 2
            pooled = (run[POOL_HALO:, :] / jnp.minimum(pos, float(window)) - cur[:, cols]).astype(BF16)
            pooled_ref[:, cols] = pooled
            mixed_ref[:, cols] = _dot(pooled, w_ref[g])

    return pl.pallas_call(
        body, name="pool_fwd", grid=(length // tile,),
        in_specs=[pl.BlockSpec((tile, width), lambda i: (i, 0)),
                  pl.BlockSpec((tile, width), lambda i: (jnp.maximum(i - 1, 0), 0)),
                  pl.BlockSpec(w_pool.shape, lambda i: (0, 0, 0))],
        out_specs=[pl.BlockSpec((tile, width), lambda i: (i, 0))] * 2,
        out_shape=[jax.ShapeDtypeStruct((length, width), BF16), jax.ShapeDtypeStruct((length, width), F32)],
        compiler_params=_params("parallel"),
    )(proj, proj, w_pool)


def _pool_bwd_mix(d_mixed, pooled, w_pool):
    length, width = d_mixed.shape
    ngroups, ch, _ = w_pool.shape
    tile = _pool_tile(length)

    def body(dm_ref, pooled_ref, w_ref, dp_ref, dw_ref):
        @pl.when(pl.program_id(0) == 0)
        def _():
            dw_ref[...] = jnp.zeros_like(dw_ref)
        for g in range(ngroups):
            cols = slice(g * ch, (g + 1) * ch)
            dm = dm_ref[:, cols].astype(BF16)
            dp_ref[:, cols] = _dot(dm, w_ref[g], NT)
            dw_ref[g] += _dot(pooled_ref[:, cols], dm, TN)

    return pl.pallas_call(
        body, name="pool_bwd_mix", grid=(length // tile,),
        in_specs=[pl.BlockSpec((tile, width), lambda i: (i, 0)), pl.BlockSpec((tile, width), lambda i: (i, 0)),
                  pl.BlockSpec(w_pool.shape, lambda i: (0, 0, 0))],
        out_specs=[pl.BlockSpec((tile, width), lambda i: (i, 0)), pl.BlockSpec(w_pool.shape, lambda i: (0, 0, 0))],
        out_shape=[jax.ShapeDtypeStruct((length, width), F32), jax.ShapeDtypeStruct(w_pool.shape, F32)],
        compiler_params=_params("arbitrary"),
    )(d_mixed, pooled, w_pool)


def _pool_bwd_window(d_pooled, ngroups):
    length, width = d_pooled.shape
    ch, tile = width // ngroups, _pool_tile(length)
    last = length // tile - 1

    def body(cur_ref, next_ref, dx_ref):
        i = pl.program_id(0)
        cur = cur_ref[...]
        head = jnp.where(i < last, next_ref[0:POOL_HALO, :], 0.0)
        padded = jnp.concatenate([cur, head], axis=0)
        rows = tile + POOL_HALO
        pos = (lax.broadcasted_iota(jnp.int32, (rows, 1), 0) + i * tile + 1).astype(F32)
        for g, window in enumerate(POOL_WINDOWS):
            cols = slice(g * ch, (g + 1) * ch)
            run, shift = padded[:, cols] / jnp.minimum(pos, float(window)), 1
            while shift < window:
                run = run + pltpu.roll(run, rows - shift, 0)
                shift *= 2
            dx_ref[:, cols] = (run[0:tile, :] - cur[:, cols]).astype(BF16)

    return pl.pallas_call(
        body, name="pool_bwd_window", grid=(length // tile,),
        in_specs=[pl.BlockSpec((tile, width), lambda i: (i, 0)),
                  pl.BlockSpec((tile, width), lambda i: (jnp.minimum(i + 1, last), 0))],
        out_specs=pl.BlockSpec((tile, width), lambda i: (i, 0)),
        out_shape=jax.ShapeDtypeStruct((length, width), BF16),
        compiler_params=_params("parallel"),
    )(d_pooled, d_pooled)


ATTN_TILE = 256
LOG_WEIGHT_FLOOR = -110.0


def _walk_back(n_chunks, chunk, carry):
    def cond(state):
        return jnp.logical_and(state[0] < n_chunks, jnp.max(state[1]) > LOG_WEIGHT_FLOOR)

    def step(state):
        return (state[0] + 1,) + tuple(chunk(n_chunks - 1 - state[0], tuple(state[1:])))

    return lax.while_loop(cond, step, (jnp.int32(0),) + tuple(carry))[1:]


def _stick_weights(q, kc, upper, run_log, mask):
    z = _dot(q, kc, NT)
    e = jnp.exp(-jnp.abs(z))
    softplus = jnp.maximum(z, 0.0) + jnp.log(1.0 + e)
    log_sig = z - softplus
    log_1m = -softplus if mask is None else jnp.where(mask, -softplus, 0.0)
    hi, lo = _split(log_1m)
    suffix = _dot(hi, upper) + _dot(lo, upper) + run_log
    w = jnp.exp(log_sig + suffix)
    if mask is not None:
        w = jnp.where(mask, w, 0.0)
    return w, log_sig, suffix[:, 0:1] + log_1m[:, 0:1]


def _attn_consts(tile):
    jj = lax.broadcasted_iota(jnp.int32, (tile, tile), 0)
    ss = lax.broadcasted_iota(jnp.int32, (tile, tile), 1)
    return (jj > ss).astype(BF16), (jj >= ss).astype(BF16), ss < jj


def _attn_fwd(proj, n_heads, q_blk, k_blk, v_blk):
    length = proj.shape[0]
    tile = min(ATTN_TILE, length)
    scale = HEAD_DIM ** -0.5

    def body(q_ref, k_ref, v_ref, o_ref):
        i = pl.program_id(1)
        q = (q_ref[...] * scale).astype(BF16)
        upper, _, diag_mask = _attn_consts(tile)

        def chunk(j, carry, mask):
            run_log, acc = carry
            start = pl.multiple_of(j * tile, tile)
            kc = k_ref[pl.ds(start, tile), :].astype(BF16)
            vc = v_ref[pl.ds(start, tile), :].astype(BF16)
            w, _, run_log = _stick_weights(q, kc, upper, run_log, mask)
            return run_log, acc + _dot(w.astype(BF16), vc)

        carry = (jnp.zeros((tile, 1), F32), jnp.zeros((tile, HEAD_DIM), F32))
        carry = chunk(i, carry, diag_mask)
        carry = _walk_back(i, lambda j, cr: chunk(j, cr, None), carry)
        o_ref[...] = carry[1]

    return pl.pallas_call(
        body, name="attn_fwd", grid=(n_heads, length // tile),
        in_specs=[pl.BlockSpec((tile, HEAD_DIM), lambda h, i: (i, q_blk + h)),
                  pl.BlockSpec((length, HEAD_DIM), lambda h, i: (0, k_blk + h)),
                  pl.BlockSpec((length, HEAD_DIM), lambda h, i: (0, v_blk + h))],
        out_specs=pl.BlockSpec((tile, HEAD_DIM), lambda h, i: (i, h)),
        out_shape=jax.ShapeDtypeStruct((length, n_heads * HEAD_DIM), F32),
        compiler_params=_params("parallel", "parallel"),
    )(proj, proj, proj)


def _attn_bwd(proj, out, d_out, n_heads, q_blk, k_blk, v_blk):
    length = proj.shape[0]
    tile = min(ATTN_TILE, length)
    scale = HEAD_DIM ** -0.5

    def body(q_ref, k_ref, v_ref, o_ref, do_ref, dq_ref, dk_ref, dv_ref):
        i = pl.program_id(1)

        @pl.when(i == 0)
        def _():
            dk_ref[...] = jnp.zeros_like(dk_ref)
            dv_ref[...] = jnp.zeros_like(dv_ref)

        q = (q_ref[...] * scale).astype(BF16)
        do = do_ref[...].astype(BF16)
        total = jnp.sum(do.astype(F32) * o_ref[...], axis=1, keepdims=True)
        upper, upper_incl, diag_mask = _attn_consts(tile)

        def chunk(j, carry, mask):
            run_log, run_g, dq = carry
            start = pl.multiple_of(j * tile, tile)
            kc = k_ref[pl.ds(start, tile), :].astype(BF16)
            vc = v_ref[pl.ds(start, tile), :].astype(BF16)
            w, log_sig, run_log = _stick_weights(q, kc, upper, run_log, mask)
            wb = w.astype(BF16)
            g = wb.astype(F32) * _dot(do, vc, NT)
            g_hi, g_lo = _split(g)
            g_suffix = _dot(g_hi, upper_incl) + _dot(g_lo, upper_incl) + run_g
            dz = g - jnp.exp(log_sig) * (g + (total - g_suffix))
            if mask is not None:
                dz = jnp.where(mask, dz, 0.0)
            dzb = dz.astype(BF16)
            dk_ref[pl.ds(start, tile), :] += _dot(dzb, q, TN)
            dv_ref[pl.ds(start, tile), :] += _dot(wb, do, TN)
            return run_log, g_suffix[:, 0:1], dq + _dot(dzb, kc)

        carry = (jnp.zeros((tile, 1), F32), jnp.zeros((tile, 1), F32), jnp.zeros((tile, HEAD_DIM), F32))
        carry = chunk(i, carry, diag_mask)
        carry = _walk_back(i, lambda j, cr: chunk(j, cr, None), carry)
        dq_ref[...] = (carry[2] * scale).astype(BF16)

    width = n_heads * HEAD_DIM
    tile_spec = pl.BlockSpec((tile, HEAD_DIM), lambda h, i: (i, h))
    head_spec = pl.BlockSpec((length, HEAD_DIM), lambda h, i: (0, h))
    return pl.pallas_call(
        body, name="attn_bwd", grid=(n_heads, length // tile),
        in_specs=[pl.BlockSpec((tile, HEAD_DIM), lambda h, i: (i, q_blk + h)),
                  pl.BlockSpec((length, HEAD_DIM), lambda h, i: (0, k_blk + h)),
                  pl.BlockSpec((length, HEAD_DIM), lambda h, i: (0, v_blk + h)),
                  tile_spec, tile_spec],
        out_specs=[tile_spec, head_spec, head_spec],
        out_shape=[jax.ShapeDtypeStruct((length, width), BF16), jax.ShapeDtypeStruct((length, width), F32),
                   jax.ShapeDtypeStruct((length, width), F32)],
        compiler_params=_params("parallel", "arbitrary"),
    )(proj, proj, proj, out, d_out)


SCAN_CHUNK = 128


def _disc_lam(lam_re, lam_im, log_dt):
    dt = jnp.exp(log_dt)
    mag, phase = jnp.exp(lam_re * dt), lam_im * dt
    bar_re, bar_im = mag * jnp.cos(phase), mag * jnp.sin(phase)
    num_re, den = bar_re - 1.0, lam_re * lam_re + lam_im * lam_im
    return (bar_re, bar_im, (num_re * lam_re + bar_im * lam_im) / den, (bar_im * lam_re - num_re * lam_im) / den)


def _disc_b(cf_re, cf_im, b_re, b_im):
    return cf_re * b_re - cf_im * b_im, cf_re * b_im + cf_im * b_re


def _halves(re, im):
    return jnp.concatenate([re, im], axis=1)


def _scan_fwd(bu, bar_re, bar_im):
    length, groups, width = bu.shape
    chunk = min(SCAN_CHUNK, length)

    def body(bu_ref, re_ref, im_ref, st_ref, carry):
        @pl.when(pl.program_id(0) == 0)
        def _():
            carry[...] = jnp.zeros_like(carry)
        a_same = _halves(re_ref[...], re_ref[...])
        a_swap = _halves(-im_ref[...], im_ref[...])

        def step(t, x):
            x = a_same * x + a_swap * pltpu.roll(x, width // 2, 1) + bu_ref[t]
            st_ref[t] = x
            return x

        carry[...] = lax.fori_loop(0, chunk, step, carry[...], unroll=8)

    blk = pl.BlockSpec((chunk, groups, width), lambda i: (i, 0, 0))
    par = pl.BlockSpec(bar_re.shape, lambda i: (0, 0))
    return pl.pallas_call(
        body, name="s5_scan_fwd", grid=(length // chunk,), in_specs=[blk, par, par], out_specs=blk,
        out_shape=jax.ShapeDtypeStruct(bu.shape, F32), scratch_shapes=[pltpu.VMEM((groups, width), F32)],
        compiler_params=_params("arbitrary"),
    )(bu, bar_re, bar_im)


def _scan_bwd(d_states, states, bar_re, bar_im):
    length, groups, width = states.shape
    chunk = min(SCAN_CHUNK, length)
    last = length // chunk - 1

    def body(g_ref, st_ref, re_ref, im_ref, out_ref, same_ref, swap_ref, carry):
        @pl.when(pl.program_id(0) == 0)
        def _():
            carry[...] = jnp.zeros_like(carry)
            same_ref[...] = jnp.zeros_like(same_ref)
            swap_ref[...] = jnp.zeros_like(swap_ref)
        a_same = _halves(re_ref[...], re_ref[...])
        a_swap = _halves(im_ref[...], -im_ref[...])

        def step(k, cr):
            adj, acc_same, acc_swap = cr
            t = chunk - 1 - k
            s = st_ref[t]
            acc_same = acc_same + adj * s
            acc_swap = acc_swap + adj * pltpu.roll(s, width // 2, 1)
            adj = g_ref[t] + a_same * adj + a_swap * pltpu.roll(adj, width // 2, 1)
            out_ref[t] = adj
            return adj, acc_same, acc_swap

        adj, acc_same, acc_swap = lax.fori_loop(0, chunk, step, (carry[...], same_ref[...], swap_ref[...]), unroll=4)
        carry[...] = adj
        same_ref[...] = acc_same
        swap_ref[...] = acc_swap

    blk = pl.BlockSpec((chunk, groups, width), lambda i: (last - i, 0, 0))
    par = pl.BlockSpec(bar_re.shape, lambda i: (0, 0))
    acc = pl.BlockSpec((groups, width), lambda i: (0, 0))
    return pl.pallas_call(
        body, name="s5_scan_bwd", grid=(length // chunk,), in_specs=[blk, blk, par, par], out_specs=[blk, acc, acc],
        out_shape=[jax.ShapeDtypeStruct(states.shape, F32), jax.ShapeDtypeStruct((groups, width), F32),
                   jax.ShapeDtypeStruct((groups, width), F32)],
        scratch_shapes=[pltpu.VMEM((groups, width), F32)],
        compiler_params=_params("arbitrary"),
    )(d_states, states, bar_re, bar_im)


def _block_diag(per_group):
    groups, a, b = per_group.shape
    nc = groups // GROUPS_PER_CHUNK
    eye = jnp.eye(GROUPS_PER_CHUNK, dtype=per_group.dtype)
    x = per_group.reshape(nc, GROUPS_PER_CHUNK, a, 1, b) * eye[None, :, None, :, None]
    return x.reshape(nc, GROUPS_PER_CHUNK * a, GROUPS_PER_CHUNK * b)


def _block_diag_part(chunks, a, b):
    nc = chunks.shape[0]
    x = chunks.reshape(nc, GROUPS_PER_CHUNK, a, GROUPS_PER_CHUNK, b)
    x = jnp.stack([x[:, g, :, g, :] for g in range(GROUPS_PER_CHUNK)], axis=1)
    return x.reshape(nc * GROUPS_PER_CHUNK, a, b)


def _epilogue(raw, gate, scale, g):
    return _rms(raw * scale, g) * (gate * _sigmoid(gate))


def _ssm_mid(y, u, d_skip):
    return _gelu(y + d_skip * u)


def _adamw(w, g, m, v):
    m = ADAM_B1 * m + (1.0 - ADAM_B1) * g
    v = ADAM_B2 * v + (1.0 - ADAM_B2) * (g * g)
    m_hat = m / (1.0 - ADAM_B1 ** ADAM_STEP)
    v_hat = v / (1.0 - ADAM_B2 ** ADAM_STEP)
    return -ADAM_LR * (m_hat / (jnp.sqrt(v_hat) + ADAM_EPS) + ADAM_WD * w), m, v


class _Dims:
    def __init__(self, d_model, length):
        self.d, self.length = d_model, length
        self.d_pool, self.d_attn = d_model // 4, d_model // 2
        self.d_ssm = d_model - self.d_pool - self.d_attn
        self.heads = self.d_attn // HEAD_DIM
        self.groups = self.d_ssm // SSM_GROUP
        self.d_in = 2 * self.d_pool + 4 * self.d_attn + 2 * self.d_ssm
        sizes = (self.d_pool, self.d_pool, self.d_attn, self.d_attn, self.d_attn, self.d_attn, self.d_ssm, self.d_ssm)
        offs = [0]
        for s in sizes[:-1]:
            offs.append(offs[-1] + s)
        (self.o_px, self.o_pgate, self.o_q, self.o_k, self.o_v, self.o_agate, self.o_u, self.o_sgate) = offs


def _ssm_operands(dm, p):
    groups, states = dm.groups, SSM_STATE
    bar_re, bar_im, cf_re, cf_im = _whole(_disc_lam, "s5_disc_lam", [p["lam_re"], p["lam_im"], p["log_dt"].reshape(groups, 1)],
                                          [(groups, states)] * 4)
    b_re2, b_im2 = p["b_re"].reshape(groups * states, SSM_GROUP), p["b_im"].reshape(groups * states, SSM_GROUP)
    bb_re, bb_im = _whole(_disc_b, "s5_disc_b", [cf_re.reshape(-1, 1), cf_im.reshape(-1, 1), b_re2, b_im2],
                          [(groups * states, SSM_GROUP)] * 2)
    bcat = jnp.concatenate([bb_re.reshape(groups, states, SSM_GROUP), bb_im.reshape(groups, states, SSM_GROUP)], axis=1)
    b_blk = _block_diag(jnp.swapaxes(bcat, 1, 2))
    ccat = jnp.concatenate([p["c_re"], -p["c_im"]], axis=2)
    c_blk = _block_diag(jnp.swapaxes(ccat, 1, 2))
    return dict(bar_re=bar_re, bar_im=bar_im, cf_re=cf_re, cf_im=cf_im, b_re2=b_re2, b_im2=b_im2, b_blk=b_blk, c_blk=c_blk)


def _layer_fwd(dm, x_in, p, gw):
    length = dm.length
    blk = lambda off, w: off // w
    (h,), _ = _rowwise(lambda x, g: ((_rms(x, g),), ()), "rms_fwd", [(x_in, dm.d, 0)], [p["ln_g"]], [(dm.d, BF16)])
    proj = _mm(h, gw["w_in"], NN, F32, "in_proj", n=dm.d_in, b_quarters="n")
    pooled, mixed = _pool_fwd(proj, gw["w_pool"])
    qb, kb, vb = dm.o_q // HEAD_DIM, dm.o_k // HEAD_DIM, dm.o_v // HEAD_DIM
    attn = _attn_fwd(proj, dm.heads, qb, kb, vb)
    so = _ssm_operands(dm, p)
    u_row = (proj, dm.d_ssm, blk(dm.o_u, dm.d_ssm))
    (u,), _ = _rowwise(lambda v: ((v,), ()), "take_u", [u_row], [], [(dm.d_ssm, F32)])
    bu = _mm3(u, so["b_blk"], NN, "s5_bu")
    states3 = _scan_fwd(bu.reshape(length, dm.groups, 2 * SSM_STATE), so["bar_re"], so["bar_im"])
    states = states3.reshape(length, dm.groups * 2 * SSM_STATE)
    y = _mm3(states, so["c_blk"], NN, "s5_y")
    (hg,), _ = _rowwise(lambda yy, uu, dsk: ((_ssm_mid(yy, uu, dsk),), ()), "s5_mid_fwd",
                        [(y, dm.d_ssm, 0), (u, dm.d_ssm, 0)], [p["d_skip"]], [(dm.d_ssm, BF16)])
    z = _mm(hg, gw["w_glu"], NN, F32, "glu_proj", n=2 * dm.d_ssm, b_quarters="n")

    def glu(zz, bias):
        zz = zz + bias
        return (zz[:, :dm.d_ssm] * _sigmoid(zz[:, dm.d_ssm:]),), ()

    (ssm,), _ = _rowwise(glu, "glu_fwd", [(z, 2 * dm.d_ssm, 0)], [p["b_glu"]], [(dm.d_ssm, F32)])
    g_pool, g_attn, g_ssm = (p["branch_g"][:, :dm.d_pool], p["branch_g"][:, dm.d_pool:dm.d_pool + dm.d_attn],
                             p["branch_g"][:, dm.d_pool + dm.d_attn:])
    ones_attn, ones_ssm = jnp.ones((1, dm.d_attn), F32), jnp.ones((1, dm.d_ssm), F32)
    epi = lambda raw, gate, scale, g: ((_epilogue(raw, gate, scale, g),), ())
    branches = [("pool", mixed, dm.d_pool, dm.o_pgate, p["pool_scale"], g_pool),
                ("attn", attn, dm.d_attn, dm.o_agate, ones_attn, g_attn),
                ("ssm", ssm, dm.d_ssm, dm.o_sgate, ones_ssm, g_ssm)]
    ys = []
    for nm, raw, w, off, scale, g in branches:
        (yb,), _ = _rowwise(epi, "epilogue_fwd_" + nm, [(raw, w, 0), (proj, w, blk(off, w))], [scale, g], [(w, BF16)])
        ys.append(yb)
    y_cat = jnp.concatenate(ys, axis=1)
    x_out = _mm(y_cat, gw["w_out"], NN, F32, "out_proj", add=x_in)
    saved = dict(x_in=x_in, h=h, proj=proj, pooled=pooled, mixed=mixed, attn=attn, so=so, u=u, states3=states3, states=states,
                 y=y, hg=hg, z=z, ssm=ssm, y_cat=y_cat, scales=(p["pool_scale"], ones_attn, ones_ssm), gs=(g_pool, g_attn, g_ssm))
    return x_out, saved


def _layer_bwd(dm, d_out, d_out_bf, p, gw, sv, want_bf):
    length = dm.length
    blk = lambda off, w: off // w
    proj = sv["proj"]
    d_y = _mm(d_out_bf, gw["w_out"], NT, F32, "out_proj_dgrad")
    g_w_out = _mm(sv["y_cat"], d_out_bf, TN, F32, "out_proj_wgrad", out_quarters="rows")

    def epi_bwd(nseg):
        def fn(*vals):
            dys, (raw, gate, scale, g) = vals[:nseg], vals[nseg:]
            dyb = dys[0] if nseg == 1 else jnp.concatenate(dys, axis=1)
            _, vjp = jax.vjp(_epilogue, raw, gate, scale, g)
            d_raw, d_gate, d_scale, d_g = vjp(dyb)
            return (d_raw, d_gate), (d_scale, d_g)
        return fn

    branch = [("pool", sv["mixed"], dm.d_pool, dm.o_pgate, 0), ("attn", sv["attn"], dm.d_attn, dm.o_agate, dm.d_pool),
              ("ssm", sv["ssm"], dm.d_ssm, dm.o_sgate, dm.d_pool + dm.d_attn)]
    d_raws, d_gates, d_scales, d_gs = [], [], [], []
    for (nm, raw, w, off, yoff), scale, g in zip(branch, sv["scales"], sv["gs"]):
        seg = math.gcd(w, yoff) if yoff else w
        dy_rows = [(d_y, seg, yoff // seg + s) for s in range(w // seg)]
        (d_raw, d_gate), (d_scale, d_g) = _rowwise(
            epi_bwd(len(dy_rows)), "epilogue_bwd_" + nm, dy_rows + [(raw, w, 0), (proj, w, blk(off, w))], [scale, g],
            [(w, F32), (w, BF16)], [w, w])
        d_raws.append(d_raw); d_gates.append(d_gate); d_scales.append(d_scale); d_gs.append(d_g)
    d_pooled, g_w_pool = _pool_bwd_mix(d_raws[0], sv["pooled"], gw["w_pool"])
    d_px = _pool_bwd_window(d_pooled, len(POOL_WINDOWS))
    qb, kb, vb = dm.o_q // HEAD_DIM, dm.o_k // HEAD_DIM, dm.o_v // HEAD_DIM
    d_q, d_k, d_v = _attn_bwd(proj, sv["attn"], d_raws[1], dm.heads, qb, kb, vb)
    so = sv["so"]

    def glu_bwd(d_ssm, zz, bias):
        zz = zz + bias
        val, sg = zz[:, :dm.d_ssm], _sigmoid(zz[:, dm.d_ssm:])
        dz = jnp.concatenate([d_ssm * sg, d_ssm * val * sg * (1.0 - sg)], axis=1)
        return (dz,), (jnp.sum(dz, axis=0, keepdims=True),)

    (d_z,), (g_b_glu,) = _rowwise(glu_bwd, "glu_bwd", [(d_raws[2], dm.d_ssm, 0), (sv["z"], 2 * dm.d_ssm, 0)], [p["b_glu"]],
                                  [(2 * dm.d_ssm, BF16)], [2 * dm.d_ssm])
    d_hg = _mm(d_z, gw["w_glu"], NT, F32, "glu_dgrad", n=dm.d_ssm, b_quarters="k")
    g_w_glu = _mm(sv["hg"], d_z, TN, F32, "glu_wgrad", out_quarters="cols")

    def mid_bwd(dh, yy, uu, dsk):
        _, vjp = jax.vjp(_ssm_mid, yy, uu, dsk)
        dy_, du_, ddsk = vjp(dh)
        return (dy_, du_), (ddsk,)

    (d_yssm, d_u_direct), (g_d_skip,) = _rowwise(mid_bwd, "s5_mid_bwd", [(d_hg, dm.d_ssm, 0), (sv["y"], dm.d_ssm, 0), (sv["u"], dm.d_ssm, 0)],
                                                 [p["d_skip"]], [(dm.d_ssm, F32), (dm.d_ssm, F32)], [dm.d_ssm])
    d_states = _mm3(d_yssm, so["c_blk"], NT, "s5_y_dgrad")
    d_c_blk = _mm3(sv["states"], d_yssm, TN, "s5_y_wgrad", nc=so["c_blk"].shape[0])
    d_bu3, acc_same, acc_swap = _scan_bwd(d_states.reshape(length, dm.groups, 2 * SSM_STATE), sv["states3"], so["bar_re"], so["bar_im"])
    d_bu = d_bu3.reshape(length, dm.groups * 2 * SSM_STATE)
    d_u_scan = _mm3(d_bu, so["b_blk"], NT, "s5_bu_dgrad")
    d_b_blk = _mm3(sv["u"], d_bu, TN, "s5_bu_wgrad", nc=so["b_blk"].shape[0])
    (d_u,) = _elementwise(lambda a, b: (a + b,), "s5_du", [d_u_direct, d_u_scan], [BF16])
    groups, states = dm.groups, SSM_STATE
    d_ccat = jnp.swapaxes(_block_diag_part(d_c_blk, 2 * states, SSM_GROUP), 1, 2)
    d_bcat = jnp.swapaxes(_block_diag_part(d_b_blk, SSM_GROUP, 2 * states), 1, 2)
    d_bb_re, d_bb_im = d_bcat[:, :states].reshape(-1, SSM_GROUP), d_bcat[:, states:].reshape(-1, SSM_GROUP)

    def disc_b_bwd(cf_re, cf_im, b_re, b_im, g_re, g_im):
        _, vjp = jax.vjp(_disc_b, cf_re, cf_im, b_re, b_im)
        return vjp((g_re, g_im))

    d_cf_re, d_cf_im, g_b_re, g_b_im = _whole(
        disc_b_bwd, "s5_disc_b_bwd", [so["cf_re"].reshape(-1, 1), so["cf_im"].reshape(-1, 1), so["b_re2"], so["b_im2"], d_bb_re, d_bb_im],
        [(groups * states, 1)] * 2 + [(groups * states, SSM_GROUP)] * 2)

    def disc_lam_bwd(lam_re, lam_im, log_dt, same, swap, g_cf_re, g_cf_im):
        g_bar_re = same[:, :states] + same[:, states:]
        g_bar_im = swap[:, states:] - swap[:, :states]
        _, vjp = jax.vjp(_disc_lam, lam_re, lam_im, log_dt)
        return vjp((g_bar_re, g_bar_im, g_cf_re, g_cf_im))

    g_lam_re, g_lam_im, g_log_dt = _whole(
        disc_lam_bwd, "s5_disc_lam_bwd", [p["lam_re"], p["lam_im"], p["log_dt"].reshape(groups, 1), acc_same, acc_swap,
                                          d_cf_re.reshape(groups, states), d_cf_im.reshape(groups, states)],
        [(groups, states)] * 2 + [(groups, 1)])
    (g_c_im,) = _elementwise(lambda a: (-a,), "s5_neg_c_im", [d_ccat[:, :, states:].reshape(groups * SSM_GROUP, states)], [F32])
    d_proj = jnp.concatenate([d_px, d_gates[0], d_q, d_k.astype(BF16), d_v.astype(BF16), d_gates[1], d_u, d_gates[2]], axis=1)
    d_h = _mm(d_proj, gw["w_in"], NT, F32, "in_proj_dgrad", n=dm.d, b_quarters="k")
    g_w_in = _mm(sv["h"], d_proj, TN, F32, "in_proj_wgrad", out_quarters="cols")

    def rms_bwd(dh, xx, dres, g):
        _, vjp = jax.vjp(_rms, xx, g)
        dx, dg = vjp(dh)
        dx = dx + dres
        return ((dx, dx) if want_bf else (dx,)), (dg,)

    d_xs, (g_ln_g,) = _rowwise(rms_bwd, "rms_bwd", [(d_h, dm.d, 0), (sv["x_in"], dm.d, 0), (d_out, dm.d, 0)], [p["ln_g"]],
                               [(dm.d, F32), (dm.d, BF16)] if want_bf else [(dm.d, F32)], [dm.d])
    ngr, ch = gw["w_pool"].shape[0], gw["w_pool"].shape[1]
    q_rows = ch // 4
    g_w_pool2 = g_w_pool.reshape(ngr, 4, 2, q_rows // 2, ch).transpose(2, 1, 0, 3, 4).reshape(2, 4, ngr * q_rows // 2, ch)
    big = dict(w_in=g_w_in, w_out=g_w_out, w_glu=g_w_glu, w_pool=g_w_pool2)
    small = dict(ln_g=g_ln_g, pool_scale=d_scales[0], lam_re=g_lam_re, lam_im=g_lam_im, log_dt=g_log_dt.reshape(1, groups),
                 b_re=g_b_re, b_im=g_b_im, c_re=d_ccat[:, :, :states], c_im=g_c_im, d_skip=g_d_skip, b_glu=g_b_glu,
                 branch_g=jnp.concatenate(d_gs, axis=1))
    return d_xs[0], (d_xs[1] if want_bf else None), big, small


SMALL_ROWS = 8


def _pack(arrays):
    parts = []
    for a in arrays:
        flat = a.reshape(-1)
        pad = (-flat.shape[0]) % (SMALL_ROWS * LANES)
        parts.append(jnp.pad(flat, (0, pad)).reshape(-1, LANES))
    return jnp.concatenate(parts, axis=0)


def _unpack(buf, like):
    res, row = [], 0
    for a in like:
        size = math.prod(a.shape)
        rows = -(-size // (SMALL_ROWS * LANES)) * SMALL_ROWS
        res.append(buf[row:row + rows].reshape(-1)[:size].reshape(a.shape))
        row += rows
    return res


def kernel(x, ln_g, w_in, w_pool, pool_scale, lam_re, lam_im, log_dt, b_re, b_im, c_re, c_im, d_skip, w_glu, b_glu, branch_g, w_out, final_g, loss_target, m_ln_g, m_w_in, m_w_pool, m_pool_scale, m_lam_re, m_lam_im, m_log_dt, m_b_re, m_b_im, m_c_re, m_c_im, m_d_skip, m_w_glu, m_b_glu, m_branch_g, m_w_out, m_final_g, v_ln_g, v_w_in, v_w_pool, v_pool_scale, v_lam_re, v_lam_im, v_log_dt, v_b_re, v_b_im, v_c_re, v_c_im, v_d_skip, v_w_glu, v_b_glu, v_branch_g, v_w_out, v_final_g):
    given = dict(locals())
    weights = {n: given[n] for n in WEIGHTS}
    depth = ln_g.shape[0]
    _, length, d_model = x.shape
    dm = _Dims(d_model, length)
    x0, target = x[0], loss_target[0]
    c_idx = lax.axis_index("c")
    my_quarter = 2 * lax.axis_index("x") + lax.axis_index("y")

    shard2d = {}
    for l in range(depth):
        for n in SHARDED:
            w = weights[n][l]
            w2 = w.reshape(-1, w.shape[-1])
            shard2d[(n, l)] = w2
    keys = list(shard2d)
    gathered = _allgather_quarters([shard2d[k].astype(BF16).reshape(2, shard2d[k].shape[0] // 2, -1) for k in keys])
    gw = [dict() for _ in range(depth)]
    for (n, l), g in zip(keys, gathered):
        rows, cols = shard2d[(n, l)].shape
        g = g.reshape(4, rows, cols)
        if n == "w_out":
            g = g.reshape(4 * rows, cols)
        if n == "w_pool":
            ngr = w_pool.shape[1]
            g = g.reshape(4, ngr, rows // ngr, cols).transpose(1, 0, 2, 3).reshape(ngr, 4 * rows // ngr, cols)
        gw[l][n] = g
    small_names = [n for n in WEIGHTS if n not in SHARDED and n != "final_g"]
    ps = [{n: (weights[n][l].reshape(1, -1) if weights[n][l].ndim == 1 else weights[n][l]) for n in small_names} for l in range(depth)]

    acts, saved = x0, []
    for l in range(depth):
        acts, sv = _layer_fwd(dm, acts, ps[l], gw[l])
        saved.append(sv)

    def final(xx, tt, g):
        def loss_fn(xv, gv):
            err = _rms(xv, gv) - tt
            return 0.5 * jnp.sum(jnp.mean(err * err, axis=-1))
        val, (dx, dg) = jax.value_and_grad(loss_fn, argnums=(0, 1))(xx, g)
        return (dx, dx), (val.reshape(1, 1), dg)

    (d_act, d_act_bf), (loss_part, g_final_g) = _rowwise(
        final, "final_norm_loss", [(acts, dm.d, 0), (target, dm.d, 0)], [final_g.reshape(1, -1)], [(dm.d, F32), (dm.d, BF16)], [1, dm.d])
    loss = lax.psum(loss_part[0, 0], AXES)

    big, small = [None] * depth, [None] * depth
    for l in reversed(range(depth)):
        d_act, d_act_bf, big[l], small[l] = _layer_bwd(dm, d_act, d_act_bf, ps[l], gw[l], saved[l], want_bf=l > 0)
    grad_x = d_act[None]

    place = jnp.stack([c_idx, my_quarter]).astype(jnp.int32)
    partial = [big[l][n] for (n, l) in keys]
    from_sibling = _to_sibling(partial, "grads_to_sibling", other_half=True)
    chip16, chip32 = zip(*[_chip_sum(g, r, place) for g, r in zip(partial, from_sibling)])
    from_chips = _to_owner_chips(list(chip16))
    mine = [_owner_sum(own, r) for own, r in zip(chip32, from_chips)]
    theirs = _to_sibling(mine, "reduced_half_to_sibling")
    mine, theirs = dict(zip(keys, mine)), dict(zip(keys, theirs))

    small_all = [n for n in WEIGHTS if n not in SHARDED]
    packed = _pack([small[l][n] for l in range(depth) for n in small_names] + [g_final_g])
    summed = _unpack(_allreduce_small(packed), [weights[n][l] for l in range(depth) for n in small_names] + [final_g])
    g_small = {n: jnp.stack([summed[l * len(small_names) + i] for l in range(depth)]) for i, n in enumerate(small_names)}
    g_small["final_g"] = summed[-1]

    out_g, out_d, out_m, out_v = {}, {}, {}, {}
    for n in SHARDED:
        shape = weights[n].shape
        if n == "w_pool":
            ngr = shape[1]
            both = jnp.stack([jnp.where(c_idx == 0, jnp.stack([mine[(n, l)], theirs[(n, l)]]), jnp.stack([theirs[(n, l)], mine[(n, l)]]))
                              for l in range(depth)])
            g = both.reshape(depth, 2, ngr, -1, shape[-1]).transpose(0, 2, 1, 3, 4).reshape(shape)
            res = [g] + _elementwise(lambda *a: _adamw(*a), "adamw_" + n, [weights[n], g, given["m_" + n], given["v_" + n]], [F32] * 3)
        else:
            halves = lambda a: a.reshape(depth, 2, -1, shape[-1])
            res = _adamw_halves(halves(weights[n]), halves(given["m_" + n]), halves(given["v_" + n]),
                                [mine[(n, l)] for l in range(depth)], [theirs[(n, l)] for l in range(depth)], place, "adamw_" + n)
        out_g[n], out_d[n], out_m[n], out_v[n] = [r.reshape(shape) for r in res]
    d, m, v = _whole(_adamw, "adamw_small", [_pack([weights[n] for n in small_all]), _pack([g_small[n] for n in small_all]),
                                            _pack([given["m_" + n] for n in small_all]), _pack([given["v_" + n] for n in small_all])],
                     [_pack([weights[n] for n in small_all]).shape] * 3)
    like = [weights[n] for n in small_all]
    for n, dd, mm, vv in zip(small_all, _unpack(d, like), _unpack(m, like), _unpack(v, like)):
        out_g[n], out_d[n], out_m[n], out_v[n] = g_small[n], dd, mm, vv
    return (loss, grad_x, *[out_g[n] for n in WEIGHTS], *[out_d[n] for n in WEIGHTS],
            *[out_m[n] for n in WEIGHTS], *[out_v[n] for n in WEIGHTS])
```

```python
import functools
import math

import jax
import jax.numpy as jnp
from jax import lax
from jax.experimental import pallas as pl
from jax.experimental.pallas import tpu as pltpu

F32 = jnp.float32
BF16 = jnp.bfloat16
EPS = 1e-6
POOL_WINDOWS = (2, 4, 8, 16)
POOL_HALO = 16
HEAD_DIM = 128
SSM_GROUP = 16
SSM_STATE = 64
GROUPS_PER_CHUNK = 16
LANES = 128
VMEM_LIMIT_BYTES = 56 * 1024 * 1024
ROW_TILE_ELEMS = 2 * 1024 * 1024
ADAM_LR, ADAM_B1, ADAM_B2, ADAM_EPS, ADAM_WD, ADAM_STEP = 0.001, 0.9, 0.999, 1e-08, 0.01, 10
MESH = pl.DeviceIdType.MESH
AXES = ("x", "y", "c")
WEIGHTS = ("ln_g", "w_in", "w_pool", "pool_scale", "lam_re", "lam_im", "log_dt", "b_re", "b_im",
           "c_re", "c_im", "d_skip", "w_glu", "b_glu", "branch_g", "w_out", "final_g")
SHARDED = ("w_in", "w_pool", "w_glu", "w_out")


def _params(*sem):
    return pltpu.CompilerParams(dimension_semantics=sem or None, vmem_limit_bytes=VMEM_LIMIT_BYTES)


def _dot(a, b, dims=((1,), (0,))):
    return lax.dot_general(a, b, (dims, ((), ())), preferred_element_type=F32)


NN, NT, TN = ((1,), (0,)), ((1,), (1,)), ((0,), (0,))


def _split(x):
    hi = x.astype(BF16)
    return hi, (x - hi.astype(F32)).astype(BF16)


def _dot3(a, b, dims):
    ah, al = _split(a)
    bh, bl = _split(b)
    return _dot(ah, bh, dims) + _dot(ah, bl, dims) + _dot(al, bh, dims)


def _sigmoid(x):
    return 1.0 / (1.0 + jnp.exp(-x))


def _gelu(x):
    return 0.5 * x * (1.0 + jnp.tanh(0.7978845608028654 * (x + 0.044715 * x * x * x)))


def _rms(x, g):
    return x * lax.rsqrt(jnp.mean(x * x, axis=-1, keepdims=True) + EPS) * g


HBM_SPEC = pl.BlockSpec(memory_space=pltpu.HBM)


def _place():
    x, y, c = lax.axis_index("x"), lax.axis_index("y"), lax.axis_index("c")
    chips = [(1 - x, y), (x, 1 - y), (1 - x, 1 - y)]
    return x, y, c, chips


def _remote(src, dst, send_sem, recv_sem, target):
    return pltpu.make_async_remote_copy(src_ref=src, dst_ref=dst, send_sem=send_sem, recv_sem=recv_sem,
                                        device_id=target, device_id_type=MESH)


class _Exchange:
    def __init__(self, name, inputs, out_shapes, sems, start, finish):
        self.name, self.inputs, self.out_shapes, self.sems = name, list(inputs), list(out_shapes), sems
        self.start, self.finish, self.results = start, finish, None

    def run(self):
        n_in, n_out = len(self.inputs), len(self.out_shapes)

        def body(*refs):
            parts = (refs[:n_in], refs[n_in:n_in + n_out], refs[-2], refs[-1])
            self.start(*parts)
            self.finish(*parts)

        self.results = pl.pallas_call(
            body, name=self.name, in_specs=[HBM_SPEC] * n_in, out_specs=[HBM_SPEC] * n_out, out_shape=self.out_shapes,
            scratch_shapes=[pltpu.SemaphoreType.DMA(self.sems), pltpu.SemaphoreType.DMA(self.sems)],
            compiler_params=pltpu.CompilerParams(has_side_effects=True),
        )(*self.inputs)
        return self.results


def _allgather_quarters(shards, name):
    n, own = len(shards), 6

    def start(ins, outs, send_sems, recv_sems):
        x, y, c, chips = _place()
        mine = 2 * x + y
        for t in range(n):
            for k, (cx, cy) in enumerate(chips):
                _remote(ins[t].at[c], outs[t].at[mine, c], send_sems.at[t, k], recv_sems.at[t, k], (cx, cy, c)).start()
            _remote(ins[t], outs[t].at[mine], send_sems.at[t, own], recv_sems.at[t, own], (x, y, 1 - c)).start()

    def finish(ins, outs, send_sems, recv_sems):
        x, y, c, chips = _place()
        mine, sibling = 2 * x + y, (x, y, 1 - c)
        for t in range(n):
            for k, (cx, cy) in enumerate(chips):
                landed = outs[t].at[2 * cx + cy, c]
                _remote(landed, landed, send_sems.at[t, k], recv_sems.at[t, k], sibling).wait_recv()
                _remote(landed, landed, send_sems.at[t, 3 + k], recv_sems.at[t, 3 + k], sibling).start()
        for t in range(n):
            got = outs[t].at[mine]
            _remote(got, got, send_sems.at[t, own], recv_sems.at[t, own], sibling).wait_recv()
            _remote(ins[t], got, send_sems.at[t, own], recv_sems.at[t, own], sibling).wait_send()
            for k, (cx, cy) in enumerate(chips):
                got = outs[t].at[2 * cx + cy, 1 - c]
                _remote(got, got, send_sems.at[t, 3 + k], recv_sems.at[t, 3 + k], sibling).wait_recv()
                sent = outs[t].at[2 * cx + cy, c]
                _remote(sent, sent, send_sems.at[t, 3 + k], recv_sems.at[t, 3 + k], sibling).wait_send()
                _remote(ins[t].at[c], sent, send_sems.at[t, k], recv_sems.at[t, k], sibling).wait_send()

    return _Exchange(name, shards, [jax.ShapeDtypeStruct((4,) + s.shape, s.dtype) for s in shards], (n, 7), start, finish)


def _to_sibling(arrays, name, other_half=False):
    n = len(arrays)

    def copies(ins, outs, send_sems, recv_sems):
        x, y, c, _ = _place()
        return [_remote(ins[t].at[1 - c] if other_half else ins[t], outs[t], send_sems.at[t], recv_sems.at[t], (x, y, 1 - c))
                for t in range(n)]

    def start(*refs):
        for cp in copies(*refs):
            cp.start()

    def finish(*refs):
        for cp in copies(*refs):
            cp.wait()

    shapes = [jax.ShapeDtypeStruct(a.shape[1:] if other_half else a.shape, a.dtype) for a in arrays]
    return _Exchange(name, arrays, shapes, (n,), start, finish)


def _to_owner_chips(arrays, name):
    n = len(arrays)

    def copies(ins, outs, send_sems, recv_sems):
        x, y, c, chips = _place()
        return [_remote(ins[t].at[2 * cx + cy], outs[t].at[k], send_sems.at[t, k], recv_sems.at[t, k], (cx, cy, c))
                for t in range(n) for k, (cx, cy) in enumerate(chips)]

    def start(*refs):
        for cp in copies(*refs):
            cp.start()

    def finish(*refs):
        for cp in copies(*refs):
            cp.wait()

    shapes = [jax.ShapeDtypeStruct((3,) + a.shape[1:], a.dtype) for a in arrays]
    return _Exchange(name, arrays, shapes, (n, 3), start, finish)


def _fit(tile, dim):
    tile = min(tile, dim)
    step = LANES if tile >= LANES else 8
    tile -= tile % step
    while dim % tile:
        tile -= step
    return tile


def _mm(a, b, dims, out_dtype, name, *, n=None, b_quarters=None, out_quarters=None, add=None, comm=None, tm=512, tn=512, tk=4096):
    if dims == TN:
        k_dim, m_dim = a.shape
    else:
        m_dim, k_dim = a.shape
    if n is None:
        n = b.shape[0] if dims == NT else b.shape[1]
    m_unit = {None: m_dim, "cols": m_dim // 2, "rows": m_dim // 8}[out_quarters]
    n_unit = n // 4 if (b_quarters == "n" or out_quarters == "cols") else n
    k_unit = k_dim // 4 if b_quarters == "k" else k_dim
    tm, tn, tk = _fit(tm, m_unit), _fit(tn, n_unit), _fit(tk, k_unit)
    gm, gn, gk = m_dim // tm, n // tn, k_dim // tk
    mb, nb, kb = m_unit // tm, n_unit // tn, k_unit // tk
    if dims == TN:
        a_spec = pl.BlockSpec((tk, tm), lambda i, j, k: (k, i))
    else:
        a_spec = pl.BlockSpec((tm, tk), lambda i, j, k: (i, k))
    if b_quarters == "n":
        bspec = pl.BlockSpec((None, tk, tn), lambda i, j, k: (j // nb, k, j % nb))
    elif b_quarters == "k":
        bspec = pl.BlockSpec((None, tn, tk), lambda i, j, k: (k // kb, j, k % kb))
    elif dims == NT:
        bspec = pl.BlockSpec((tn, tk), lambda i, j, k: (j, k))
    else:
        bspec = pl.BlockSpec((tk, tn), lambda i, j, k: (k, j))
    if out_quarters == "cols":
        out_shape = (2, 4, m_unit, n_unit)
        out_spec = pl.BlockSpec((None, None, tm, tn), lambda i, j, k: (i // mb, j // nb, i % mb, j % nb))
    elif out_quarters == "rows":
        out_shape = (2, 4, m_unit, n)
        out_spec = pl.BlockSpec((None, None, tm, tn), lambda i, j, k: ((i // mb) % 2, i // (2 * mb), i % mb, j))
    else:
        out_shape, out_spec = (m_dim, n), pl.BlockSpec((tm, tn), lambda i, j, k: (i, j))
    in_specs, operands = [a_spec, bspec], [a, b]
    if add is not None:
        in_specs.append(pl.BlockSpec((tm, tn), lambda i, j, k: (i, j)))
        operands.append(add)

    n_in = len(operands)
    n_cin, n_cout = (len(comm.inputs), len(comm.out_shapes)) if comm is not None else (0, 0)

    def body(*refs):
        a_ref, b_ref = refs[0], refs[1]
        add_ref = refs[2] if add is not None else None
        o_ref = refs[n_in + n_cin]
        if comm is not None:
            c_refs = (refs[n_in:n_in + n_cin], refs[n_in + n_cin + 1:n_in + n_cin + 1 + n_cout], refs[-2], refs[-1])
            step = (pl.program_id(0) * gn + pl.program_id(1)) * gk + pl.program_id(2)

            @pl.when(step == 0)
            def _():
                comm.start(*c_refs)

        def finish(r):
            if add_ref is not None:
                r = r + add_ref[...]
            o_ref[...] = r.astype(o_ref.dtype)

        if gk == 1:
            finish(_dot(a_ref[...], b_ref[...], dims))
        else:
            acc = refs[n_in + n_cin + 1 + n_cout]
            k = pl.program_id(2)

            @pl.when(k == 0)
            def _():
                acc[...] = jnp.zeros_like(acc)

            acc[...] += _dot(a_ref[...], b_ref[...], dims)

            @pl.when(k == gk - 1)
            def _():
                finish(acc[...])

        if comm is not None:
            @pl.when(step == gm * gn * gk - 1)
            def _():
                comm.finish(*c_refs)

    scratch = [pltpu.VMEM((tm, tn), F32)] if gk > 1 else []
    if comm is None:
        return pl.pallas_call(
            body, name=name, grid=(gm, gn, gk), in_specs=in_specs, out_specs=out_spec,
            out_shape=jax.ShapeDtypeStruct(out_shape, out_dtype), scratch_shapes=scratch,
            compiler_params=_params("parallel", "parallel", "arbitrary"),
        )(*operands)
    res = pl.pallas_call(
        body, name=name, grid=(gm, gn, gk), in_specs=in_specs + [HBM_SPEC] * n_cin, out_specs=[out_spec] + [HBM_SPEC] * n_cout,
        out_shape=[jax.ShapeDtypeStruct(out_shape, out_dtype)] + list(comm.out_shapes),
        scratch_shapes=scratch + [pltpu.SemaphoreType.DMA(comm.sems), pltpu.SemaphoreType.DMA(comm.sems)],
        compiler_params=pltpu.CompilerParams(dimension_semantics=("arbitrary",) * 3, vmem_limit_bytes=VMEM_LIMIT_BYTES,
                                             has_side_effects=True),
    )(*operands, *comm.inputs)
    comm.results = list(res[1:])
    return res[0]


def _mm3(a, b, dims, name, nc=None, tm=512, rows3d=()):
    rows = a.shape[0]
    tm = min(tm, rows)
    gm = rows // tm

    def row_spec(width, three_d):
        if three_d:
            return pl.BlockSpec((tm, width // LANES, LANES), lambda c, i: (i, c, 0))
        return pl.BlockSpec((tm, width), lambda c, i: (i, c))

    def load(ref, three_d):
        if not three_d:
            return ref[...]
        return jnp.concatenate([ref[:, r, :] for r in range(ref.shape[1])], axis=1)

    if dims == TN:
        ka = a.shape[1] * (a.shape[2] if "a" in rows3d else 1) // nc
        nb = b.shape[1] * (b.shape[2] if "b" in rows3d else 1) // nc

        def body(a_ref, b_ref, o_ref):
            @pl.when(pl.program_id(1) == 0)
            def _():
                o_ref[...] = jnp.zeros_like(o_ref)
            o_ref[...] += _dot3(load(a_ref, "a" in rows3d), load(b_ref, "b" in rows3d), TN)

        return pl.pallas_call(
            body, name=name, grid=(nc, gm),
            in_specs=[row_spec(ka, "a" in rows3d), row_spec(nb, "b" in rows3d)],
            out_specs=pl.BlockSpec((None, ka, nb), lambda c, i: (c, 0, 0)),
            out_shape=jax.ShapeDtypeStruct((nc, ka, nb), F32),
            compiler_params=_params("parallel", "arbitrary"),
        )(a, b)
    nc, ka, nb = b.shape
    wa, wo = (ka, nb) if dims == NN else (nb, ka)

    def body(a_ref, b_ref, o_ref):
        res = _dot3(load(a_ref, "a" in rows3d), b_ref[...], dims)
        if "o" in rows3d:
            for r in range(wo // LANES):
                o_ref[:, r, :] = res[:, r * LANES:(r + 1) * LANES]
        else:
            o_ref[...] = res

    out_shape = (rows, nc * wo // LANES, LANES) if "o" in rows3d else (rows, nc * wo)
    return pl.pallas_call(
        body, name=name, grid=(nc, gm),
        in_specs=[row_spec(wa, "a" in rows3d), pl.BlockSpec((None, ka, nb), lambda c, i: (c, 0, 0))],
        out_specs=row_spec(wo, "o" in rows3d),
        out_shape=jax.ShapeDtypeStruct(out_shape, F32),
        compiler_params=_params("parallel", "parallel"),
    )(a, b)


def _rowwise(fn, name, rows, vecs=(), outs=(), sums=()):
    length = rows[0][0].shape[0]
    total = sum(w for _, w, _ in rows) + sum(w for w, _ in outs)
    tile = 8
    while tile * 2 <= min(length, 512) and tile * 2 * total <= ROW_TILE_ELEMS:
        tile *= 2
    assert length % tile == 0
    n_r, n_v, n_o = len(rows), len(vecs), len(outs)

    def body(*refs):
        vals = [r[...] for r in refs[:n_r + n_v]]
        o_refs = refs[n_r + n_v:n_r + n_v + n_o]
        s_refs = refs[n_r + n_v + n_o:]
        res_o, res_s = fn(*vals)
        for ref, val in zip(o_refs, res_o):
            ref[...] = val.astype(ref.dtype)
        if s_refs:
            @pl.when(pl.program_id(0) == 0)
            def _():
                for ref in s_refs:
                    ref[...] = jnp.zeros_like(ref)
            for ref, val in zip(s_refs, res_s):
                ref[...] += val

    def row_spec(w, cb):
        return pl.BlockSpec((tile, w), lambda i: (i, cb))

    res = pl.pallas_call(
        body, name=name, grid=(length // tile,),
        in_specs=[row_spec(w, cb) for _, w, cb in rows] + [pl.BlockSpec(v.shape, lambda i: (0, 0)) for v in vecs],
        out_specs=[row_spec(w, 0) for w, _ in outs] + [pl.BlockSpec((1, w), lambda i: (0, 0)) for w in sums],
        out_shape=[jax.ShapeDtypeStruct((length, w), dt) for w, dt in outs]
        + [jax.ShapeDtypeStruct((1, w), F32) for w in sums],
        compiler_params=_params("arbitrary" if sums else "parallel"),
    )(*[a for a, _, _ in rows], *vecs)
    return res[:n_o], res[n_o:]


def _elementwise(fn, name, arrays, out_dtypes):
    shape = arrays[0].shape
    cols = shape[-1]
    flat = [a.reshape(-1, cols) for a in arrays]
    rows = flat[0].shape[0]
    tile = 8
    while tile * 2 <= rows and rows % (tile * 2) == 0 and tile * 2 * cols * (len(arrays) + len(out_dtypes)) <= ROW_TILE_ELEMS:
        tile *= 2
    assert rows % tile == 0
    n_in = len(flat)

    def body(*refs):
        res = fn(*[r[...] for r in refs[:n_in]])
        for ref, val in zip(refs[n_in:], res):
            ref[...] = val.astype(ref.dtype)

    spec = pl.BlockSpec((tile, cols), lambda i: (i, 0))
    res = pl.pallas_call(
        body, name=name, grid=(rows // tile,), in_specs=[spec] * n_in, out_specs=[spec] * len(out_dtypes),
        out_shape=[jax.ShapeDtypeStruct((rows, cols), dt) for dt in out_dtypes],
        compiler_params=_params("parallel"),
    )(*flat)
    return [r.reshape(shape) for r in res]


def _whole(fn, name, arrays, out_shapes):
    n_in = len(arrays)

    def body(*refs):
        res = fn(*[r[...] for r in refs[:n_in]])
        for ref, val in zip(refs[n_in:], res):
            ref[...] = val

    return pl.pallas_call(
        body, name=name, out_shape=[jax.ShapeDtypeStruct(s, F32) for s in out_shapes],
        compiler_params=_params(),
    )(*arrays)


def _grad_tile(rows, cols, arrays_per_step):
    tile = 8
    while tile * 2 <= rows and rows % (tile * 2) == 0 and tile * 2 * cols * arrays_per_step <= ROW_TILE_ELEMS:
        tile *= 2
    return tile


def _chip_sum(partial, from_sibling, place):
    _, _, rows, cols = partial.shape
    tile = _grad_tile(rows, cols, 4)

    def body(place_ref, a_ref, b_ref, o16_ref, o32_ref):
        total = a_ref[...] + b_ref[...]
        o16_ref[...] = total.astype(BF16)

        @pl.when(pl.program_id(1) == place_ref[1])
        def _():
            o32_ref[...] = total

    return pl.pallas_call(
        body, name="chip_sum",
        grid_spec=pltpu.PrefetchScalarGridSpec(
            num_scalar_prefetch=1, grid=(rows // tile, 4),
            in_specs=[pl.BlockSpec((None, None, tile, cols), lambda i, q, p: (p[0], q, i, 0)),
                      pl.BlockSpec((None, tile, cols), lambda i, q, p: (q, i, 0))],
            out_specs=[pl.BlockSpec((None, tile, cols), lambda i, q, p: (q, i, 0)),
                       pl.BlockSpec((tile, cols), lambda i, q, p: (i, 0))]),
        out_shape=[jax.ShapeDtypeStruct((4, rows, cols), BF16), jax.ShapeDtypeStruct((rows, cols), F32)],
        compiler_params=_params("parallel", "arbitrary"),
    )(place, partial, from_sibling)


def _owner_sum(own, from_chips):
    rows, cols = own.shape
    tile = _grad_tile(rows, cols, 4)

    def body(a_ref, r0_ref, r1_ref, r2_ref, o_ref):
        o_ref[...] = a_ref[...] + r0_ref[...].astype(F32) + r1_ref[...].astype(F32) + r2_ref[...].astype(F32)

    spec = pl.BlockSpec((tile, cols), lambda i: (i, 0))
    return pl.pallas_call(
        body, name="owner_sum", grid=(rows // tile,),
        in_specs=[spec] + [pl.BlockSpec((None, tile, cols), functools.partial(lambda i, k: (k, i, 0), k=k)) for k in range(3)],
        out_specs=spec, out_shape=jax.ShapeDtypeStruct((rows, cols), F32), compiler_params=_params("parallel"),
    )(own, from_chips, from_chips, from_chips)


def _adamw_halves(w, m, v, mine, theirs, place, name):
    depth, _, rows, cols = w.shape
    tile = _grad_tile(rows, cols, 9)

    def body(place_ref, w_ref, m_ref, v_ref, *refs):
        g_refs, outs = refs[:2 * depth], refs[2 * depth:]
        layer, half = pl.program_id(0), pl.program_id(1)
        g = None
        for d in range(depth):
            gd = jnp.where(half == place_ref[0], g_refs[2 * d][...], g_refs[2 * d + 1][...])
            g = gd if g is None else jnp.where(layer == d, gd, g)
        delta, m_new, v_new = _adamw(w_ref[...], g, m_ref[...], v_ref[...])
        for ref, val in zip(outs, (g, delta, m_new, v_new)):
            ref[...] = val

    full = pl.BlockSpec((None, None, tile, cols), lambda l, h, i, p: (l, h, i, 0))
    g_specs = []
    for d in range(depth):
        g_specs += [pl.BlockSpec((tile, cols), functools.partial(lambda l, h, i, p, d: (jnp.where(l == d, i, 0), 0), d=d))] * 2
    operands = [x for pair in zip(mine, theirs) for x in pair]
    return pl.pallas_call(
        body, name=name,
        grid_spec=pltpu.PrefetchScalarGridSpec(
            num_scalar_prefetch=1, grid=(depth, 2, rows // tile),
            in_specs=[full] * 3 + g_specs, out_specs=[full] * 4),
        out_shape=[jax.ShapeDtypeStruct(w.shape, F32)] * 4,
        compiler_params=_params("arbitrary", "arbitrary", "arbitrary"),
    )(place, w, m, v, *operands)


def _allreduce_small(flat):
    rows = flat.shape[0]

    def body(in_ref, out_ref, all_ref, send_sems, recv_sems):
        x, y, c, _ = _place()
        me = 4 * x + 2 * y + c
        all_ref[me] = in_ref[...]
        copies = []
        for r in range(1, 8):
            peer = (x ^ (r >> 2), y ^ ((r >> 1) & 1), c ^ (r & 1))
            cp = _remote(in_ref, all_ref.at[me], send_sems.at[r - 1], recv_sems.at[r - 1], peer)
            cp.start()
            copies.append(cp)
        for r in range(1, 8):
            slot = all_ref.at[me ^ r]
            _remote(slot, slot, send_sems.at[r - 1], recv_sems.at[r - 1], (x, y, c)).wait_recv()
        for cp in copies:
            cp.wait_send()
        acc = all_ref[0]
        for d in range(1, 8):
            acc = acc + all_ref[d]
        out_ref[...] = acc

    return pl.pallas_call(
        body, name="allreduce_small_grads",
        in_specs=[pl.BlockSpec(memory_space=pltpu.VMEM)], out_specs=pl.BlockSpec(memory_space=pltpu.VMEM),
        out_shape=jax.ShapeDtypeStruct(flat.shape, F32),
        scratch_shapes=[pltpu.VMEM((8, rows, LANES), F32), pltpu.SemaphoreType.DMA((7,)), pltpu.SemaphoreType.DMA((7,))],
        compiler_params=pltpu.CompilerParams(has_side_effects=True, vmem_limit_bytes=VMEM_LIMIT_BYTES),
    )(flat)


def _pool_tile(length):
    return min(256, length)


def _pool_fwd(proj, w_pool):
    length = proj.shape[0]
    ngroups, ch, _ = w_pool.shape
    width, tile = ngroups * ch, _pool_tile(length)

    def body(cur_ref, prev_ref, w_ref, pooled_ref, mixed_ref):
        i = pl.program_id(0)
        cur = cur_ref[...]
        tail = jnp.where(i > 0, prev_ref[tile - POOL_HALO:tile, :], 0.0)
        padded = jnp.concatenate([tail, cur], axis=0)
        pos = (lax.broadcasted_iota(jnp.int32, (tile, 1), 0) + i * tile + 1).astype(F32)
        for g, window in enumerate(POOL_WINDOWS):
            cols = slice(g * ch, (g + 1) * ch)
            run, shift = padded[:, cols], 1
            while shift < window:
                run = run + pltpu.roll(run, shift, 0)
                shift *= 2
            pooled = (run[POOL_HALO:, :] / jnp.minimum(pos, float(window)) - cur[:, cols]).astype(BF16)
            pooled_ref[:, cols] = pooled
            mixed_ref[:, cols] = _dot(pooled, w_ref[g])

    return pl.pallas_call(
        body, name="pool_fwd", grid=(length // tile,),
        in_specs=[pl.BlockSpec((tile, width), lambda i: (i, 0)),
                  pl.BlockSpec((tile, width), lambda i: (jnp.maximum(i - 1, 0), 0)),
                  pl.BlockSpec(w_pool.shape, lambda i: (0, 0, 0))],
        out_specs=[pl.BlockSpec((tile, width), lambda i: (i, 0))] * 2,
        out_shape=[jax.ShapeDtypeStruct((length, width), BF16), jax.ShapeDtypeStruct((length, width), F32)],
        compiler_params=_params("parallel"),
    )(proj, proj, w_pool)


def _pool_bwd_mix(d_mixed, pooled, w_pool):
    length, width = d_mixed.shape
    ngroups, ch, _ = w_pool.shape
    tile = _pool_tile(length)

    def body(dm_ref, pooled_ref, w_ref, dp_ref, dw_ref):
        @pl.when(pl.program_id(0) == 0)
        def _():
            dw_ref[...] = jnp.zeros_like(dw_ref)
        for g in range(ngroups):
            cols = slice(g * ch, (g + 1) * ch)
            dm = dm_ref[:, cols].astype(BF16)
            dp_ref[:, cols] = _dot(dm, w_ref[g], NT)
            dw_ref[g] += _dot(pooled_ref[:, cols], dm, TN)

    return pl.pallas_call(
        body, name="pool_bwd_mix", grid=(length // tile,),
        in_specs=[pl.BlockSpec((tile, width), lambda i: (i, 0)), pl.BlockSpec((tile, width), lambda i: (i, 0)),
                  pl.BlockSpec(w_pool.shape, lambda i: (0, 0, 0))],
        out_specs=[pl.BlockSpec((tile, width), lambda i: (i, 0)), pl.BlockSpec(w_pool.shape, lambda i: (0, 0, 0))],
        out_shape=[jax.ShapeDtypeStruct((length, width), F32), jax.ShapeDtypeStruct(w_pool.shape, F32)],
        compiler_params=_params("arbitrary"),
    )(d_mixed, pooled, w_pool)


def _pool_bwd_window(d_pooled, ngroups):
    length, width = d_pooled.shape
    ch, tile = width // ngroups, _pool_tile(length)
    last = length // tile - 1

    def body(cur_ref, next_ref, dx_ref):
        i = pl.program_id(0)
        cur = cur_ref[...]
        head = jnp.where(i < last, next_ref[0:POOL_HALO, :], 0.0)
        padded = jnp.concatenate([cur, head], axis=0)
        rows = tile + POOL_HALO
        pos = (lax.broadcasted_iota(jnp.int32, (rows, 1), 0) + i * tile + 1).astype(F32)
        for g, window in enumerate(POOL_WINDOWS):
            cols = slice(g * ch, (g + 1) * ch)
            run, shift = padded[:, cols] / jnp.minimum(pos, float(window)), 1
            while shift < window:
                run = run + pltpu.roll(run, rows - shift, 0)
                shift *= 2
            dx_ref[:, cols] = (run[0:tile, :] - cur[:, cols]).astype(BF16)

    return pl.pallas_call(
        body, name="pool_bwd_window", grid=(length // tile,),
        in_specs=[pl.BlockSpec((tile, width), lambda i: (i, 0)),
                  pl.BlockSpec((tile, width), lambda i: (jnp.minimum(i + 1, last), 0))],
        out_specs=pl.BlockSpec((tile, width), lambda i: (i, 0)),
        out_shape=jax.ShapeDtypeStruct((length, width), BF16),
        compiler_params=_params("parallel"),
    )(d_pooled, d_pooled)


ATTN_TILE = 256
LOG_WEIGHT_FLOOR = -110.0


def _walk_back(n_chunks, chunk, carry):
    def cond(state):
        return jnp.logical_and(state[0] < n_chunks, jnp.max(state[1]) > LOG_WEIGHT_FLOOR)

    def step(state):
        return (state[0] + 1,) + tuple(chunk(n_chunks - 1 - state[0], tuple(state[1:])))

    return lax.while_loop(cond, step, (jnp.int32(0),) + tuple(carry))[1:]


def _stick_weights(q, kc, upper, run_log, mask):
    z = _dot(q, kc, NT)
    e = jnp.exp(-jnp.abs(z))
    softplus = jnp.maximum(z, 0.0) + jnp.log(1.0 + e)
    log_sig = z - softplus
    log_1m = -softplus if mask is None else jnp.where(mask, -softplus, 0.0)
    hi, lo = _split(log_1m)
    suffix = _dot(hi, upper) + _dot(lo, upper) + run_log
    w = jnp.exp(log_sig + suffix)
    if mask is not None:
        w = jnp.where(mask, w, 0.0)
    return w, log_sig, suffix[:, 0:1] + log_1m[:, 0:1]


def _attn_consts(tile):
    jj = lax.broadcasted_iota(jnp.int32, (tile, tile), 0)
    ss = lax.broadcasted_iota(jnp.int32, (tile, tile), 1)
    return (jj > ss).astype(BF16), (jj >= ss).astype(BF16), ss < jj


def _attn_fwd(proj, n_heads, q_blk, k_blk, v_blk):
    length = proj.shape[0]
    tile = min(ATTN_TILE, length)
    scale = HEAD_DIM ** -0.5

    def body(q_ref, k_ref, v_ref, o_ref):
        i = pl.program_id(1)
        q = (q_ref[...] * scale).astype(BF16)
        upper, _, diag_mask = _attn_consts(tile)

        def chunk(j, carry, mask):
            run_log, acc = carry
            start = pl.multiple_of(j * tile, tile)
            kc = k_ref[pl.ds(start, tile), :].astype(BF16)
            vc = v_ref[pl.ds(start, tile), :].astype(BF16)
            w, _, run_log = _stick_weights(q, kc, upper, run_log, mask)
            return run_log, acc + _dot(w.astype(BF16), vc)

        carry = (jnp.zeros((tile, 1), F32), jnp.zeros((tile, HEAD_DIM), F32))
        carry = chunk(i, carry, diag_mask)
        carry = _walk_back(i, lambda j, cr: chunk(j, cr, None), carry)
        o_ref[...] = carry[1]

    return pl.pallas_call(
        body, name="attn_fwd", grid=(n_heads, length // tile),
        in_specs=[pl.BlockSpec((tile, HEAD_DIM), lambda h, i: (i, q_blk + h)),
                  pl.BlockSpec((length, HEAD_DIM), lambda h, i: (0, k_blk + h)),
                  pl.BlockSpec((length, HEAD_DIM), lambda h, i: (0, v_blk + h))],
        out_specs=pl.BlockSpec((tile, HEAD_DIM), lambda h, i: (i, h)),
        out_shape=jax.ShapeDtypeStruct((length, n_heads * HEAD_DIM), F32),
        compiler_params=_params("parallel", "parallel"),
    )(proj, proj, proj)


def _attn_bwd(proj, out, d_out, n_heads, q_blk, k_blk, v_blk):
    length = proj.shape[0]
    tile = min(ATTN_TILE, length)
    scale = HEAD_DIM ** -0.5

    def body(q_ref, k_ref, v_ref, o_ref, do_ref, dq_ref, dk_ref, dv_ref):
        i = pl.program_id(1)

        @pl.when(i == 0)
        def _():
            dk_ref[...] = jnp.zeros_like(dk_ref)
            dv_ref[...] = jnp.zeros_like(dv_ref)

        q = (q_ref[...] * scale).astype(BF16)
        do = do_ref[...].astype(BF16)
        total = jnp.sum(do.astype(F32) * o_ref[...], axis=1, keepdims=True)
        upper, upper_incl, diag_mask = _attn_consts(tile)

        def chunk(j, carry, mask):
            run_log, run_g, dq = carry
            start = pl.multiple_of(j * tile, tile)
            kc = k_ref[pl.ds(start, tile), :].astype(BF16)
            vc = v_ref[pl.ds(start, tile), :].astype(BF16)
            w, log_sig, run_log = _stick_weights(q, kc, upper, run_log, mask)
            wb = w.astype(BF16)
            g = wb.astype(F32) * _dot(do, vc, NT)
            g_hi, g_lo = _split(g)
            g_suffix = _dot(g_hi, upper_incl) + _dot(g_lo, upper_incl) + run_g
            dz = g - jnp.exp(log_sig) * (g + (total - g_suffix))
            if mask is not None:
                dz = jnp.where(mask, dz, 0.0)
            dzb = dz.astype(BF16)
            dk_ref[pl.ds(start, tile), :] += _dot(dzb, q, TN)
            dv_ref[pl.ds(start, tile), :] += _dot(wb, do, TN)
            return run_log, g_suffix[:, 0:1], dq + _dot(dzb, kc)

        carry = (jnp.zeros((tile, 1), F32), jnp.zeros((tile, 1), F32), jnp.zeros((tile, HEAD_DIM), F32))
        carry = chunk(i, carry, diag_mask)
        carry = _walk_back(i, lambda j, cr: chunk(j, cr, None), carry)
        dq_ref[...] = (carry[2] * scale).astype(BF16)

    width = n_heads * HEAD_DIM
    tile_spec = pl.BlockSpec((tile, HEAD_DIM), lambda h, i: (i, h))
    head_spec = pl.BlockSpec((length, HEAD_DIM), lambda h, i: (0, h))
    return pl.pallas_call(
        body, name="attn_bwd", grid=(n_heads, length // tile),
        in_specs=[pl.BlockSpec((tile, HEAD_DIM), lambda h, i: (i, q_blk + h)),
                  pl.BlockSpec((length, HEAD_DIM), lambda h, i: (0, k_blk + h)),
                  pl.BlockSpec((length, HEAD_DIM), lambda h, i: (0, v_blk + h)),
                  tile_spec, tile_spec],
        out_specs=[tile_spec, head_spec, head_spec],
        out_shape=[jax.ShapeDtypeStruct((length, width), BF16), jax.ShapeDtypeStruct((length, width), F32),
                   jax.ShapeDtypeStruct((length, width), F32)],
        compiler_params=_params("parallel", "arbitrary"),
    )(proj, proj, proj, out, d_out)


SCAN_CHUNK = 128


def _disc_lam(lam_re, lam_im, log_dt):
    dt = jnp.exp(log_dt)
    mag, phase = jnp.exp(lam_re * dt), lam_im * dt
    bar_re, bar_im = mag * jnp.cos(phase), mag * jnp.sin(phase)
    num_re, den = bar_re - 1.0, lam_re * lam_re + lam_im * lam_im
    return (bar_re, bar_im, (num_re * lam_re + bar_im * lam_im) / den, (bar_im * lam_re - num_re * lam_im) / den)


def _disc_b(cf_re, cf_im, b_re, b_im):
    return cf_re * b_re - cf_im * b_im, cf_re * b_im + cf_im * b_re


SCAN_UNROLL = 8
SUBLANES = 8


def _state_rows(per_group):
    return per_group.reshape(-1, SUBLANES, LANES)


def _swap_parts(x):
    pieces = []
    for k in range(x.shape[0] // (2 * SUBLANES)):
        base = 2 * SUBLANES * k
        pieces += [x[base + SUBLANES:base + 2 * SUBLANES], x[base:base + SUBLANES]]
    return jnp.concatenate(pieces, axis=0)


def _scan_coeffs(re_rows, im_rows, conj):
    same, cross = [], []
    for k in range(re_rows.shape[0]):
        same += [re_rows[k], re_rows[k]]
        cross += [im_rows[k], -im_rows[k]] if conj else [-im_rows[k], im_rows[k]]
    return jnp.concatenate(same, axis=0), jnp.concatenate(cross, axis=0)


def _scan_fwd(bu, bar_re, bar_im):
    length, groups, width = bu.shape
    chunk = min(SCAN_CHUNK, length)

    def body(bu_ref, re_ref, im_ref, st_ref, carry):
        @pl.when(pl.program_id(0) == 0)
        def _():
            carry[...] = jnp.zeros_like(carry)
        a_same, a_cross = _scan_coeffs(re_ref[...], im_ref[...], conj=False)

        def step(blk, x):
            for r in range(SCAN_UNROLL):
                t = blk * SCAN_UNROLL + r
                x = a_same * x + a_cross * _swap_parts(x) + bu_ref[t]
                st_ref[t] = x
            return x

        carry[...] = lax.fori_loop(0, chunk // SCAN_UNROLL, step, carry[...])

    blk = pl.BlockSpec((chunk, groups, width), lambda i: (i, 0, 0))
    par = pl.BlockSpec(bar_re.shape, lambda i: (0, 0, 0))
    return pl.pallas_call(
        body, name="s5_scan_fwd", grid=(length // chunk,), in_specs=[blk, par, par], out_specs=blk,
        out_shape=jax.ShapeDtypeStruct(bu.shape, F32), scratch_shapes=[pltpu.VMEM((groups, width), F32)],
        compiler_params=_params("arbitrary"),
    )(bu, bar_re, bar_im)


def _scan_bwd(d_states, states, bar_re, bar_im):
    length, groups, width = states.shape
    chunk = min(SCAN_CHUNK, length)
    last = length // chunk - 1

    def body(g_ref, st_ref, re_ref, im_ref, out_ref, same_ref, swap_ref, carry):
        @pl.when(pl.program_id(0) == 0)
        def _():
            carry[...] = jnp.zeros_like(carry)
            same_ref[...] = jnp.zeros_like(same_ref)
            swap_ref[...] = jnp.zeros_like(swap_ref)
        a_same, a_cross = _scan_coeffs(re_ref[...], im_ref[...], conj=True)

        def step(blk, cr):
            adj, acc_same, acc_swap = cr
            for r in range(SCAN_UNROLL):
                t = chunk - 1 - (blk * SCAN_UNROLL + r)
                s = st_ref[t]
                acc_same = acc_same + adj * s
                acc_swap = acc_swap + adj * _swap_parts(s)
                adj = g_ref[t] + a_same * adj + a_cross * _swap_parts(adj)
                out_ref[t] = adj
            return adj, acc_same, acc_swap

        adj, acc_same, acc_swap = lax.fori_loop(0, chunk // SCAN_UNROLL, step, (carry[...], same_ref[...], swap_ref[...]))
        carry[...] = adj
        same_ref[...] = acc_same
        swap_ref[...] = acc_swap

    blk = pl.BlockSpec((chunk, groups, width), lambda i: (last - i, 0, 0))
    par = pl.BlockSpec(bar_re.shape, lambda i: (0, 0, 0))
    acc = pl.BlockSpec((groups, width), lambda i: (0, 0))
    return pl.pallas_call(
        body, name="s5_scan_bwd", grid=(length // chunk,), in_specs=[blk, blk, par, par], out_specs=[blk, acc, acc],
        out_shape=[jax.ShapeDtypeStruct(states.shape, F32), jax.ShapeDtypeStruct((groups, width), F32),
                   jax.ShapeDtypeStruct((groups, width), F32)],
        scratch_shapes=[pltpu.VMEM((groups, width), F32)],
        compiler_params=_params("arbitrary"),
    )(d_states, states, bar_re, bar_im)


def _lam_bar_grad(acc_same, acc_swap):
    def fn(same, swap):
        g_re, g_im = [], []
        for k in range(same.shape[0] // (2 * SUBLANES)):
            re, im = slice(2 * SUBLANES * k, 2 * SUBLANES * k + SUBLANES), slice(2 * SUBLANES * k + SUBLANES, 2 * SUBLANES * (k + 1))
            g_re.append(same[re] + same[im])
            g_im.append(swap[im] - swap[re])
        return jnp.concatenate(g_re, axis=0), jnp.concatenate(g_im, axis=0)
    return _whole(fn, "s5_lam_bar_grad", [acc_same, acc_swap], [(acc_same.shape[0] // 2, LANES)] * 2)


def _block_diag(per_group):
    groups, a, b = per_group.shape
    nc = groups // GROUPS_PER_CHUNK
    eye = jnp.eye(GROUPS_PER_CHUNK, dtype=per_group.dtype)
    x = per_group.reshape(nc, GROUPS_PER_CHUNK, a, 1, b) * eye[None, :, None, :, None]
    return x.reshape(nc, GROUPS_PER_CHUNK * a, GROUPS_PER_CHUNK * b)


def _block_diag_part(chunks, a, b):
    nc = chunks.shape[0]
    x = chunks.reshape(nc, GROUPS_PER_CHUNK, a, GROUPS_PER_CHUNK, b)
    x = jnp.stack([x[:, g, :, g, :] for g in range(GROUPS_PER_CHUNK)], axis=1)
    return x.reshape(nc * GROUPS_PER_CHUNK, a, b)


def _epilogue(raw, gate, scale, g):
    return _rms(raw * scale, g) * (gate * _sigmoid(gate))


def _ssm_mid(y, u, d_skip):
    return _gelu(y + d_skip * u)


def _adamw(w, g, m, v):
    m = ADAM_B1 * m + (1.0 - ADAM_B1) * g
    v = ADAM_B2 * v + (1.0 - ADAM_B2) * (g * g)
    m_hat = m / (1.0 - ADAM_B1 ** ADAM_STEP)
    v_hat = v / (1.0 - ADAM_B2 ** ADAM_STEP)
    return -ADAM_LR * (m_hat / (jnp.sqrt(v_hat) + ADAM_EPS) + ADAM_WD * w), m, v


class _Dims:
    def __init__(self, d_model, length):
        self.d, self.length = d_model, length
        self.d_pool, self.d_attn = d_model // 4, d_model // 2
        self.d_ssm = d_model - self.d_pool - self.d_attn
        self.heads = self.d_attn // HEAD_DIM
        self.groups = self.d_ssm // SSM_GROUP
        self.d_in = 2 * self.d_pool + 4 * self.d_attn + 2 * self.d_ssm
        sizes = (self.d_pool, self.d_pool, self.d_attn, self.d_attn, self.d_attn, self.d_attn, self.d_ssm, self.d_ssm)
        offs = [0]
        for s in sizes[:-1]:
            offs.append(offs[-1] + s)
        (self.o_px, self.o_pgate, self.o_q, self.o_k, self.o_v, self.o_agate, self.o_u, self.o_sgate) = offs


def _ssm_operands(dm, p):
    groups, states = dm.groups, SSM_STATE
    bar_re, bar_im, cf_re, cf_im = _whole(_disc_lam, "s5_disc_lam", [p["lam_re"], p["lam_im"], p["log_dt"].reshape(groups, 1)],
                                          [(groups, states)] * 4)
    b_re2, b_im2 = p["b_re"].reshape(groups * states, SSM_GROUP), p["b_im"].reshape(groups * states, SSM_GROUP)
    bb_re, bb_im = _whole(_disc_b, "s5_disc_b", [cf_re.reshape(-1, 1), cf_im.reshape(-1, 1), b_re2, b_im2],
                          [(groups * states, SSM_GROUP)] * 2)
    per_group = lambda a: jnp.swapaxes(a.reshape(groups, states, SSM_GROUP), 1, 2)
    b_blk = jnp.concatenate([_block_diag(per_group(bb_re)), _block_diag(per_group(bb_im))], axis=2)
    c_blk = jnp.concatenate([_block_diag(jnp.swapaxes(p["c_re"], 1, 2)), _block_diag(jnp.swapaxes(-p["c_im"], 1, 2))], axis=1)
    return dict(bar_re=bar_re, bar_im=bar_im, bar_re_rows=_state_rows(bar_re), bar_im_rows=_state_rows(bar_im),
                cf_re=cf_re, cf_im=cf_im, b_re2=b_re2, b_im2=b_im2, b_blk=b_blk, c_blk=c_blk)


def _mm_carrying(carry, key, *args, **kwargs):
    if key not in carry:
        return _mm(*args, **kwargs)
    make, deliver = carry[key]
    exchange = make()
    out = _mm(*args, comm=exchange, **kwargs)
    deliver(exchange.results)
    return out


def _layer_fwd(dm, x_in, p, gw, carry):
    length = dm.length
    blk = lambda off, w: off // w
    (h,), _ = _rowwise(lambda x, g: ((_rms(x, g),), ()), "rms_fwd", [(x_in, dm.d, 0)], [p["ln_g"]], [(dm.d, BF16)])
    proj = _mm_carrying(carry, "in_proj", h, gw["w_in"], NN, F32, "in_proj", n=dm.d_in, b_quarters="n")
    pooled, mixed = _pool_fwd(proj, gw["w_pool"])
    qb, kb, vb = dm.o_q // HEAD_DIM, dm.o_k // HEAD_DIM, dm.o_v // HEAD_DIM
    attn = _attn_fwd(proj, dm.heads, qb, kb, vb)
    so = _ssm_operands(dm, p)
    u_row = (proj, dm.d_ssm, blk(dm.o_u, dm.d_ssm))
    (u,), _ = _rowwise(lambda v: ((v,), ()), "take_u", [u_row], [], [(dm.d_ssm, F32)])
    bu = _mm3(u, so["b_blk"], NN, "s5_bu", rows3d="o")
    states = _scan_fwd(bu, so["bar_re_rows"], so["bar_im_rows"])
    y = _mm3(states, so["c_blk"], NN, "s5_y", rows3d="a")
    (hg,), _ = _rowwise(lambda yy, uu, dsk: ((_ssm_mid(yy, uu, dsk),), ()), "s5_mid_fwd",
                        [(y, dm.d_ssm, 0), (u, dm.d_ssm, 0)], [p["d_skip"]], [(dm.d_ssm, BF16)])
    z = _mm(hg, gw["w_glu"], NN, F32, "glu_proj", n=2 * dm.d_ssm, b_quarters="n")

    def glu(zz, bias):
        zz = zz + bias
        return (zz[:, :dm.d_ssm] * _sigmoid(zz[:, dm.d_ssm:]),), ()

    (ssm,), _ = _rowwise(glu, "glu_fwd", [(z, 2 * dm.d_ssm, 0)], [p["b_glu"]], [(dm.d_ssm, F32)])
    g_pool, g_attn, g_ssm = (p["branch_g"][:, :dm.d_pool], p["branch_g"][:, dm.d_pool:dm.d_pool + dm.d_attn],
                             p["branch_g"][:, dm.d_pool + dm.d_attn:])
    ones_attn, ones_ssm = jnp.ones((1, dm.d_attn), F32), jnp.ones((1, dm.d_ssm), F32)
    epi = lambda raw, gate, scale, g: ((_epilogue(raw, gate, scale, g),), ())
    branches = [("pool", mixed, dm.d_pool, dm.o_pgate, p["pool_scale"], g_pool),
                ("attn", attn, dm.d_attn, dm.o_agate, ones_attn, g_attn),
                ("ssm", ssm, dm.d_ssm, dm.o_sgate, ones_ssm, g_ssm)]
    ys = []
    for nm, raw, w, off, scale, g in branches:
        (yb,), _ = _rowwise(epi, "epilogue_fwd_" + nm, [(raw, w, 0), (proj, w, blk(off, w))], [scale, g], [(w, BF16)])
        ys.append(yb)
    y_cat = jnp.concatenate(ys, axis=1)
    x_out = _mm_carrying(carry, "out_proj", y_cat, gw["w_out"], NN, F32, "out_proj", add=x_in)
    saved = dict(x_in=x_in, h=h, proj=proj, pooled=pooled, mixed=mixed, attn=attn, so=so, u=u, states=states,
                 y=y, hg=hg, z=z, ssm=ssm, y_cat=y_cat, scales=(p["pool_scale"], ones_attn, ones_ssm), gs=(g_pool, g_attn, g_ssm))
    return x_out, saved


def _layer_bwd(dm, d_out, d_out_bf, p, gw, sv, want_bf, carry):
    length = dm.length
    blk = lambda off, w: off // w
    proj = sv["proj"]
    d_y = _mm_carrying(carry, "out_proj_dgrad", d_out_bf, gw["w_out"], NT, F32, "out_proj_dgrad")
    g_w_out = _mm(sv["y_cat"], d_out_bf, TN, F32, "out_proj_wgrad", out_quarters="rows")

    def epi_bwd(nseg):
        def fn(*vals):
            dys, (raw, gate, scale, g) = vals[:nseg], vals[nseg:]
            dyb = dys[0] if nseg == 1 else jnp.concatenate(dys, axis=1)
            _, vjp = jax.vjp(_epilogue, raw, gate, scale, g)
            d_raw, d_gate, d_scale, d_g = vjp(dyb)
            return (d_raw, d_gate), (d_scale, d_g)
        return fn

    branch = [("pool", sv["mixed"], dm.d_pool, dm.o_pgate, 0), ("attn", sv["attn"], dm.d_attn, dm.o_agate, dm.d_pool),
              ("ssm", sv["ssm"], dm.d_ssm, dm.o_sgate, dm.d_pool + dm.d_attn)]
    d_raws, d_gates, d_scales, d_gs = [], [], [], []
    for (nm, raw, w, off, yoff), scale, g in zip(branch, sv["scales"], sv["gs"]):
        seg = math.gcd(w, yoff) if yoff else w
        dy_rows = [(d_y, seg, yoff // seg + s) for s in range(w // seg)]
        (d_raw, d_gate), (d_scale, d_g) = _rowwise(
            epi_bwd(len(dy_rows)), "epilogue_bwd_" + nm, dy_rows + [(raw, w, 0), (proj, w, blk(off, w))], [scale, g],
            [(w, F32), (w, BF16)], [w, w])
        d_raws.append(d_raw); d_gates.append(d_gate); d_scales.append(d_scale); d_gs.append(d_g)
    d_pooled, g_w_pool = _pool_bwd_mix(d_raws[0], sv["pooled"], gw["w_pool"])
    d_px = _pool_bwd_window(d_pooled, len(POOL_WINDOWS))
    qb, kb, vb = dm.o_q // HEAD_DIM, dm.o_k // HEAD_DIM, dm.o_v // HEAD_DIM
    d_q, d_k, d_v = _attn_bwd(proj, sv["attn"], d_raws[1], dm.heads, qb, kb, vb)
    so = sv["so"]

    def glu_bwd(d_ssm, zz, bias):
        zz = zz + bias
        val, sg = zz[:, :dm.d_ssm], _sigmoid(zz[:, dm.d_ssm:])
        dz = jnp.concatenate([d_ssm * sg, d_ssm * val * sg * (1.0 - sg)], axis=1)
        return (dz,), (jnp.sum(dz, axis=0, keepdims=True),)

    (d_z,), (g_b_glu,) = _rowwise(glu_bwd, "glu_bwd", [(d_raws[2], dm.d_ssm, 0), (sv["z"], 2 * dm.d_ssm, 0)], [p["b_glu"]],
                                  [(2 * dm.d_ssm, BF16)], [2 * dm.d_ssm])
    d_hg = _mm(d_z, gw["w_glu"], NT, F32, "glu_dgrad", n=dm.d_ssm, b_quarters="k")
    g_w_glu = _mm(sv["hg"], d_z, TN, F32, "glu_wgrad", out_quarters="cols")

    def mid_bwd(dh, yy, uu, dsk):
        _, vjp = jax.vjp(_ssm_mid, yy, uu, dsk)
        dy_, du_, ddsk = vjp(dh)
        return (dy_, du_), (ddsk,)

    (d_yssm, d_u_direct), (g_d_skip,) = _rowwise(mid_bwd, "s5_mid_bwd", [(d_hg, dm.d_ssm, 0), (sv["y"], dm.d_ssm, 0), (sv["u"], dm.d_ssm, 0)],
                                                 [p["d_skip"]], [(dm.d_ssm, F32), (dm.d_ssm, F32)], [dm.d_ssm])
    d_states = _mm3(d_yssm, so["c_blk"], NT, "s5_y_dgrad", rows3d="o")
    d_c_blk = _mm3(sv["states"], d_yssm, TN, "s5_y_wgrad", nc=so["c_blk"].shape[0], rows3d="a")
    d_bu, acc_same, acc_swap = _scan_bwd(d_states, sv["states"], so["bar_re_rows"], so["bar_im_rows"])
    d_u_scan = _mm3(d_bu, so["b_blk"], NT, "s5_bu_dgrad", rows3d="a")
    d_b_blk = _mm3(sv["u"], d_bu, TN, "s5_bu_wgrad", nc=so["b_blk"].shape[0], rows3d="b")
    (d_u,) = _elementwise(lambda a, b: (a + b,), "s5_du", [d_u_direct, d_u_scan], [BF16])
    groups, states = dm.groups, SSM_STATE
    part = GROUPS_PER_CHUNK * states
    g_c_re = jnp.swapaxes(_block_diag_part(d_c_blk[:, :part], states, SSM_GROUP), 1, 2)
    g_c_im_neg = jnp.swapaxes(_block_diag_part(d_c_blk[:, part:], states, SSM_GROUP), 1, 2)
    d_bb_re = jnp.swapaxes(_block_diag_part(d_b_blk[:, :, :part], SSM_GROUP, states), 1, 2).reshape(-1, SSM_GROUP)
    d_bb_im = jnp.swapaxes(_block_diag_part(d_b_blk[:, :, part:], SSM_GROUP, states), 1, 2).reshape(-1, SSM_GROUP)

    def disc_b_bwd(cf_re, cf_im, b_re, b_im, g_re, g_im):
        _, vjp = jax.vjp(_disc_b, cf_re, cf_im, b_re, b_im)
        return vjp((g_re, g_im))

    d_cf_re, d_cf_im, g_b_re, g_b_im = _whole(
        disc_b_bwd, "s5_disc_b_bwd", [so["cf_re"].reshape(-1, 1), so["cf_im"].reshape(-1, 1), so["b_re2"], so["b_im2"], d_bb_re, d_bb_im],
        [(groups * states, 1)] * 2 + [(groups * states, SSM_GROUP)] * 2)

    def disc_lam_bwd(lam_re, lam_im, log_dt, g_bar_re, g_bar_im, g_cf_re, g_cf_im):
        _, vjp = jax.vjp(_disc_lam, lam_re, lam_im, log_dt)
        return vjp((g_bar_re, g_bar_im, g_cf_re, g_cf_im))

    g_bar_re, g_bar_im = _lam_bar_grad(acc_same, acc_swap)

    g_lam_re, g_lam_im, g_log_dt = _whole(
        disc_lam_bwd, "s5_disc_lam_bwd", [p["lam_re"], p["lam_im"], p["log_dt"].reshape(groups, 1), g_bar_re.reshape(groups, states), g_bar_im.reshape(groups, states),
                                          d_cf_re.reshape(groups, states), d_cf_im.reshape(groups, states)],
        [(groups, states)] * 2 + [(groups, 1)])
    (g_c_im,) = _elementwise(lambda a: (-a,), "s5_neg_c_im", [g_c_im_neg.reshape(groups * SSM_GROUP, states)], [F32])
    d_proj = jnp.concatenate([d_px, d_gates[0], d_q, d_k.astype(BF16), d_v.astype(BF16), d_gates[1], d_u, d_gates[2]], axis=1)
    d_h = _mm_carrying(carry, "in_proj_dgrad", d_proj, gw["w_in"], NT, F32, "in_proj_dgrad", n=dm.d, b_quarters="k")
    g_w_in = _mm_carrying(carry, "in_proj_wgrad", sv["h"], d_proj, TN, F32, "in_proj_wgrad", out_quarters="cols")

    def rms_bwd(dh, xx, dres, g):
        _, vjp = jax.vjp(_rms, xx, g)
        dx, dg = vjp(dh)
        dx = dx + dres
        return ((dx, dx) if want_bf else (dx,)), (dg,)

    d_xs, (g_ln_g,) = _rowwise(rms_bwd, "rms_bwd", [(d_h, dm.d, 0), (sv["x_in"], dm.d, 0), (d_out, dm.d, 0)], [p["ln_g"]],
                               [(dm.d, F32), (dm.d, BF16)] if want_bf else [(dm.d, F32)], [dm.d])
    ngr, ch = gw["w_pool"].shape[0], gw["w_pool"].shape[1]
    q_rows = ch // 4
    g_w_pool2 = g_w_pool.reshape(ngr, 4, 2, q_rows // 2, ch).transpose(2, 1, 0, 3, 4).reshape(2, 4, ngr * q_rows // 2, ch)
    big = dict(w_in=g_w_in, w_out=g_w_out, w_glu=g_w_glu, w_pool=g_w_pool2)
    small = dict(ln_g=g_ln_g, pool_scale=d_scales[0], lam_re=g_lam_re, lam_im=g_lam_im, log_dt=g_log_dt.reshape(1, groups),
                 b_re=g_b_re, b_im=g_b_im, c_re=g_c_re, c_im=g_c_im, d_skip=g_d_skip, b_glu=g_b_glu,
                 branch_g=jnp.concatenate(d_gs, axis=1))
    return d_xs[0], (d_xs[1] if want_bf else None), big, small


SMALL_ROWS = 8


def _pack(arrays):
    parts = []
    for a in arrays:
        flat = a.reshape(-1)
        pad = (-flat.shape[0]) % (SMALL_ROWS * LANES)
        parts.append(jnp.pad(flat, (0, pad)).reshape(-1, LANES))
    return jnp.concatenate(parts, axis=0)


def _unpack(buf, like):
    res, row = [], 0
    for a in like:
        size = math.prod(a.shape)
        rows = -(-size // (SMALL_ROWS * LANES)) * SMALL_ROWS
        res.append(buf[row:row + rows].reshape(-1)[:size].reshape(a.shape))
        row += rows
    return res


def kernel(x, ln_g, w_in, w_pool, pool_scale, lam_re, lam_im, log_dt, b_re, b_im, c_re, c_im, d_skip, w_glu, b_glu, branch_g, w_out, final_g, loss_target, m_ln_g, m_w_in, m_w_pool, m_pool_scale, m_lam_re, m_lam_im, m_log_dt, m_b_re, m_b_im, m_c_re, m_c_im, m_d_skip, m_w_glu, m_b_glu, m_branch_g, m_w_out, m_final_g, v_ln_g, v_w_in, v_w_pool, v_pool_scale, v_lam_re, v_lam_im, v_log_dt, v_b_re, v_b_im, v_c_re, v_c_im, v_d_skip, v_w_glu, v_b_glu, v_branch_g, v_w_out, v_final_g):
    given = dict(locals())
    weights = {n: given[n] for n in WEIGHTS}
    depth = ln_g.shape[0]
    _, length, d_model = x.shape
    dm = _Dims(d_model, length)
    x0, target = x[0], loss_target[0]
    c_idx = lax.axis_index("c")
    my_quarter = 2 * lax.axis_index("x") + lax.axis_index("y")

    shard2d = {(n, l): weights[n][l].reshape(-1, weights[n].shape[-1]) for l in range(depth) for n in SHARDED}
    keys = list(shard2d)
    halves16 = {k: w.astype(BF16).reshape(2, w.shape[0] // 2, -1) for k, w in shard2d.items()}
    gw = [dict() for _ in range(depth)]

    def gather(group, name):
        return _allgather_quarters([halves16[k] for k in group], name)

    def deliver_weights(group):
        def deliver(results):
            for (n, l), g in zip(group, results):
                rows, cols = shard2d[(n, l)].shape
                g = g.reshape(4, rows, cols)
                if n == "w_out":
                    g = g.reshape(4 * rows, cols)
                if n == "w_pool":
                    ngr = w_pool.shape[1]
                    g = g.reshape(4, ngr, rows // ngr, cols).transpose(1, 0, 2, 3).reshape(ngr, 4 * rows // ngr, cols)
                gw[l][n] = g
        return deliver

    first = [("w_in", 0), ("w_pool", 0), ("w_out", 0)]
    behind_in_proj = [("w_glu", 0)] + ([("w_in", 1)] if depth > 1 else [])
    behind_out_proj = [k for k in keys if k not in first + behind_in_proj]
    deliver_weights(first)(gather(first, "allgather_first").run())
    fwd_carry = [dict() for _ in range(depth)]
    fwd_carry[0]["in_proj"] = (lambda: gather(behind_in_proj, "allgather_behind_in_proj"), deliver_weights(behind_in_proj))
    if behind_out_proj:
        fwd_carry[0]["out_proj"] = (lambda: gather(behind_out_proj, "allgather_behind_out_proj"), deliver_weights(behind_out_proj))
    small_names = [n for n in WEIGHTS if n not in SHARDED and n != "final_g"]
    ps = [{n: (weights[n][l].reshape(1, -1) if weights[n][l].ndim == 1 else weights[n][l]) for n in small_names} for l in range(depth)]

    acts, saved = x0, []
    for l in range(depth):
        acts, sv = _layer_fwd(dm, acts, ps[l], gw[l], fwd_carry[l])
        saved.append(sv)

    def final(xx, tt, g):
        def loss_fn(xv, gv):
            err = _rms(xv, gv) - tt
            return 0.5 * jnp.sum(jnp.mean(err * err, axis=-1))
        val, (dx, dg) = jax.value_and_grad(loss_fn, argnums=(0, 1))(xx, g)
        return (dx, dx), (val.reshape(1, 1), dg)

    (d_act, d_act_bf), (loss_part, g_final_g) = _rowwise(
        final, "final_norm_loss", [(acts, dm.d, 0), (target, dm.d, 0)], [final_g.reshape(1, -1)], [(dm.d, F32), (dm.d, BF16)], [1, dm.d])
    loss = lax.psum(loss_part[0, 0], AXES)

    place = jnp.stack([c_idx, my_quarter]).astype(jnp.int32)
    big, small, mine, theirs = [None] * depth, [None] * depth, {}, {}

    def reduction(l):
        group = [(n, l) for n in SHARDED]
        state = {}

        def sums(from_sibling):
            state["chip"] = [_chip_sum(big[l][n], r, place) for (n, _), r in zip(group, from_sibling)]

        def totals(from_chips):
            state["mine"] = [_owner_sum(chip32, r) for (_, chip32), r in zip(state["chip"], from_chips)]

        def done(from_sibling):
            mine.update(zip(group, state["mine"]))
            theirs.update(zip(group, from_sibling))

        tag = "_%d" % l
        return [(lambda: _to_sibling([big[l][n] for n, _ in group], "grads_to_sibling" + tag, other_half=True), sums),
                (lambda: _to_owner_chips([chip16 for chip16, _ in state["chip"]], "grads_to_owner_chips" + tag), totals),
                (lambda: _to_sibling(state["mine"], "reduced_half_to_sibling" + tag), done)]

    bwd_carry = {}
    for l in reversed(range(depth)):
        d_act, d_act_bf, big[l], small[l] = _layer_bwd(dm, d_act, d_act_bf, ps[l], gw[l], saved[l], want_bf=l > 0, carry=bwd_carry)
        stages = reduction(l)
        if l > 0:
            bwd_carry = dict(zip(["out_proj_dgrad", "in_proj_dgrad", "in_proj_wgrad"], stages))
        else:
            for make, deliver in stages:
                deliver(make().run())
    grad_x = d_act[None]

    small_all = [n for n in WEIGHTS if n not in SHARDED]
    packed = _pack([small[l][n] for l in range(depth) for n in small_names] + [g_final_g])
    summed = _unpack(_allreduce_small(packed), [weights[n][l] for l in range(depth) for n in small_names] + [final_g])
    g_small = {n: jnp.stack([summed[l * len(small_names) + i] for l in range(depth)]) for i, n in enumerate(small_names)}
    g_small["final_g"] = summed[-1]

    out_g, out_d, out_m, out_v = {}, {}, {}, {}
    for n in SHARDED:
        shape = weights[n].shape
        if n == "w_pool":
            ngr = shape[1]
            both = jnp.stack([jnp.where(c_idx == 0, jnp.stack([mine[(n, l)], theirs[(n, l)]]), jnp.stack([theirs[(n, l)], mine[(n, l)]]))
                              for l in range(depth)])
            g = both.reshape(depth, 2, ngr, -1, shape[-1]).transpose(0, 2, 1, 3, 4).reshape(shape)
            res = [g] + _elementwise(lambda *a: _adamw(*a), "adamw_" + n, [weights[n], g, given["m_" + n], given["v_" + n]], [F32] * 3)
        else:
            halves = lambda a: a.reshape(depth, 2, -1, shape[-1])
            res = _adamw_halves(halves(weights[n]), halves(given["m_" + n]), halves(given["v_" + n]),
                                [mine[(n, l)] for l in range(depth)], [theirs[(n, l)] for l in range(depth)], place, "adamw_" + n)
        out_g[n], out_d[n], out_m[n], out_v[n] = [r.reshape(shape) for r in res]
    d, m, v = _whole(_adamw, "adamw_small", [_pack([weights[n] for n in small_all]), _pack([g_small[n] for n in small_all]),
                                            _pack([given["m_" + n] for n in small_all]), _pack([given["v_" + n] for n in small_all])],
                     [_pack([weights[n] for n in small_all]).shape] * 3)
    like = [weights[n] for n in small_all]
    for n, dd, mm, vv in zip(small_all, _unpack(d, like), _unpack(m, like), _unpack(v, like)):
        out_g[n], out_d[n], out_m[n], out_v[n] = g_small[n], dd, mm, vv
    return (loss, grad_x, *[out_g[n] for n in WEIGHTS], *[out_d[n] for n in WEIGHTS],
            *[out_m[n] for n in WEIGHTS], *[out_v[n] for n in WEIGHTS])
```

```python
import functools
import math

import jax
import jax.numpy as jnp
from jax import lax
from jax.experimental import pallas as pl
from jax.experimental.pallas import tpu as pltpu

F32 = jnp.float32
BF16 = jnp.bfloat16
EPS = 1e-6
POOL_WINDOWS = (2, 4, 8, 16)
POOL_HALO = 16
HEAD_DIM = 128
SSM_GROUP = 16
SSM_STATE = 64
GROUPS_PER_CHUNK = 16
LANES = 128
VMEM_LIMIT_BYTES = 56 * 1024 * 1024
ROW_TILE_ELEMS = 2 * 1024 * 1024
WIDE_TILE = 1024
ADAM_LR, ADAM_B1, ADAM_B2, ADAM_EPS, ADAM_WD, ADAM_STEP = 0.001, 0.9, 0.999, 1e-08, 0.01, 10
MESH = pl.DeviceIdType.MESH
AXES = ("x", "y", "c")
WEIGHTS = ("ln_g", "w_in", "w_pool", "pool_scale", "lam_re", "lam_im", "log_dt", "b_re", "b_im",
           "c_re", "c_im", "d_skip", "w_glu", "b_glu", "branch_g", "w_out", "final_g")
SHARDED = ("w_in", "w_pool", "w_glu", "w_out")


def _params(*sem):
    return pltpu.CompilerParams(dimension_semantics=sem or None, vmem_limit_bytes=VMEM_LIMIT_BYTES)


def _dot(a, b, dims=((1,), (0,))):
    return lax.dot_general(a, b, (dims, ((), ())), preferred_element_type=F32)


NN, NT, TN = ((1,), (0,)), ((1,), (1,)), ((0,), (0,))


def _split(x):
    hi = x.astype(BF16)
    return hi, (x - hi.astype(F32)).astype(BF16)


def _dot_rounded(a, b, dims):
    return _dot(a.astype(BF16), b.astype(BF16), dims)


def _sigmoid(x):
    return 1.0 / (1.0 + jnp.exp(-x))


def _gelu(x):
    return 0.5 * x * (1.0 + jnp.tanh(0.7978845608028654 * (x + 0.044715 * x * x * x)))


def _rms(x, g):
    return x * lax.rsqrt(jnp.mean(x * x, axis=-1, keepdims=True) + EPS) * g


HBM_SPEC = pl.BlockSpec(memory_space=pltpu.HBM)


def _place():
    x, y, c = lax.axis_index("x"), lax.axis_index("y"), lax.axis_index("c")
    chips = [(1 - x, y), (x, 1 - y), (1 - x, 1 - y)]
    return x, y, c, chips


def _remote(src, dst, send_sem, recv_sem, target):
    return pltpu.make_async_remote_copy(src_ref=src, dst_ref=dst, send_sem=send_sem, recv_sem=recv_sem,
                                        device_id=target, device_id_type=MESH)


class _Exchange:
    def __init__(self, name, inputs, out_shapes, sems, start, finish):
        self.name, self.inputs, self.out_shapes, self.sems = name, list(inputs), list(out_shapes), sems
        self.start, self.finish, self.results = start, finish, None

    def run(self):
        n_in, n_out = len(self.inputs), len(self.out_shapes)

        def body(*refs):
            parts = (refs[:n_in], refs[n_in:n_in + n_out], refs[-2], refs[-1])
            self.start(*parts)
            self.finish(*parts)

        self.results = pl.pallas_call(
            body, name=self.name, in_specs=[HBM_SPEC] * n_in, out_specs=[HBM_SPEC] * n_out, out_shape=self.out_shapes,
            scratch_shapes=[pltpu.SemaphoreType.DMA(self.sems), pltpu.SemaphoreType.DMA(self.sems)],
            compiler_params=pltpu.CompilerParams(has_side_effects=True),
        )(*self.inputs)
        return self.results


def _allgather_quarters(shards, name):
    n, own = len(shards), 6

    def start(ins, outs, send_sems, recv_sems):
        x, y, c, chips = _place()
        mine = 2 * x + y
        for t in range(n):
            for k, (cx, cy) in enumerate(chips):
                _remote(ins[t].at[c], outs[t].at[mine, c], send_sems.at[t, k], recv_sems.at[t, k], (cx, cy, c)).start()
            _remote(ins[t], outs[t].at[mine], send_sems.at[t, own], recv_sems.at[t, own], (x, y, 1 - c)).start()

    def finish(ins, outs, send_sems, recv_sems):
        x, y, c, chips = _place()
        mine, sibling = 2 * x + y, (x, y, 1 - c)
        for t in range(n):
            for k, (cx, cy) in enumerate(chips):
                landed = outs[t].at[2 * cx + cy, c]
                _remote(landed, landed, send_sems.at[t, k], recv_sems.at[t, k], sibling).wait_recv()
                _remote(landed, landed, send_sems.at[t, 3 + k], recv_sems.at[t, 3 + k], sibling).start()
        for t in range(n):
            got = outs[t].at[mine]
            _remote(got, got, send_sems.at[t, own], recv_sems.at[t, own], sibling).wait_recv()
            _remote(ins[t], got, send_sems.at[t, own], recv_sems.at[t, own], sibling).wait_send()
            for k, (cx, cy) in enumerate(chips):
                got = outs[t].at[2 * cx + cy, 1 - c]
                _remote(got, got, send_sems.at[t, 3 + k], recv_sems.at[t, 3 + k], sibling).wait_recv()
                sent = outs[t].at[2 * cx + cy, c]
                _remote(sent, sent, send_sems.at[t, 3 + k], recv_sems.at[t, 3 + k], sibling).wait_send()
                _remote(ins[t].at[c], sent, send_sems.at[t, k], recv_sems.at[t, k], sibling).wait_send()

    return _Exchange(name, shards, [jax.ShapeDtypeStruct((4,) + s.shape, s.dtype) for s in shards], (n, 7), start, finish)


def _to_sibling(arrays, name, other_half=False):
    n = len(arrays)

    def copies(ins, outs, send_sems, recv_sems):
        x, y, c, _ = _place()
        return [_remote(ins[t].at[1 - c] if other_half else ins[t], outs[t], send_sems.at[t], recv_sems.at[t], (x, y, 1 - c))
                for t in range(n)]

    def start(*refs):
        for cp in copies(*refs):
            cp.start()

    def finish(*refs):
        for cp in copies(*refs):
            cp.wait()

    shapes = [jax.ShapeDtypeStruct(a.shape[1:] if other_half else a.shape, a.dtype) for a in arrays]
    return _Exchange(name, arrays, shapes, (n,), start, finish)


def _to_owner_chips(arrays, name):
    n = len(arrays)

    def copies(ins, outs, send_sems, recv_sems):
        x, y, c, chips = _place()
        return [_remote(ins[t].at[2 * cx + cy], outs[t].at[k], send_sems.at[t, k], recv_sems.at[t, k], (cx, cy, c))
                for t in range(n) for k, (cx, cy) in enumerate(chips)]

    def start(*refs):
        for cp in copies(*refs):
            cp.start()

    def finish(*refs):
        for cp in copies(*refs):
            cp.wait()

    shapes = [jax.ShapeDtypeStruct((3,) + a.shape[1:], a.dtype) for a in arrays]
    return _Exchange(name, arrays, shapes, (n, 3), start, finish)


def _fit(tile, dim):
    tile = min(tile, dim)
    step = LANES if tile >= LANES else 8
    tile -= tile % step
    while dim % tile:
        tile -= step
    return tile


def _mm(a, b, dims, out_dtype, name, *, n=None, b_quarters=None, out_quarters=None, add=None, comm=None, tm=512, tn=512, tk=4096):
    if dims == TN:
        k_dim, m_dim = a.shape
    else:
        m_dim, k_dim = a.shape
    if n is None:
        n = b.shape[0] if dims == NT else b.shape[1]
    m_unit = {None: m_dim, "cols": m_dim // 2, "rows": m_dim // 8}[out_quarters]
    n_unit = n // 4 if (b_quarters == "n" or out_quarters == "cols") else n
    k_unit = k_dim // 4 if b_quarters == "k" else k_dim
    tm, tn, tk = _fit(tm, m_unit), _fit(tn, n_unit), _fit(tk, k_unit)
    gm, gn, gk = m_dim // tm, n // tn, k_dim // tk
    mb, nb, kb = m_unit // tm, n_unit // tn, k_unit // tk
    if dims == TN:
        a_spec = pl.BlockSpec((tk, tm), lambda i, j, k: (k, i))
    else:
        a_spec = pl.BlockSpec((tm, tk), lambda i, j, k: (i, k))
    if b_quarters == "n":
        bspec = pl.BlockSpec((None, tk, tn), lambda i, j, k: (j // nb, k, j % nb))
    elif b_quarters == "k":
        bspec = pl.BlockSpec((None, tn, tk), lambda i, j, k: (k // kb, j, k % kb))
    elif dims == NT:
        bspec = pl.BlockSpec((tn, tk), lambda i, j, k: (j, k))
    else:
        bspec = pl.BlockSpec((tk, tn), lambda i, j, k: (k, j))
    if out_quarters == "cols":
        out_shape = (2, 4, m_unit, n_unit)
        out_spec = pl.BlockSpec((None, None, tm, tn), lambda i, j, k: (i // mb, j // nb, i % mb, j % nb))
    elif out_quarters == "rows":
        out_shape = (2, 4, m_unit, n)
        out_spec = pl.BlockSpec((None, None, tm, tn), lambda i, j, k: ((i // mb) % 2, i // (2 * mb), i % mb, j))
    else:
        out_shape, out_spec = (m_dim, n), pl.BlockSpec((tm, tn), lambda i, j, k: (i, j))
    in_specs, operands = [a_spec, bspec], [a, b]
    if add is not None:
        in_specs.append(pl.BlockSpec((tm, tn), lambda i, j, k: (i, j)))
        operands.append(add)

    n_in = len(operands)
    n_cin, n_cout = (len(comm.inputs), len(comm.out_shapes)) if comm is not None else (0, 0)

    def body(*refs):
        a_ref, b_ref = refs[0], refs[1]
        add_ref = refs[2] if add is not None else None
        o_ref = refs[n_in + n_cin]
        if comm is not None:
            c_refs = (refs[n_in:n_in + n_cin], refs[n_in + n_cin + 1:n_in + n_cin + 1 + n_cout], refs[-2], refs[-1])
            step = (pl.program_id(0) * gn + pl.program_id(1)) * gk + pl.program_id(2)

            @pl.when(step == 0)
            def _():
                comm.start(*c_refs)

        def finish(r):
            if add_ref is not None:
                r = r + add_ref[...]
            o_ref[...] = r.astype(o_ref.dtype)

        if gk == 1:
            finish(_dot(a_ref[...], b_ref[...], dims))
        else:
            acc = refs[n_in + n_cin + 1 + n_cout]
            k = pl.program_id(2)

            @pl.when(k == 0)
            def _():
                acc[...] = jnp.zeros_like(acc)

            acc[...] += _dot(a_ref[...], b_ref[...], dims)

            @pl.when(k == gk - 1)
            def _():
                finish(acc[...])

        if comm is not None:
            @pl.when(step == gm * gn * gk - 1)
            def _():
                comm.finish(*c_refs)

    scratch = [pltpu.VMEM((tm, tn), F32)] if gk > 1 else []
    if comm is None:
        return pl.pallas_call(
            body, name=name, grid=(gm, gn, gk), in_specs=in_specs, out_specs=out_spec,
            out_shape=jax.ShapeDtypeStruct(out_shape, out_dtype), scratch_shapes=scratch,
            compiler_params=_params("parallel", "parallel", "arbitrary"),
        )(*operands)
    res = pl.pallas_call(
        body, name=name, grid=(gm, gn, gk), in_specs=in_specs + [HBM_SPEC] * n_cin, out_specs=[out_spec] + [HBM_SPEC] * n_cout,
        out_shape=[jax.ShapeDtypeStruct(out_shape, out_dtype)] + list(comm.out_shapes),
        scratch_shapes=scratch + [pltpu.SemaphoreType.DMA(comm.sems), pltpu.SemaphoreType.DMA(comm.sems)],
        compiler_params=pltpu.CompilerParams(dimension_semantics=("arbitrary",) * 3, vmem_limit_bytes=VMEM_LIMIT_BYTES,
                                             has_side_effects=True),
    )(*operands, *comm.inputs)
    comm.results = list(res[1:])
    return res[0]


def _mm3(a, b, dims, name, nc=None, tm=512, rows3d=()):
    rows = a.shape[0]
    tm = min(tm, rows)
    gm = rows // tm

    def row_spec(width, three_d):
        if three_d:
            return pl.BlockSpec((tm, width // LANES, LANES), lambda c, i: (i, c, 0))
        return pl.BlockSpec((tm, width), lambda c, i: (i, c))

    def load(ref, three_d):
        if not three_d:
            return ref[...]
        return jnp.concatenate([ref[:, r, :] for r in range(ref.shape[1])], axis=1)

    if dims == TN:
        ka = a.shape[1] * (a.shape[2] if "a" in rows3d else 1) // nc
        nb = b.shape[1] * (b.shape[2] if "b" in rows3d else 1) // nc

        def body(a_ref, b_ref, o_ref):
            @pl.when(pl.program_id(1) == 0)
            def _():
                o_ref[...] = jnp.zeros_like(o_ref)
            o_ref[...] += _dot_rounded(load(a_ref, "a" in rows3d), load(b_ref, "b" in rows3d), TN)

        return pl.pallas_call(
            body, name=name, grid=(nc, gm),
            in_specs=[row_spec(ka, "a" in rows3d), row_spec(nb, "b" in rows3d)],
            out_specs=pl.BlockSpec((None, ka, nb), lambda c, i: (c, 0, 0)),
            out_shape=jax.ShapeDtypeStruct((nc, ka, nb), F32),
            compiler_params=_params("parallel", "arbitrary"),
        )(a, b)
    nc, ka, nb = b.shape
    wa, wo = (ka, nb) if dims == NN else (nb, ka)

    def body(a_ref, b_ref, o_ref):
        res = _dot_rounded(load(a_ref, "a" in rows3d), b_ref[...], dims)
        if "o" in rows3d:
            for r in range(wo // LANES):
                o_ref[:, r, :] = res[:, r * LANES:(r + 1) * LANES]
        else:
            o_ref[...] = res

    out_shape = (rows, nc * wo // LANES, LANES) if "o" in rows3d else (rows, nc * wo)
    return pl.pallas_call(
        body, name=name, grid=(nc, gm),
        in_specs=[row_spec(wa, "a" in rows3d), pl.BlockSpec((None, ka, nb), lambda c, i: (c, 0, 0))],
        out_specs=row_spec(wo, "o" in rows3d),
        out_shape=jax.ShapeDtypeStruct(out_shape, F32),
        compiler_params=_params("parallel", "parallel"),
    )(a, b)


def _rowwise(fn, name, rows, vecs=(), outs=(), sums=()):
    length = rows[0][0].shape[0]
    total = sum(w for _, w, _ in rows) + sum(w for w, _ in outs)
    tile = 8
    while tile * 2 <= min(length, 512) and tile * 2 * total <= ROW_TILE_ELEMS:
        tile *= 2
    assert length % tile == 0
    n_r, n_v, n_o = len(rows), len(vecs), len(outs)

    def body(*refs):
        vals = [r[...] for r in refs[:n_r + n_v]]
        o_refs = refs[n_r + n_v:n_r + n_v + n_o]
        s_refs = refs[n_r + n_v + n_o:]
        res_o, res_s = fn(*vals)
        for ref, val in zip(o_refs, res_o):
            ref[...] = val.astype(ref.dtype)
        if s_refs:
            @pl.when(pl.program_id(0) == 0)
            def _():
                for ref in s_refs:
                    ref[...] = jnp.zeros_like(ref)
            for ref, val in zip(s_refs, res_s):
                ref[...] += val

    def row_spec(w, cb):
        return pl.BlockSpec((tile, w), lambda i: (i, cb))

    res = pl.pallas_call(
        body, name=name, grid=(length // tile,),
        in_specs=[row_spec(w, cb) for _, w, cb in rows] + [pl.BlockSpec(v.shape, lambda i: (0, 0)) for v in vecs],
        out_specs=[row_spec(w, 0) for w, _ in outs] + [pl.BlockSpec((1, w), lambda i: (0, 0)) for w in sums],
        out_shape=[jax.ShapeDtypeStruct((length, w), dt) for w, dt in outs]
        + [jax.ShapeDtypeStruct((1, w), F32) for w in sums],
        compiler_params=_params("arbitrary" if sums else "parallel"),
    )(*[a for a, _, _ in rows], *vecs)
    return res[:n_o], res[n_o:]


def _elementwise(fn, name, arrays, out_dtypes):
    shape = arrays[0].shape
    cols = shape[-1]
    flat = [a.reshape(-1, cols) for a in arrays]
    rows = flat[0].shape[0]
    tile = 8
    while tile * 2 <= rows and rows % (tile * 2) == 0 and tile * 2 * cols * (len(arrays) + len(out_dtypes)) <= ROW_TILE_ELEMS:
        tile *= 2
    assert rows % tile == 0
    n_in = len(flat)

    def body(*refs):
        res = fn(*[r[...] for r in refs[:n_in]])
        for ref, val in zip(refs[n_in:], res):
            ref[...] = val.astype(ref.dtype)

    spec = pl.BlockSpec((tile, cols), lambda i: (i, 0))
    res = pl.pallas_call(
        body, name=name, grid=(rows // tile,), in_specs=[spec] * n_in, out_specs=[spec] * len(out_dtypes),
        out_shape=[jax.ShapeDtypeStruct((rows, cols), dt) for dt in out_dtypes],
        compiler_params=_params("parallel"),
    )(*flat)
    return [r.reshape(shape) for r in res]


def _whole(fn, name, arrays, out_shapes):
    n_in = len(arrays)

    def body(*refs):
        res = fn(*[r[...] for r in refs[:n_in]])
        for ref, val in zip(refs[n_in:], res):
            ref[...] = val

    return pl.pallas_call(
        body, name=name, out_shape=[jax.ShapeDtypeStruct(s, F32) for s in out_shapes],
        compiler_params=_params(),
    )(*arrays)


def _grad_tile(rows, cols, arrays_per_step):
    tile = 8
    while tile * 2 <= rows and rows % (tile * 2) == 0 and tile * 2 * cols * arrays_per_step <= ROW_TILE_ELEMS:
        tile *= 2
    return tile


def _chip_sum(partial, from_sibling, place):
    _, _, rows, cols = partial.shape
    tile = _grad_tile(rows, cols, 4)

    def body(place_ref, a_ref, b_ref, o16_ref, o32_ref):
        total = a_ref[...] + b_ref[...]
        o16_ref[...] = total.astype(BF16)

        @pl.when(pl.program_id(1) == place_ref[1])
        def _():
            o32_ref[...] = total

    return pl.pallas_call(
        body, name="chip_sum",
        grid_spec=pltpu.PrefetchScalarGridSpec(
            num_scalar_prefetch=1, grid=(rows // tile, 4),
            in_specs=[pl.BlockSpec((None, None, tile, cols), lambda i, q, p: (p[0], q, i, 0)),
                      pl.BlockSpec((None, tile, cols), lambda i, q, p: (q, i, 0))],
            out_specs=[pl.BlockSpec((None, tile, cols), lambda i, q, p: (q, i, 0)),
                       pl.BlockSpec((tile, cols), lambda i, q, p: (i, 0))]),
        out_shape=[jax.ShapeDtypeStruct((4, rows, cols), BF16), jax.ShapeDtypeStruct((rows, cols), F32)],
        compiler_params=_params("parallel", "arbitrary"),
    )(place, partial, from_sibling)


def _owner_sum(own, from_chips):
    rows, cols = own.shape
    tile = _grad_tile(rows, cols, 4)

    def body(a_ref, r0_ref, r1_ref, r2_ref, o_ref):
        o_ref[...] = a_ref[...] + r0_ref[...].astype(F32) + r1_ref[...].astype(F32) + r2_ref[...].astype(F32)

    spec = pl.BlockSpec((tile, cols), lambda i: (i, 0))
    return pl.pallas_call(
        body, name="owner_sum", grid=(rows // tile,),
        in_specs=[spec] + [pl.BlockSpec((None, tile, cols), functools.partial(lambda i, k: (k, i, 0), k=k)) for k in range(3)],
        out_specs=spec, out_shape=jax.ShapeDtypeStruct((rows, cols), F32), compiler_params=_params("parallel"),
    )(own, from_chips, from_chips, from_chips)


def _adamw_halves(w, m, v, mine, theirs, place, name):
    depth, _, rows, cols = w.shape
    tile = _grad_tile(rows, cols, 9)

    def body(place_ref, w_ref, m_ref, v_ref, *refs):
        g_refs, outs = refs[:2 * depth], refs[2 * depth:]
        layer, half = pl.program_id(0), pl.program_id(1)
        g = None
        for d in range(depth):
            gd = jnp.where(half == place_ref[0], g_refs[2 * d][...], g_refs[2 * d + 1][...])
            g = gd if g is None else jnp.where(layer == d, gd, g)
        delta, m_new, v_new = _adamw(w_ref[...], g, m_ref[...], v_ref[...])
        for ref, val in zip(outs, (g, delta, m_new, v_new)):
            ref[...] = val

    full = pl.BlockSpec((None, None, tile, cols), lambda l, h, i, p: (l, h, i, 0))
    g_specs = []
    for d in range(depth):
        g_specs += [pl.BlockSpec((tile, cols), functools.partial(lambda l, h, i, p, d: (jnp.where(l == d, i, 0), 0), d=d))] * 2
    operands = [x for pair in zip(mine, theirs) for x in pair]
    return pl.pallas_call(
        body, name=name,
        grid_spec=pltpu.PrefetchScalarGridSpec(
            num_scalar_prefetch=1, grid=(depth, 2, rows // tile),
            in_specs=[full] * 3 + g_specs, out_specs=[full] * 4),
        out_shape=[jax.ShapeDtypeStruct(w.shape, F32)] * 4,
        compiler_params=_params("arbitrary", "arbitrary", "arbitrary"),
    )(place, w, m, v, *operands)


def _allreduce_small(flat):
    rows = flat.shape[0] + (-flat.shape[0]) % (2 * SUBLANES)
    half = rows // 2
    padded = jnp.pad(flat, ((0, rows - flat.shape[0]), (0, 0)))
    swap, final = 0, 4

    def body(in_ref, out_ref, sibling_ref, chips_ref, total_ref, send_sems, recv_sems):
        x, y, c, chips = _place()
        mine, sibling = 2 * x + y, (x, y, 1 - c)
        my_rows = pl.ds(pl.multiple_of(c * half, SUBLANES), half)
        their_rows = pl.ds(pl.multiple_of((1 - c) * half, SUBLANES), half)
        cp = _remote(in_ref, sibling_ref, send_sems.at[swap], recv_sems.at[swap], sibling)
        cp.start()
        cp.wait()
        chips_ref[mine] = in_ref[my_rows, :] + sibling_ref[my_rows, :]
        sends = [_remote(chips_ref.at[mine], chips_ref.at[mine], send_sems.at[1 + k], recv_sems.at[1 + k], (cx, cy, c))
                 for k, (cx, cy) in enumerate(chips)]
        for cp in sends:
            cp.start()
        for k, (cx, cy) in enumerate(chips):
            slot = chips_ref.at[2 * cx + cy]
            _remote(slot, slot, send_sems.at[1 + k], recv_sems.at[1 + k], sibling).wait_recv()
        for cp in sends:
            cp.wait_send()
        total_ref[my_rows, :] = (chips_ref[0] + chips_ref[1]) + (chips_ref[2] + chips_ref[3])
        cp = _remote(total_ref.at[my_rows, :], total_ref.at[my_rows, :], send_sems.at[final], recv_sems.at[final], sibling)
        cp.start()
        _remote(total_ref.at[their_rows, :], total_ref.at[their_rows, :], send_sems.at[final], recv_sems.at[final], sibling).wait_recv()
        cp.wait_send()
        out_ref[...] = total_ref[...]

    out = pl.pallas_call(
        body, name="allreduce_small_grads",
        in_specs=[pl.BlockSpec(memory_space=pltpu.VMEM)], out_specs=pl.BlockSpec(memory_space=pltpu.VMEM),
        out_shape=jax.ShapeDtypeStruct((rows, LANES), F32),
        scratch_shapes=[pltpu.VMEM((rows, LANES), F32), pltpu.VMEM((4, half, LANES), F32), pltpu.VMEM((rows, LANES), F32),
                        pltpu.SemaphoreType.DMA((5,)), pltpu.SemaphoreType.DMA((5,))],
        compiler_params=pltpu.CompilerParams(has_side_effects=True, vmem_limit_bytes=VMEM_LIMIT_BYTES),
    )(padded)
    return out[:flat.shape[0]]


def _pool_tile(length):
    return min(256, length)


def _pool_fwd(proj, w_pool):
    length = proj.shape[0]
    ngroups, ch, _ = w_pool.shape
    width, tile = ngroups * ch, _pool_tile(length)

    def body(cur_ref, prev_ref, w_ref, pooled_ref, mixed_ref):
        i = pl.program_id(0)
        cur = cur_ref[...]
        tail = jnp.where(i > 0, prev_ref[tile - POOL_HALO:tile, :], 0.0)
        padded = jnp.concatenate([tail, cur], axis=0)
        pos = (lax.broadcasted_iota(jnp.int32, (tile, 1), 0) + i * tile + 1).astype(F32)
        for g, window in enumerate(POOL_WINDOWS):
            cols = slice(g * ch, (g + 1) * ch)
            run, shift = padded[:, cols], 1
            while shift < window:
                run = run + pltpu.roll(run, shift, 0)
                shift *= 2
            pooled = (run[POOL_HALO:, :] / jnp.minimum(pos, float(window)) - cur[:, cols]).astype(BF16)
            pooled_ref[:, cols] = pooled
            mixed_ref[:, cols] = _dot(pooled, w_ref[g])

    return pl.pallas_call(
        body, name="pool_fwd", grid=(length // tile,),
        in_specs=[pl.BlockSpec((tile, width), lambda i: (i, 0)),
                  pl.BlockSpec((tile, width), lambda i: (jnp.maximum(i - 1, 0), 0)),
                  pl.BlockSpec(w_pool.shape, lambda i: (0, 0, 0))],
        out_specs=[pl.BlockSpec((tile, width), lambda i: (i, 0))] * 2,
        out_shape=[jax.ShapeDtypeStruct((length, width), BF16), jax.ShapeDtypeStruct((length, width), F32)],
        compiler_params=_params("parallel"),
    )(proj, proj, w_pool)


def _pool_bwd_mix(d_mixed, pooled, w_pool):
    length, width = d_mixed.shape
    ngroups, ch, _ = w_pool.shape
    tile = _pool_tile(length)

    def body(dm_ref, pooled_ref, w_ref, dp_ref, dw_ref):
        @pl.when(pl.program_id(0) == 0)
        def _():
            dw_ref[...] = jnp.zeros_like(dw_ref)
        for g in range(ngroups):
            cols = slice(g * ch, (g + 1) * ch)
            dm = dm_ref[:, cols].astype(BF16)
            dp_ref[:, cols] = _dot(dm, w_ref[g], NT)
            dw_ref[g] += _dot(pooled_ref[:, cols], dm, TN)

    return pl.pallas_call(
        body, name="pool_bwd_mix", grid=(length // tile,),
        in_specs=[pl.BlockSpec((tile, width), lambda i: (i, 0)), pl.BlockSpec((tile, width), lambda i: (i, 0)),
                  pl.BlockSpec(w_pool.shape, lambda i: (0, 0, 0))],
        out_specs=[pl.BlockSpec((tile, width), lambda i: (i, 0)), pl.BlockSpec(w_pool.shape, lambda i: (0, 0, 0))],
        out_shape=[jax.ShapeDtypeStruct((length, width), F32), jax.ShapeDtypeStruct(w_pool.shape, F32)],
        compiler_params=_params("arbitrary"),
    )(d_mixed, pooled, w_pool)


def _pool_bwd_window(d_pooled, ngroups):
    length, width = d_pooled.shape
    ch, tile = width // ngroups, _pool_tile(length)
    last = length // tile - 1

    def body(cur_ref, next_ref, dx_ref):
        i = pl.program_id(0)
        cur = cur_ref[...]
        head = jnp.where(i < last, next_ref[0:POOL_HALO, :], 0.0)
        padded = jnp.concatenate([cur, head], axis=0)
        rows = tile + POOL_HALO
        pos = (lax.broadcasted_iota(jnp.int32, (rows, 1), 0) + i * tile + 1).astype(F32)
        for g, window in enumerate(POOL_WINDOWS):
            cols = slice(g * ch, (g + 1) * ch)
            run, shift = padded[:, cols] / jnp.minimum(pos, float(window)), 1
            while shift < window:
                run = run + pltpu.roll(run, rows - shift, 0)
                shift *= 2
            dx_ref[:, cols] = (run[0:tile, :] - cur[:, cols]).astype(BF16)

    return pl.pallas_call(
        body, name="pool_bwd_window", grid=(length // tile,),
        in_specs=[pl.BlockSpec((tile, width), lambda i: (i, 0)),
                  pl.BlockSpec((tile, width), lambda i: (jnp.minimum(i + 1, last), 0))],
        out_specs=pl.BlockSpec((tile, width), lambda i: (i, 0)),
        out_shape=jax.ShapeDtypeStruct((length, width), BF16),
        compiler_params=_params("parallel"),
    )(d_pooled, d_pooled)


ATTN_TILE = 256
LOG_WEIGHT_FLOOR = -110.0


def _walk_back(n_chunks, chunk, carry):
    def cond(state):
        return jnp.logical_and(state[0] < n_chunks, jnp.max(state[1]) > LOG_WEIGHT_FLOOR)

    def step(state):
        return (state[0] + 1,) + tuple(chunk(n_chunks - 1 - state[0], tuple(state[1:])))

    return lax.while_loop(cond, step, (jnp.int32(0),) + tuple(carry))[1:]


def _stick_weights(q, kc, upper, run_log, mask):
    z = _dot(q, kc, NT)
    e = jnp.exp(-jnp.abs(z))
    softplus = jnp.maximum(z, 0.0) + jnp.log(1.0 + e)
    log_sig = z - softplus
    log_1m = -softplus if mask is None else jnp.where(mask, -softplus, 0.0)
    suffix = _dot(log_1m.astype(BF16), upper) + run_log
    w = jnp.exp(log_sig + suffix)
    if mask is not None:
        w = jnp.where(mask, w, 0.0)
    return w, log_sig, suffix[:, 0:1] + log_1m[:, 0:1]


def _attn_consts(tile):
    jj = lax.broadcasted_iota(jnp.int32, (tile, tile), 0)
    ss = lax.broadcasted_iota(jnp.int32, (tile, tile), 1)
    return (jj > ss).astype(BF16), (jj >= ss).astype(BF16), ss < jj


HEADS_PER_STEP = 2


def _heads_per_step(n_heads, *blocks):
    ok = n_heads % HEADS_PER_STEP == 0 and all(b % HEADS_PER_STEP == 0 for b in blocks)
    return HEADS_PER_STEP if ok else 1


def _slowest(run_logs):
    out = run_logs[0]
    for r in run_logs[1:]:
        out = jnp.maximum(out, r)
    return out


def _attn_fwd(proj, n_heads, q_blk, k_blk, v_blk):
    length = proj.shape[0]
    tile = min(ATTN_TILE, length)
    scale = HEAD_DIM ** -0.5

    hps = _heads_per_step(n_heads, q_blk, k_blk, v_blk)
    width = hps * HEAD_DIM

    def body(q_ref, k_ref, v_ref, o_ref):
        i = pl.program_id(1)
        cols = [slice(h * HEAD_DIM, (h + 1) * HEAD_DIM) for h in range(hps)]
        qs = [(q_ref[:, c] * scale).astype(BF16) for c in cols]
        upper, _, diag_mask = _attn_consts(tile)

        def chunk(j, carry, mask):
            start = pl.multiple_of(j * tile, tile)
            new = []
            for h, c in enumerate(cols):
                run_log, acc = carry[1 + 2 * h], carry[2 + 2 * h]
                kc = k_ref[pl.ds(start, tile), c].astype(BF16)
                vc = v_ref[pl.ds(start, tile), c].astype(BF16)
                w, _, run_log = _stick_weights(qs[h], kc, upper, run_log, mask)
                new += [run_log, acc + _dot(w.astype(BF16), vc)]
            return (_slowest(new[0::2]),) + tuple(new)

        zero = jnp.zeros((tile, 1), F32)
        carry = (zero,) + (zero, jnp.zeros((tile, HEAD_DIM), F32)) * hps
        carry = chunk(i, carry, diag_mask)
        carry = _walk_back(i, lambda j, cr: chunk(j, cr, None), carry)
        for h, c in enumerate(cols):
            o_ref[:, c] = carry[2 + 2 * h]

    return pl.pallas_call(
        body, name="attn_fwd", grid=(n_heads // hps, length // tile),
        in_specs=[pl.BlockSpec((tile, width), lambda h, i: (i, q_blk // hps + h)),
                  pl.BlockSpec((length, width), lambda h, i: (0, k_blk // hps + h)),
                  pl.BlockSpec((length, width), lambda h, i: (0, v_blk // hps + h))],
        out_specs=pl.BlockSpec((tile, width), lambda h, i: (i, h)),
        out_shape=jax.ShapeDtypeStruct((length, n_heads * HEAD_DIM), F32),
        compiler_params=_params("parallel", "parallel"),
    )(proj, proj, proj)


def _attn_bwd(proj, out, d_out, n_heads, q_blk, k_blk, v_blk):
    length = proj.shape[0]
    tile = min(ATTN_TILE, length)
    scale = HEAD_DIM ** -0.5

    hps = _heads_per_step(n_heads, q_blk, k_blk, v_blk)
    step_width = hps * HEAD_DIM

    def body(q_ref, k_ref, v_ref, o_ref, do_ref, dq_ref, dk_ref, dv_ref):
        i = pl.program_id(1)

        @pl.when(i == 0)
        def _():
            dk_ref[...] = jnp.zeros_like(dk_ref)
            dv_ref[...] = jnp.zeros_like(dv_ref)

        cols = [slice(h * HEAD_DIM, (h + 1) * HEAD_DIM) for h in range(hps)]
        qs = [(q_ref[:, c] * scale).astype(BF16) for c in cols]
        dos = [do_ref[:, c].astype(BF16) for c in cols]
        totals = [jnp.sum(do.astype(F32) * o_ref[:, c], axis=1, keepdims=True) for do, c in zip(dos, cols)]
        upper, upper_incl, diag_mask = _attn_consts(tile)

        def chunk(j, carry, mask):
            start = pl.multiple_of(j * tile, tile)
            new = []
            for h, c in enumerate(cols):
                run_log, run_g, dq = carry[1 + 3 * h:4 + 3 * h]
                q, do = qs[h], dos[h]
                kc = k_ref[pl.ds(start, tile), c].astype(BF16)
                vc = v_ref[pl.ds(start, tile), c].astype(BF16)
                w, log_sig, run_log = _stick_weights(q, kc, upper, run_log, mask)
                wb = w.astype(BF16)
                g = wb.astype(F32) * _dot(do, vc, NT)
                g_hi, g_lo = _split(g)
                g_suffix = _dot(g_hi, upper_incl) + _dot(g_lo, upper_incl) + run_g
                dz = g - jnp.exp(log_sig) * (g + (totals[h] - g_suffix))
                if mask is not None:
                    dz = jnp.where(mask, dz, 0.0)
                dzb = dz.astype(BF16)
                dk_ref[pl.ds(start, tile), c] += _dot(dzb, q, TN)
                dv_ref[pl.ds(start, tile), c] += _dot(wb, do, TN)
                new += [run_log, g_suffix[:, 0:1], dq + _dot(dzb, kc)]
            return (_slowest(new[0::3]),) + tuple(new)

        zero = jnp.zeros((tile, 1), F32)
        carry = (zero,) + (zero, zero, jnp.zeros((tile, HEAD_DIM), F32)) * hps
        carry = chunk(i, carry, diag_mask)
        carry = _walk_back(i, lambda j, cr: chunk(j, cr, None), carry)
        for h, c in enumerate(cols):
            dq_ref[:, c] = (carry[3 + 3 * h] * scale).astype(BF16)

    width = n_heads * HEAD_DIM
    tile_spec = pl.BlockSpec((tile, step_width), lambda h, i: (i, h))
    head_spec = pl.BlockSpec((length, step_width), lambda h, i: (0, h))
    return pl.pallas_call(
        body, name="attn_bwd", grid=(n_heads // hps, length // tile),
        in_specs=[pl.BlockSpec((tile, step_width), lambda h, i: (i, q_blk // hps + h)),
                  pl.BlockSpec((length, step_width), lambda h, i: (0, k_blk // hps + h)),
                  pl.BlockSpec((length, step_width), lambda h, i: (0, v_blk // hps + h)),
                  tile_spec, tile_spec],
        out_specs=[tile_spec, head_spec, head_spec],
        out_shape=[jax.ShapeDtypeStruct((length, width), BF16), jax.ShapeDtypeStruct((length, width), F32),
                   jax.ShapeDtypeStruct((length, width), F32)],
        compiler_params=_params("parallel", "arbitrary"),
    )(proj, proj, proj, out, d_out)


SCAN_CHUNK = 128


def _disc_lam(lam_re, lam_im, log_dt):
    dt = jnp.exp(log_dt)
    mag, phase = jnp.exp(lam_re * dt), lam_im * dt
    bar_re, bar_im = mag * jnp.cos(phase), mag * jnp.sin(phase)
    num_re, den = bar_re - 1.0, lam_re * lam_re + lam_im * lam_im
    return (bar_re, bar_im, (num_re * lam_re + bar_im * lam_im) / den, (bar_im * lam_re - num_re * lam_im) / den)


def _disc_b(cf_re, cf_im, b_re, b_im):
    return cf_re * b_re - cf_im * b_im, cf_re * b_im + cf_im * b_re


SCAN_UNROLL = 8
SUBLANES = 8


def _state_rows(per_group):
    return per_group.reshape(-1, SUBLANES, LANES)


def _swap_parts(x):
    pieces = []
    for k in range(x.shape[0] // (2 * SUBLANES)):
        base = 2 * SUBLANES * k
        pieces += [x[base + SUBLANES:base + 2 * SUBLANES], x[base:base + SUBLANES]]
    return jnp.concatenate(pieces, axis=0)


def _scan_coeffs(re_rows, im_rows, conj):
    same, cross = [], []
    for k in range(re_rows.shape[0]):
        same += [re_rows[k], re_rows[k]]
        cross += [im_rows[k], -im_rows[k]] if conj else [-im_rows[k], im_rows[k]]
    return jnp.concatenate(same, axis=0), jnp.concatenate(cross, axis=0)


def _scan_fwd(bu, bar_re, bar_im):
    length, groups, width = bu.shape
    chunk = min(SCAN_CHUNK, length)

    def body(bu_ref, re_ref, im_ref, st_ref, carry):
        @pl.when(pl.program_id(0) == 0)
        def _():
            carry[...] = jnp.zeros_like(carry)
        a_same, a_cross = _scan_coeffs(re_ref[...], im_ref[...], conj=False)

        def step(blk, x):
            for r in range(SCAN_UNROLL):
                t = blk * SCAN_UNROLL + r
                x = a_same * x + a_cross * _swap_parts(x) + bu_ref[t]
                st_ref[t] = x
            return x

        carry[...] = lax.fori_loop(0, chunk // SCAN_UNROLL, step, carry[...])

    blk = pl.BlockSpec((chunk, groups, width), lambda i: (i, 0, 0))
    par = pl.BlockSpec(bar_re.shape, lambda i: (0, 0, 0))
    return pl.pallas_call(
        body, name="s5_scan_fwd", grid=(length // chunk,), in_specs=[blk, par, par], out_specs=blk,
        out_shape=jax.ShapeDtypeStruct(bu.shape, F32), scratch_shapes=[pltpu.VMEM((groups, width), F32)],
        compiler_params=_params("arbitrary"),
    )(bu, bar_re, bar_im)


def _scan_bwd(d_states, states, bar_re, bar_im):
    length, groups, width = states.shape
    chunk = min(SCAN_CHUNK, length)
    last = length // chunk - 1

    def body(g_ref, st_ref, re_ref, im_ref, out_ref, same_ref, swap_ref, carry):
        @pl.when(pl.program_id(0) == 0)
        def _():
            carry[...] = jnp.zeros_like(carry)
            same_ref[...] = jnp.zeros_like(same_ref)
            swap_ref[...] = jnp.zeros_like(swap_ref)
        a_same, a_cross = _scan_coeffs(re_ref[...], im_ref[...], conj=True)

        def step(blk, cr):
            adj, acc_same, acc_swap = cr
            for r in range(SCAN_UNROLL):
                t = chunk - 1 - (blk * SCAN_UNROLL + r)
                s = st_ref[t]
                acc_same = acc_same + adj * s
                acc_swap = acc_swap + adj * _swap_parts(s)
                adj = g_ref[t] + a_same * adj + a_cross * _swap_parts(adj)
                out_ref[t] = adj
            return adj, acc_same, acc_swap

        adj, acc_same, acc_swap = lax.fori_loop(0, chunk // SCAN_UNROLL, step, (carry[...], same_ref[...], swap_ref[...]))
        carry[...] = adj
        same_ref[...] = acc_same
        swap_ref[...] = acc_swap

    blk = pl.BlockSpec((chunk, groups, width), lambda i: (last - i, 0, 0))
    par = pl.BlockSpec(bar_re.shape, lambda i: (0, 0, 0))
    acc = pl.BlockSpec((groups, width), lambda i: (0, 0))
    return pl.pallas_call(
        body, name="s5_scan_bwd", grid=(length // chunk,), in_specs=[blk, blk, par, par], out_specs=[blk, acc, acc],
        out_shape=[jax.ShapeDtypeStruct(states.shape, F32), jax.ShapeDtypeStruct((groups, width), F32),
                   jax.ShapeDtypeStruct((groups, width), F32)],
        scratch_shapes=[pltpu.VMEM((groups, width), F32)],
        compiler_params=_params("arbitrary"),
    )(d_states, states, bar_re, bar_im)


def _lam_bar_grad(acc_same, acc_swap):
    def fn(same, swap):
        g_re, g_im = [], []
        for k in range(same.shape[0] // (2 * SUBLANES)):
            re, im = slice(2 * SUBLANES * k, 2 * SUBLANES * k + SUBLANES), slice(2 * SUBLANES * k + SUBLANES, 2 * SUBLANES * (k + 1))
            g_re.append(same[re] + same[im])
            g_im.append(swap[im] - swap[re])
        return jnp.concatenate(g_re, axis=0), jnp.concatenate(g_im, axis=0)
    return _whole(fn, "s5_lam_bar_grad", [acc_same, acc_swap], [(acc_same.shape[0] // 2, LANES)] * 2)


def _block_diag(per_group):
    groups, a, b = per_group.shape
    nc = groups // GROUPS_PER_CHUNK
    eye = jnp.eye(GROUPS_PER_CHUNK, dtype=per_group.dtype)
    x = per_group.reshape(nc, GROUPS_PER_CHUNK, a, 1, b) * eye[None, :, None, :, None]
    return x.reshape(nc, GROUPS_PER_CHUNK * a, GROUPS_PER_CHUNK * b)


def _block_diag_part(chunks, a, b):
    nc = chunks.shape[0]
    x = chunks.reshape(nc, GROUPS_PER_CHUNK, a, GROUPS_PER_CHUNK, b)
    x = jnp.stack([x[:, g, :, g, :] for g in range(GROUPS_PER_CHUNK)], axis=1)
    return x.reshape(nc * GROUPS_PER_CHUNK, a, b)


def _epilogue(raw, gate, scale, g):
    return _rms(raw * scale, g) * (gate * _sigmoid(gate))


def _ssm_mid(y, u, d_skip):
    return _gelu(y + d_skip * u)


def _adamw(w, g, m, v):
    m = ADAM_B1 * m + (1.0 - ADAM_B1) * g
    v = ADAM_B2 * v + (1.0 - ADAM_B2) * (g * g)
    m_hat = m / (1.0 - ADAM_B1 ** ADAM_STEP)
    v_hat = v / (1.0 - ADAM_B2 ** ADAM_STEP)
    return -ADAM_LR * (m_hat / (jnp.sqrt(v_hat) + ADAM_EPS) + ADAM_WD * w), m, v


class _Dims:
    def __init__(self, d_model, length):
        self.d, self.length = d_model, length
        self.d_pool, self.d_attn = d_model // 4, d_model // 2
        self.d_ssm = d_model - self.d_pool - self.d_attn
        self.heads = self.d_attn // HEAD_DIM
        self.groups = self.d_ssm // SSM_GROUP
        self.d_in = 2 * self.d_pool + 4 * self.d_attn + 2 * self.d_ssm
        sizes = (self.d_pool, self.d_pool, self.d_attn, self.d_attn, self.d_attn, self.d_attn, self.d_ssm, self.d_ssm)
        offs = [0]
        for s in sizes[:-1]:
            offs.append(offs[-1] + s)
        (self.o_px, self.o_pgate, self.o_q, self.o_k, self.o_v, self.o_agate, self.o_u, self.o_sgate) = offs


def _ssm_operands(dm, p):
    groups, states = dm.groups, SSM_STATE
    bar_re, bar_im, cf_re, cf_im = _whole(_disc_lam, "s5_disc_lam", [p["lam_re"], p["lam_im"], p["log_dt"].reshape(groups, 1)],
                                          [(groups, states)] * 4)
    b_re2, b_im2 = p["b_re"].reshape(groups * states, SSM_GROUP), p["b_im"].reshape(groups * states, SSM_GROUP)
    bb_re, bb_im = _whole(_disc_b, "s5_disc_b", [cf_re.reshape(-1, 1), cf_im.reshape(-1, 1), b_re2, b_im2],
                          [(groups * states, SSM_GROUP)] * 2)
    per_group = lambda a: jnp.swapaxes(a.reshape(groups, states, SSM_GROUP), 1, 2)
    b_blk = jnp.concatenate([_block_diag(per_group(bb_re)), _block_diag(per_group(bb_im))], axis=2)
    c_blk = jnp.concatenate([_block_diag(jnp.swapaxes(p["c_re"], 1, 2)), _block_diag(jnp.swapaxes(-p["c_im"], 1, 2))], axis=1)
    return dict(bar_re=bar_re, bar_im=bar_im, bar_re_rows=_state_rows(bar_re), bar_im_rows=_state_rows(bar_im),
                cf_re=cf_re, cf_im=cf_im, b_re2=b_re2, b_im2=b_im2, b_blk=b_blk, c_blk=c_blk)


def _mm_carrying(carry, key, *args, **kwargs):
    if key not in carry:
        return _mm(*args, **kwargs)
    make, deliver = carry[key]
    exchange = make()
    out = _mm(*args, comm=exchange, **kwargs)
    deliver(exchange.results)
    return out


def _layer_fwd(dm, x_in, p, gw, carry):
    length = dm.length
    blk = lambda off, w: off // w
    (h,), _ = _rowwise(lambda x, g: ((_rms(x, g),), ()), "rms_fwd", [(x_in, dm.d, 0)], [p["ln_g"]], [(dm.d, BF16)])
    proj = _mm_carrying(carry, "in_proj", h, gw["w_in"], NN, F32, "in_proj", n=dm.d_in, b_quarters="n", tn=WIDE_TILE)
    pooled, mixed = _pool_fwd(proj, gw["w_pool"])
    qb, kb, vb = dm.o_q // HEAD_DIM, dm.o_k // HEAD_DIM, dm.o_v // HEAD_DIM
    attn = _attn_fwd(proj, dm.heads, qb, kb, vb)
    so = _ssm_operands(dm, p)
    u_row = (proj, dm.d_ssm, blk(dm.o_u, dm.d_ssm))
    (u,), _ = _rowwise(lambda v: ((v,), ()), "take_u", [u_row], [], [(dm.d_ssm, F32)])
    bu = _mm3(u, so["b_blk"], NN, "s5_bu", rows3d="o")
    states = _scan_fwd(bu, so["bar_re_rows"], so["bar_im_rows"])
    y = _mm3(states, so["c_blk"], NN, "s5_y", rows3d="a")
    (hg,), _ = _rowwise(lambda yy, uu, dsk: ((_ssm_mid(yy, uu, dsk),), ()), "s5_mid_fwd",
                        [(y, dm.d_ssm, 0), (u, dm.d_ssm, 0)], [p["d_skip"]], [(dm.d_ssm, BF16)])
    z = _mm(hg, gw["w_glu"], NN, F32, "glu_proj", n=2 * dm.d_ssm, b_quarters="n")

    def glu(zz, bias):
        zz = zz + bias
        return (zz[:, :dm.d_ssm] * _sigmoid(zz[:, dm.d_ssm:]),), ()

    (ssm,), _ = _rowwise(glu, "glu_fwd", [(z, 2 * dm.d_ssm, 0)], [p["b_glu"]], [(dm.d_ssm, F32)])
    g_pool, g_attn, g_ssm = (p["branch_g"][:, :dm.d_pool], p["branch_g"][:, dm.d_pool:dm.d_pool + dm.d_attn],
                             p["branch_g"][:, dm.d_pool + dm.d_attn:])
    ones_attn, ones_ssm = jnp.ones((1, dm.d_attn), F32), jnp.ones((1, dm.d_ssm), F32)
    epi = lambda raw, gate, scale, g: ((_epilogue(raw, gate, scale, g),), ())
    branches = [("pool", mixed, dm.d_pool, dm.o_pgate, p["pool_scale"], g_pool),
                ("attn", attn, dm.d_attn, dm.o_agate, ones_attn, g_attn),
                ("ssm", ssm, dm.d_ssm, dm.o_sgate, ones_ssm, g_ssm)]
    ys = []
    for nm, raw, w, off, scale, g in branches:
        (yb,), _ = _rowwise(epi, "epilogue_fwd_" + nm, [(raw, w, 0), (proj, w, blk(off, w))], [scale, g], [(w, BF16)])
        ys.append(yb)
    y_cat = jnp.concatenate(ys, axis=1)
    x_out = _mm_carrying(carry, "out_proj", y_cat, gw["w_out"], NN, F32, "out_proj", add=x_in, tn=WIDE_TILE)
    saved = dict(x_in=x_in, h=h, proj=proj, pooled=pooled, mixed=mixed, attn=attn, so=so, u=u, states=states,
                 y=y, hg=hg, z=z, ssm=ssm, y_cat=y_cat, scales=(p["pool_scale"], ones_attn, ones_ssm), gs=(g_pool, g_attn, g_ssm))
    return x_out, saved


def _layer_bwd(dm, d_out, d_out_bf, p, gw, sv, want_bf, carry):
    length = dm.length
    blk = lambda off, w: off // w
    proj = sv["proj"]
    d_y = _mm_carrying(carry, "out_proj_dgrad", d_out_bf, gw["w_out"], NT, F32, "out_proj_dgrad", tn=WIDE_TILE)
    g_w_out = _mm(sv["y_cat"], d_out_bf, TN, F32, "out_proj_wgrad", out_quarters="rows", tn=WIDE_TILE)

    def epi_bwd(nseg):
        def fn(*vals):
            dys, (raw, gate, scale, g) = vals[:nseg], vals[nseg:]
            dyb = dys[0] if nseg == 1 else jnp.concatenate(dys, axis=1)
            _, vjp = jax.vjp(_epilogue, raw, gate, scale, g)
            d_raw, d_gate, d_scale, d_g = vjp(dyb)
            return (d_raw, d_gate), (d_scale, d_g)
        return fn

    branch = [("pool", sv["mixed"], dm.d_pool, dm.o_pgate, 0), ("attn", sv["attn"], dm.d_attn, dm.o_agate, dm.d_pool),
              ("ssm", sv["ssm"], dm.d_ssm, dm.o_sgate, dm.d_pool + dm.d_attn)]
    d_raws, d_gates, d_scales, d_gs = [], [], [], []
    for (nm, raw, w, off, yoff), scale, g in zip(branch, sv["scales"], sv["gs"]):
        seg = math.gcd(w, yoff) if yoff else w
        dy_rows = [(d_y, seg, yoff // seg + s) for s in range(w // seg)]
        (d_raw, d_gate), (d_scale, d_g) = _rowwise(
            epi_bwd(len(dy_rows)), "epilogue_bwd_" + nm, dy_rows + [(raw, w, 0), (proj, w, blk(off, w))], [scale, g],
            [(w, F32), (w, BF16)], [w, w])
        d_raws.append(d_raw); d_gates.append(d_gate); d_scales.append(d_scale); d_gs.append(d_g)
    d_pooled, g_w_pool = _pool_bwd_mix(d_raws[0], sv["pooled"], gw["w_pool"])
    d_px = _pool_bwd_window(d_pooled, len(POOL_WINDOWS))
    qb, kb, vb = dm.o_q // HEAD_DIM, dm.o_k // HEAD_DIM, dm.o_v // HEAD_DIM
    d_q, d_k, d_v = _attn_bwd(proj, sv["attn"], d_raws[1], dm.heads, qb, kb, vb)
    so = sv["so"]

    def glu_bwd(d_ssm, zz, bias):
        zz = zz + bias
        val, sg = zz[:, :dm.d_ssm], _sigmoid(zz[:, dm.d_ssm:])
        dz = jnp.concatenate([d_ssm * sg, d_ssm * val * sg * (1.0 - sg)], axis=1)
        return (dz,), (jnp.sum(dz, axis=0, keepdims=True),)

    (d_z,), (g_b_glu,) = _rowwise(glu_bwd, "glu_bwd", [(d_raws[2], dm.d_ssm, 0), (sv["z"], 2 * dm.d_ssm, 0)], [p["b_glu"]],
                                  [(2 * dm.d_ssm, BF16)], [2 * dm.d_ssm])
    d_hg = _mm(d_z, gw["w_glu"], NT, F32, "glu_dgrad", n=dm.d_ssm, b_quarters="k")
    g_w_glu = _mm(sv["hg"], d_z, TN, F32, "glu_wgrad", out_quarters="cols")

    def mid_bwd(dh, yy, uu, dsk):
        _, vjp = jax.vjp(_ssm_mid, yy, uu, dsk)
        dy_, du_, ddsk = vjp(dh)
        return (dy_, du_), (ddsk,)

    (d_yssm, d_u_direct), (g_d_skip,) = _rowwise(mid_bwd, "s5_mid_bwd", [(d_hg, dm.d_ssm, 0), (sv["y"], dm.d_ssm, 0), (sv["u"], dm.d_ssm, 0)],
                                                 [p["d_skip"]], [(dm.d_ssm, F32), (dm.d_ssm, F32)], [dm.d_ssm])
    d_states = _mm3(d_yssm, so["c_blk"], NT, "s5_y_dgrad", rows3d="o")
    d_c_blk = _mm3(sv["states"], d_yssm, TN, "s5_y_wgrad", nc=so["c_blk"].shape[0], rows3d="a")
    d_bu, acc_same, acc_swap = _scan_bwd(d_states, sv["states"], so["bar_re_rows"], so["bar_im_rows"])
    d_u_scan = _mm3(d_bu, so["b_blk"], NT, "s5_bu_dgrad", rows3d="a")
    d_b_blk = _mm3(sv["u"], d_bu, TN, "s5_bu_wgrad", nc=so["b_blk"].shape[0], rows3d="b")
    (d_u,) = _elementwise(lambda a, b: (a + b,), "s5_du", [d_u_direct, d_u_scan], [BF16])
    groups, states = dm.groups, SSM_STATE
    part = GROUPS_PER_CHUNK * states
    g_c_re = jnp.swapaxes(_block_diag_part(d_c_blk[:, :part], states, SSM_GROUP), 1, 2)
    g_c_im_neg = jnp.swapaxes(_block_diag_part(d_c_blk[:, part:], states, SSM_GROUP), 1, 2)
    d_bb_re = jnp.swapaxes(_block_diag_part(d_b_blk[:, :, :part], SSM_GROUP, states), 1, 2).reshape(-1, SSM_GROUP)
    d_bb_im = jnp.swapaxes(_block_diag_part(d_b_blk[:, :, part:], SSM_GROUP, states), 1, 2).reshape(-1, SSM_GROUP)

    def disc_b_bwd(cf_re, cf_im, b_re, b_im, g_re, g_im):
        _, vjp = jax.vjp(_disc_b, cf_re, cf_im, b_re, b_im)
        return vjp((g_re, g_im))

    d_cf_re, d_cf_im, g_b_re, g_b_im = _whole(
        disc_b_bwd, "s5_disc_b_bwd", [so["cf_re"].reshape(-1, 1), so["cf_im"].reshape(-1, 1), so["b_re2"], so["b_im2"], d_bb_re, d_bb_im],
        [(groups * states, 1)] * 2 + [(groups * states, SSM_GROUP)] * 2)

    def disc_lam_bwd(lam_re, lam_im, log_dt, g_bar_re, g_bar_im, g_cf_re, g_cf_im):
        _, vjp = jax.vjp(_disc_lam, lam_re, lam_im, log_dt)
        return vjp((g_bar_re, g_bar_im, g_cf_re, g_cf_im))

    g_bar_re, g_bar_im = _lam_bar_grad(acc_same, acc_swap)

    g_lam_re, g_lam_im, g_log_dt = _whole(
        disc_lam_bwd, "s5_disc_lam_bwd", [p["lam_re"], p["lam_im"], p["log_dt"].reshape(groups, 1), g_bar_re.reshape(groups, states), g_bar_im.reshape(groups, states),
                                          d_cf_re.reshape(groups, states), d_cf_im.reshape(groups, states)],
        [(groups, states)] * 2 + [(groups, 1)])
    (g_c_im,) = _elementwise(lambda a: (-a,), "s5_neg_c_im", [g_c_im_neg.reshape(groups * SSM_GROUP, states)], [F32])
    d_proj = jnp.concatenate([d_px, d_gates[0], d_q, d_k.astype(BF16), d_v.astype(BF16), d_gates[1], d_u, d_gates[2]], axis=1)
    d_h = _mm_carrying(carry, "in_proj_dgrad", d_proj, gw["w_in"], NT, F32, "in_proj_dgrad", n=dm.d, b_quarters="k", tm=WIDE_TILE)
    g_w_in = _mm_carrying(carry, "in_proj_wgrad", sv["h"], d_proj, TN, F32, "in_proj_wgrad", out_quarters="cols", tn=WIDE_TILE)

    def rms_bwd(dh, xx, dres, g):
        _, vjp = jax.vjp(_rms, xx, g)
        dx, dg = vjp(dh)
        dx = dx + dres
        return ((dx, dx) if want_bf else (dx,)), (dg,)

    d_xs, (g_ln_g,) = _rowwise(rms_bwd, "rms_bwd", [(d_h, dm.d, 0), (sv["x_in"], dm.d, 0), (d_out, dm.d, 0)], [p["ln_g"]],
                               [(dm.d, F32), (dm.d, BF16)] if want_bf else [(dm.d, F32)], [dm.d])
    ngr, ch = gw["w_pool"].shape[0], gw["w_pool"].shape[1]
    q_rows = ch // 4
    g_w_pool2 = g_w_pool.reshape(ngr, 4, 2, q_rows // 2, ch).transpose(2, 1, 0, 3, 4).reshape(2, 4, ngr * q_rows // 2, ch)
    big = dict(w_in=g_w_in, w_out=g_w_out, w_glu=g_w_glu, w_pool=g_w_pool2)
    small = dict(ln_g=g_ln_g, pool_scale=d_scales[0], lam_re=g_lam_re, lam_im=g_lam_im, log_dt=g_log_dt.reshape(1, groups),
                 b_re=g_b_re, b_im=g_b_im, c_re=g_c_re, c_im=g_c_im, d_skip=g_d_skip, b_glu=g_b_glu,
                 branch_g=jnp.concatenate(d_gs, axis=1))
    return d_xs[0], (d_xs[1] if want_bf else None), big, small


SMALL_ROWS = 8


def _pack(arrays):
    parts = []
    for a in arrays:
        flat = a.reshape(-1)
        pad = (-flat.shape[0]) % (SMALL_ROWS * LANES)
        parts.append(jnp.pad(flat, (0, pad)).reshape(-1, LANES))
    return jnp.concatenate(parts, axis=0)


def _unpack(buf, like):
    res, row = [], 0
    for a in like:
        size = math.prod(a.shape)
        rows = -(-size // (SMALL_ROWS * LANES)) * SMALL_ROWS
        res.append(buf[row:row + rows].reshape(-1)[:size].reshape(a.shape))
        row += rows
    return res


def kernel(x, ln_g, w_in, w_pool, pool_scale, lam_re, lam_im, log_dt, b_re, b_im, c_re, c_im, d_skip, w_glu, b_glu, branch_g, w_out, final_g, loss_target, m_ln_g, m_w_in, m_w_pool, m_pool_scale, m_lam_re, m_lam_im, m_log_dt, m_b_re, m_b_im, m_c_re, m_c_im, m_d_skip, m_w_glu, m_b_glu, m_branch_g, m_w_out, m_final_g, v_ln_g, v_w_in, v_w_pool, v_pool_scale, v_lam_re, v_lam_im, v_log_dt, v_b_re, v_b_im, v_c_re, v_c_im, v_d_skip, v_w_glu, v_b_glu, v_branch_g, v_w_out, v_final_g):
    given = dict(locals())
    weights = {n: given[n] for n in WEIGHTS}
    depth = ln_g.shape[0]
    _, length, d_model = x.shape
    dm = _Dims(d_model, length)
    x0, target = x[0], loss_target[0]
    c_idx = lax.axis_index("c")
    my_quarter = 2 * lax.axis_index("x") + lax.axis_index("y")

    shard2d = {(n, l): weights[n][l].reshape(-1, weights[n].shape[-1]) for l in range(depth) for n in SHARDED}
    keys = list(shard2d)
    halves16 = {k: w.astype(BF16).reshape(2, w.shape[0] // 2, -1) for k, w in shard2d.items()}
    gw = [dict() for _ in range(depth)]

    def gather(group, name):
        return _allgather_quarters([halves16[k] for k in group], name)

    def deliver_weights(group):
        def deliver(results):
            for (n, l), g in zip(group, results):
                rows, cols = shard2d[(n, l)].shape
                g = g.reshape(4, rows, cols)
                if n == "w_out":
                    g = g.reshape(4 * rows, cols)
                if n == "w_pool":
                    ngr = w_pool.shape[1]
                    g = g.reshape(4, ngr, rows // ngr, cols).transpose(1, 0, 2, 3).reshape(ngr, 4 * rows // ngr, cols)
                gw[l][n] = g
        return deliver

    first = [("w_in", 0), ("w_pool", 0), ("w_out", 0)]
    behind_in_proj = [("w_glu", 0)] + ([("w_in", 1)] if depth > 1 else [])
    behind_out_proj = [k for k in keys if k not in first + behind_in_proj]
    deliver_weights(first)(gather(first, "allgather_first").run())
    fwd_carry = [dict() for _ in range(depth)]
    fwd_carry[0]["in_proj"] = (lambda: gather(behind_in_proj, "allgather_behind_in_proj"), deliver_weights(behind_in_proj))
    if behind_out_proj:
        fwd_carry[0]["out_proj"] = (lambda: gather(behind_out_proj, "allgather_behind_out_proj"), deliver_weights(behind_out_proj))
    small_names = [n for n in WEIGHTS if n not in SHARDED and n != "final_g"]
    ps = [{n: (weights[n][l].reshape(1, -1) if weights[n][l].ndim == 1 else weights[n][l]) for n in small_names} for l in range(depth)]

    acts, saved = x0, []
    for l in range(depth):
        acts, sv = _layer_fwd(dm, acts, ps[l], gw[l], fwd_carry[l])
        saved.append(sv)

    def final(xx, tt, g):
        def loss_fn(xv, gv):
            err = _rms(xv, gv) - tt
            return 0.5 * jnp.sum(jnp.mean(err * err, axis=-1))
        val, (dx, dg) = jax.value_and_grad(loss_fn, argnums=(0, 1))(xx, g)
        return (dx, dx), (val.reshape(1, 1), dg)

    (d_act, d_act_bf), (loss_part, g_final_g) = _rowwise(
        final, "final_norm_loss", [(acts, dm.d, 0), (target, dm.d, 0)], [final_g.reshape(1, -1)], [(dm.d, F32), (dm.d, BF16)], [1, dm.d])
    loss = lax.psum(loss_part[0, 0], AXES)

    place = jnp.stack([c_idx, my_quarter]).astype(jnp.int32)
    big, small, mine, theirs = [None] * depth, [None] * depth, {}, {}

    def reduction(l):
        group = [(n, l) for n in SHARDED]
        state = {}

        def sums(from_sibling):
            state["chip"] = [_chip_sum(big[l][n], r, place) for (n, _), r in zip(group, from_sibling)]

        def totals(from_chips):
            state["mine"] = [_owner_sum(chip32, r) for (_, chip32), r in zip(state["chip"], from_chips)]

        def done(from_sibling):
            mine.update(zip(group, state["mine"]))
            theirs.update(zip(group, from_sibling))

        tag = "_%d" % l
        return [(lambda: _to_sibling([big[l][n] for n, _ in group], "grads_to_sibling" + tag, other_half=True), sums),
                (lambda: _to_owner_chips([chip16 for chip16, _ in state["chip"]], "grads_to_owner_chips" + tag), totals),
                (lambda: _to_sibling(state["mine"], "reduced_half_to_sibling" + tag), done)]

    bwd_carry = {}
    for l in reversed(range(depth)):
        d_act, d_act_bf, big[l], small[l] = _layer_bwd(dm, d_act, d_act_bf, ps[l], gw[l], saved[l], want_bf=l > 0, carry=bwd_carry)
        stages = reduction(l)
        if l > 0:
            bwd_carry = dict(zip(["out_proj_dgrad", "in_proj_dgrad", "in_proj_wgrad"], stages))
        else:
            for make, deliver in stages:
                deliver(make().run())
    grad_x = d_act[None]

    small_all = [n for n in WEIGHTS if n not in SHARDED]
    packed = _pack([small[l][n] for l in range(depth) for n in small_names] + [g_final_g])
    summed = _unpack(_allreduce_small(packed), [weights[n][l] for l in range(depth) for n in small_names] + [final_g])
    g_small = {n: jnp.stack([summed[l * len(small_names) + i] for l in range(depth)]) for i, n in enumerate(small_names)}
    g_small["final_g"] = summed[-1]

    out_g, out_d, out_m, out_v = {}, {}, {}, {}
    for n in SHARDED:
        shape = weights[n].shape
        if n == "w_pool":
            ngr = shape[1]
            both = jnp.stack([jnp.where(c_idx == 0, jnp.stack([mine[(n, l)], theirs[(n, l)]]), jnp.stack([theirs[(n, l)], mine[(n, l)]]))
                              for l in range(depth)])
            g = both.reshape(depth, 2, ngr, -1, shape[-1]).transpose(0, 2, 1, 3, 4).reshape(shape)
            res = [g] + _elementwise(lambda *a: _adamw(*a), "adamw_" + n, [weights[n], g, given["m_" + n], given["v_" + n]], [F32] * 3)
        else:
            halves = lambda a: a.reshape(depth, 2, -1, shape[-1])
            res = _adamw_halves(halves(weights[n]), halves(given["m_" + n]), halves(given["v_" + n]),
                                [mine[(n, l)] for l in range(depth)], [theirs[(n, l)] for l in range(depth)], place, "adamw_" + n)
        out_g[n], out_d[n], out_m[n], out_v[n] = [r.reshape(shape) for r in res]
    d, m, v = _whole(_adamw, "adamw_small", [_pack([weights[n] for n in small_all]), _pack([g_small[n] for n in small_all]),
                                            _pack([given["m_" + n] for n in small_all]), _pack([given["v_" + n] for n in small_all])],
                     [_pack([weights[n] for n in small_all]).shape] * 3)
    like = [weights[n] for n in small_all]
    for n, dd, mm, vv in zip(small_all, _unpack(d, like), _unpack(m, like), _unpack(v, like)):
        out_g[n], out_d[n], out_m[n], out_v[n] = g_small[n], dd, mm, vv
    return (loss, grad_x, *[out_g[n] for n in WEIGHTS], *[out_d[n] for n in WEIGHTS],
            *[out_m[n] for n in WEIGHTS], *[out_v[n] for n in WEIGHTS])
```

```python
import functools
import math

import jax
import jax.numpy as jnp
from jax import lax
from jax.experimental import pallas as pl
from jax.experimental.pallas import tpu as pltpu

F32 = jnp.float32
BF16 = jnp.bfloat16
EPS = 1e-6
POOL_WINDOWS = (2, 4, 8, 16)
POOL_HALO = 16
HEAD_DIM = 128
SSM_GROUP = 16
SSM_STATE = 64
GROUPS_PER_CHUNK = 16
LANES = 128
VMEM_LIMIT_BYTES = 56 * 1024 * 1024
ROW_TILE_ELEMS = 2 * 1024 * 1024
WIDE_TILE = 1024
ADAM_LR, ADAM_B1, ADAM_B2, ADAM_EPS, ADAM_WD, ADAM_STEP = 0.001, 0.9, 0.999, 1e-08, 0.01, 10
MESH = pl.DeviceIdType.MESH
AXES = ("x", "y", "c")
WEIGHTS = ("ln_g", "w_in", "w_pool", "pool_scale", "lam_re", "lam_im", "log_dt", "b_re", "b_im",
           "c_re", "c_im", "d_skip", "w_glu", "b_glu", "branch_g", "w_out", "final_g")
SHARDED = ("w_in", "w_pool", "w_glu", "w_out")


def _params(*sem):
    return pltpu.CompilerParams(dimension_semantics=sem or None, vmem_limit_bytes=VMEM_LIMIT_BYTES)


def _dot(a, b, dims=((1,), (0,))):
    return lax.dot_general(a, b, (dims, ((), ())), preferred_element_type=F32)


NN, NT, TN = ((1,), (0,)), ((1,), (1,)), ((0,), (0,))


def _split(x):
    hi = x.astype(BF16)
    return hi, (x - hi.astype(F32)).astype(BF16)


def _dot_rounded(a, b, dims):
    return _dot(a.astype(BF16), b.astype(BF16), dims)


def _sigmoid(x):
    return 1.0 / (1.0 + jnp.exp(-x))


def _gelu(x):
    return 0.5 * x * (1.0 + jnp.tanh(0.7978845608028654 * (x + 0.044715 * x * x * x)))


def _rms(x, g):
    return x * lax.rsqrt(jnp.mean(x * x, axis=-1, keepdims=True) + EPS) * g


HBM_SPEC = pl.BlockSpec(memory_space=pltpu.HBM)


def _place():
    x, y, c = lax.axis_index("x"), lax.axis_index("y"), lax.axis_index("c")
    chips = [(1 - x, y), (x, 1 - y), (1 - x, 1 - y)]
    return x, y, c, chips


def _remote(src, dst, send_sem, recv_sem, target):
    return pltpu.make_async_remote_copy(src_ref=src, dst_ref=dst, send_sem=send_sem, recv_sem=recv_sem,
                                        device_id=target, device_id_type=MESH)


class _Exchange:
    def __init__(self, name, inputs, out_shapes, sems, start, finish):
        self.name, self.inputs, self.out_shapes, self.sems = name, list(inputs), list(out_shapes), sems
        self.start, self.finish, self.results = start, finish, None

    def run(self):
        n_in, n_out = len(self.inputs), len(self.out_shapes)

        def body(*refs):
            parts = (refs[:n_in], refs[n_in:n_in + n_out], refs[-2], refs[-1])
            self.start(*parts)
            self.finish(*parts)

        self.results = pl.pallas_call(
            body, name=self.name, in_specs=[HBM_SPEC] * n_in, out_specs=[HBM_SPEC] * n_out, out_shape=self.out_shapes,
            scratch_shapes=[pltpu.SemaphoreType.DMA(self.sems), pltpu.SemaphoreType.DMA(self.sems)],
            compiler_params=pltpu.CompilerParams(has_side_effects=True),
        )(*self.inputs)
        return self.results


def _allgather_quarters(shards, name):
    n, own = len(shards), 6

    def start(ins, outs, send_sems, recv_sems):
        x, y, c, chips = _place()
        mine = 2 * x + y
        for t in range(n):
            for k, (cx, cy) in enumerate(chips):
                _remote(ins[t].at[c], outs[t].at[mine, c], send_sems.at[t, k], recv_sems.at[t, k], (cx, cy, c)).start()
            _remote(ins[t], outs[t].at[mine], send_sems.at[t, own], recv_sems.at[t, own], (x, y, 1 - c)).start()

    def finish(ins, outs, send_sems, recv_sems):
        x, y, c, chips = _place()
        mine, sibling = 2 * x + y, (x, y, 1 - c)
        for t in range(n):
            for k, (cx, cy) in enumerate(chips):
                landed = outs[t].at[2 * cx + cy, c]
                _remote(landed, landed, send_sems.at[t, k], recv_sems.at[t, k], sibling).wait_recv()
                _remote(landed, landed, send_sems.at[t, 3 + k], recv_sems.at[t, 3 + k], sibling).start()
        for t in range(n):
            got = outs[t].at[mine]
            _remote(got, got, send_sems.at[t, own], recv_sems.at[t, own], sibling).wait_recv()
            _remote(ins[t], got, send_sems.at[t, own], recv_sems.at[t, own], sibling).wait_send()
            for k, (cx, cy) in enumerate(chips):
                got = outs[t].at[2 * cx + cy, 1 - c]
                _remote(got, got, send_sems.at[t, 3 + k], recv_sems.at[t, 3 + k], sibling).wait_recv()
                sent = outs[t].at[2 * cx + cy, c]
                _remote(sent, sent, send_sems.at[t, 3 + k], recv_sems.at[t, 3 + k], sibling).wait_send()
                _remote(ins[t].at[c], sent, send_sems.at[t, k], recv_sems.at[t, k], sibling).wait_send()

    return _Exchange(name, shards, [jax.ShapeDtypeStruct((4,) + s.shape, s.dtype) for s in shards], (n, 7), start, finish)


def _to_sibling(arrays, name, other_half=False):
    n = len(arrays)

    def copies(ins, outs, send_sems, recv_sems):
        x, y, c, _ = _place()
        return [_remote(ins[t].at[1 - c] if other_half else ins[t], outs[t], send_sems.at[t], recv_sems.at[t], (x, y, 1 - c))
                for t in range(n)]

    def start(*refs):
        for cp in copies(*refs):
            cp.start()

    def finish(*refs):
        for cp in copies(*refs):
            cp.wait()

    shapes = [jax.ShapeDtypeStruct(a.shape[1:] if other_half else a.shape, a.dtype) for a in arrays]
    return _Exchange(name, arrays, shapes, (n,), start, finish)


def _to_owner_chips(arrays, name):
    n = len(arrays)

    def copies(ins, outs, send_sems, recv_sems):
        x, y, c, chips = _place()
        return [_remote(ins[t].at[2 * cx + cy], outs[t].at[k], send_sems.at[t, k], recv_sems.at[t, k], (cx, cy, c))
                for t in range(n) for k, (cx, cy) in enumerate(chips)]

    def start(*refs):
        for cp in copies(*refs):
            cp.start()

    def finish(*refs):
        for cp in copies(*refs):
            cp.wait()

    shapes = [jax.ShapeDtypeStruct((3,) + a.shape[1:], a.dtype) for a in arrays]
    return _Exchange(name, arrays, shapes, (n, 3), start, finish)


def _fit(tile, dim):
    tile = min(tile, dim)
    step = LANES if tile >= LANES else 8
    tile -= tile % step
    while dim % tile:
        tile -= step
    return tile


def _mm(a, b, dims, out_dtype, name, *, n=None, b_quarters=None, out_quarters=None, add=None, comm=None, tm=512, tn=512, tk=4096):
    if dims == TN:
        k_dim, m_dim = a.shape
    else:
        m_dim, k_dim = a.shape
    if n is None:
        n = b.shape[0] if dims == NT else b.shape[1]
    m_unit = {None: m_dim, "cols": m_dim // 2, "rows": m_dim // 8}[out_quarters]
    n_unit = n // 4 if (b_quarters == "n" or out_quarters == "cols") else n
    k_unit = k_dim // 4 if b_quarters == "k" else k_dim
    tm, tn, tk = _fit(tm, m_unit), _fit(tn, n_unit), _fit(tk, k_unit)
    gm, gn, gk = m_dim // tm, n // tn, k_dim // tk
    mb, nb, kb = m_unit // tm, n_unit // tn, k_unit // tk
    if dims == TN:
        a_spec = pl.BlockSpec((tk, tm), lambda i, j, k: (k, i))
    else:
        a_spec = pl.BlockSpec((tm, tk), lambda i, j, k: (i, k))
    if b_quarters == "n":
        bspec = pl.BlockSpec((None, tk, tn), lambda i, j, k: (j // nb, k, j % nb))
    elif b_quarters == "k":
        bspec = pl.BlockSpec((None, tn, tk), lambda i, j, k: (k // kb, j, k % kb))
    elif dims == NT:
        bspec = pl.BlockSpec((tn, tk), lambda i, j, k: (j, k))
    else:
        bspec = pl.BlockSpec((tk, tn), lambda i, j, k: (k, j))
    if out_quarters == "cols":
        out_shape = (2, 4, m_unit, n_unit)
        out_spec = pl.BlockSpec((None, None, tm, tn), lambda i, j, k: (i // mb, j // nb, i % mb, j % nb))
    elif out_quarters == "rows":
        out_shape = (2, 4, m_unit, n)
        out_spec = pl.BlockSpec((None, None, tm, tn), lambda i, j, k: ((i // mb) % 2, i // (2 * mb), i % mb, j))
    else:
        out_shape, out_spec = (m_dim, n), pl.BlockSpec((tm, tn), lambda i, j, k: (i, j))
    in_specs, operands = [a_spec, bspec], [a, b]
    if add is not None:
        in_specs.append(pl.BlockSpec((tm, tn), lambda i, j, k: (i, j)))
        operands.append(add)

    n_in = len(operands)
    comms = list(comm) if comm else []
    n_cin, n_cout = sum(len(e.inputs) for e in comms), sum(len(e.out_shapes) for e in comms)
    n_acc = 1 if gk > 1 else 0

    def body(*refs):
        a_ref, b_ref = refs[0], refs[1]
        add_ref = refs[2] if add is not None else None
        o_ref = refs[n_in + n_cin]
        if comms:
            c_refs, i_at, o_at, s_at = [], n_in, n_in + n_cin + 1, n_in + n_cin + 1 + n_cout + n_acc
            for e in comms:
                c_refs.append((refs[i_at:i_at + len(e.inputs)], refs[o_at:o_at + len(e.out_shapes)], refs[s_at], refs[s_at + 1]))
                i_at, o_at, s_at = i_at + len(e.inputs), o_at + len(e.out_shapes), s_at + 2
            step = (pl.program_id(0) * gn + pl.program_id(1)) * gk + pl.program_id(2)

            @pl.when(step == 0)
            def _():
                for e, r in zip(comms, c_refs):
                    e.start(*r)

        def finish(r):
            if add_ref is not None:
                r = r + add_ref[...]
            o_ref[...] = r.astype(o_ref.dtype)

        if gk == 1:
            finish(_dot(a_ref[...], b_ref[...], dims))
        else:
            acc = refs[n_in + n_cin + 1 + n_cout]
            k = pl.program_id(2)

            @pl.when(k == 0)
            def _():
                acc[...] = jnp.zeros_like(acc)

            acc[...] += _dot(a_ref[...], b_ref[...], dims)

            @pl.when(k == gk - 1)
            def _():
                finish(acc[...])

        if comms:
            @pl.when(step == gm * gn * gk - 1)
            def _():
                for e, r in zip(comms, c_refs):
                    e.finish(*r)

    scratch = [pltpu.VMEM((tm, tn), F32)] if gk > 1 else []
    if not comms:
        return pl.pallas_call(
            body, name=name, grid=(gm, gn, gk), in_specs=in_specs, out_specs=out_spec,
            out_shape=jax.ShapeDtypeStruct(out_shape, out_dtype), scratch_shapes=scratch,
            compiler_params=_params("parallel", "parallel", "arbitrary"),
        )(*operands)
    for e in comms:
        scratch += [pltpu.SemaphoreType.DMA(e.sems), pltpu.SemaphoreType.DMA(e.sems)]
    res = pl.pallas_call(
        body, name=name, grid=(gm, gn, gk), in_specs=in_specs + [HBM_SPEC] * n_cin, out_specs=[out_spec] + [HBM_SPEC] * n_cout,
        out_shape=[jax.ShapeDtypeStruct(out_shape, out_dtype)] + [s for e in comms for s in e.out_shapes],
        scratch_shapes=scratch,
        compiler_params=pltpu.CompilerParams(dimension_semantics=("arbitrary",) * 3, vmem_limit_bytes=VMEM_LIMIT_BYTES,
                                             has_side_effects=True),
    )(*operands, *[x for e in comms for x in e.inputs])
    at = 1
    for e in comms:
        e.results = list(res[at:at + len(e.out_shapes)])
        at += len(e.out_shapes)
    return res[0]


def _mm3(a, b, dims, name, nc=None, tm=512, rows3d=()):
    rows = a.shape[0]
    tm = min(tm, rows)
    gm = rows // tm

    def row_spec(width, three_d):
        if three_d:
            return pl.BlockSpec((tm, width // LANES, LANES), lambda c, i: (i, c, 0))
        return pl.BlockSpec((tm, width), lambda c, i: (i, c))

    def load(ref, three_d):
        if not three_d:
            return ref[...]
        return jnp.concatenate([ref[:, r, :] for r in range(ref.shape[1])], axis=1)

    if dims == TN:
        ka = a.shape[1] * (a.shape[2] if "a" in rows3d else 1) // nc
        nb = b.shape[1] * (b.shape[2] if "b" in rows3d else 1) // nc

        def body(a_ref, b_ref, o_ref):
            @pl.when(pl.program_id(1) == 0)
            def _():
                o_ref[...] = jnp.zeros_like(o_ref)
            o_ref[...] += _dot_rounded(load(a_ref, "a" in rows3d), load(b_ref, "b" in rows3d), TN)

        return pl.pallas_call(
            body, name=name, grid=(nc, gm),
            in_specs=[row_spec(ka, "a" in rows3d), row_spec(nb, "b" in rows3d)],
            out_specs=pl.BlockSpec((None, ka, nb), lambda c, i: (c, 0, 0)),
            out_shape=jax.ShapeDtypeStruct((nc, ka, nb), F32),
            compiler_params=_params("parallel", "arbitrary"),
        )(a, b)
    nc, ka, nb = b.shape
    wa, wo = (ka, nb) if dims == NN else (nb, ka)

    def body(a_ref, b_ref, o_ref):
        res = _dot_rounded(load(a_ref, "a" in rows3d), b_ref[...], dims)
        if "o" in rows3d:
            for r in range(wo // LANES):
                o_ref[:, r, :] = res[:, r * LANES:(r + 1) * LANES]
        else:
            o_ref[...] = res

    out_shape = (rows, nc * wo // LANES, LANES) if "o" in rows3d else (rows, nc * wo)
    return pl.pallas_call(
        body, name=name, grid=(nc, gm),
        in_specs=[row_spec(wa, "a" in rows3d), pl.BlockSpec((None, ka, nb), lambda c, i: (c, 0, 0))],
        out_specs=row_spec(wo, "o" in rows3d),
        out_shape=jax.ShapeDtypeStruct(out_shape, F32),
        compiler_params=_params("parallel", "parallel"),
    )(a, b)


def _rowwise(fn, name, rows, vecs=(), outs=(), sums=()):
    length = rows[0][0].shape[0]
    total = sum(w for _, w, _ in rows) + sum(w for w, _ in outs)
    tile = 8
    while tile * 2 <= min(length, 512) and tile * 2 * total <= ROW_TILE_ELEMS:
        tile *= 2
    assert length % tile == 0
    n_r, n_v, n_o = len(rows), len(vecs), len(outs)

    def body(*refs):
        vals = [r[...] for r in refs[:n_r + n_v]]
        o_refs = refs[n_r + n_v:n_r + n_v + n_o]
        s_refs = refs[n_r + n_v + n_o:]
        res_o, res_s = fn(*vals)
        for ref, val in zip(o_refs, res_o):
            ref[...] = val.astype(ref.dtype)
        if s_refs:
            @pl.when(pl.program_id(0) == 0)
            def _():
                for ref in s_refs:
                    ref[...] = jnp.zeros_like(ref)
            for ref, val in zip(s_refs, res_s):
                ref[...] += val

    def row_spec(w, cb):
        return pl.BlockSpec((tile, w), lambda i: (i, cb))

    res = pl.pallas_call(
        body, name=name, grid=(length // tile,),
        in_specs=[row_spec(w, cb) for _, w, cb in rows] + [pl.BlockSpec(v.shape, lambda i: (0, 0)) for v in vecs],
        out_specs=[row_spec(w, 0) for w, _ in outs] + [pl.BlockSpec((1, w), lambda i: (0, 0)) for w in sums],
        out_shape=[jax.ShapeDtypeStruct((length, w), dt) for w, dt in outs]
        + [jax.ShapeDtypeStruct((1, w), F32) for w in sums],
        compiler_params=_params("arbitrary" if sums else "parallel"),
    )(*[a for a, _, _ in rows], *vecs)
    return res[:n_o], res[n_o:]


def _elementwise(fn, name, arrays, out_dtypes):
    shape = arrays[0].shape
    cols = shape[-1]
    flat = [a.reshape(-1, cols) for a in arrays]
    rows = flat[0].shape[0]
    tile = 8
    while tile * 2 <= rows and rows % (tile * 2) == 0 and tile * 2 * cols * (len(arrays) + len(out_dtypes)) <= ROW_TILE_ELEMS:
        tile *= 2
    assert rows % tile == 0
    n_in = len(flat)

    def body(*refs):
        res = fn(*[r[...] for r in refs[:n_in]])
        for ref, val in zip(refs[n_in:], res):
            ref[...] = val.astype(ref.dtype)

    spec = pl.BlockSpec((tile, cols), lambda i: (i, 0))
    res = pl.pallas_call(
        body, name=name, grid=(rows // tile,), in_specs=[spec] * n_in, out_specs=[spec] * len(out_dtypes),
        out_shape=[jax.ShapeDtypeStruct((rows, cols), dt) for dt in out_dtypes],
        compiler_params=_params("parallel"),
    )(*flat)
    return [r.reshape(shape) for r in res]


def _whole(fn, name, arrays, out_shapes):
    n_in = len(arrays)

    def body(*refs):
        res = fn(*[r[...] for r in refs[:n_in]])
        for ref, val in zip(refs[n_in:], res):
            ref[...] = val

    return pl.pallas_call(
        body, name=name, out_shape=[jax.ShapeDtypeStruct(s, F32) for s in out_shapes],
        compiler_params=_params(),
    )(*arrays)


def _grad_tile(rows, cols, arrays_per_step):
    tile = 8
    while tile * 2 <= rows and rows % (tile * 2) == 0 and tile * 2 * cols * arrays_per_step <= ROW_TILE_ELEMS:
        tile *= 2
    return tile


def _chip_sum(partial, from_sibling, place):
    _, _, rows, cols = partial.shape
    tile = _grad_tile(rows, cols, 4)

    def body(place_ref, a_ref, b_ref, o16_ref, o32_ref):
        total = a_ref[...] + b_ref[...]
        o16_ref[...] = total.astype(BF16)

        @pl.when(pl.program_id(1) == place_ref[1])
        def _():
            o32_ref[...] = total

    return pl.pallas_call(
        body, name="chip_sum",
        grid_spec=pltpu.PrefetchScalarGridSpec(
            num_scalar_prefetch=1, grid=(rows // tile, 4),
            in_specs=[pl.BlockSpec((None, None, tile, cols), lambda i, q, p: (p[0], q, i, 0)),
                      pl.BlockSpec((None, tile, cols), lambda i, q, p: (q, i, 0))],
            out_specs=[pl.BlockSpec((None, tile, cols), lambda i, q, p: (q, i, 0)),
                       pl.BlockSpec((tile, cols), lambda i, q, p: (i, 0))]),
        out_shape=[jax.ShapeDtypeStruct((4, rows, cols), BF16), jax.ShapeDtypeStruct((rows, cols), F32)],
        compiler_params=_params("parallel", "arbitrary"),
    )(place, partial, from_sibling)


def _owner_sum(own, from_chips):
    rows, cols = own.shape
    tile = _grad_tile(rows, cols, 4)

    def body(a_ref, r0_ref, r1_ref, r2_ref, o_ref):
        o_ref[...] = a_ref[...] + r0_ref[...].astype(F32) + r1_ref[...].astype(F32) + r2_ref[...].astype(F32)

    spec = pl.BlockSpec((tile, cols), lambda i: (i, 0))
    return pl.pallas_call(
        body, name="owner_sum", grid=(rows // tile,),
        in_specs=[spec] + [pl.BlockSpec((None, tile, cols), functools.partial(lambda i, k: (k, i, 0), k=k)) for k in range(3)],
        out_specs=spec, out_shape=jax.ShapeDtypeStruct((rows, cols), F32), compiler_params=_params("parallel"),
    )(own, from_chips, from_chips, from_chips)


def _adamw_halves(w, m, v, mine, theirs, place, name):
    depth, _, rows, cols = w.shape
    tile = _grad_tile(rows, cols, 9)

    def body(place_ref, w_ref, m_ref, v_ref, *refs):
        g_refs, outs = refs[:2 * depth], refs[2 * depth:]
        layer, half = pl.program_id(0), pl.program_id(1)
        g = None
        for d in range(depth):
            gd = jnp.where(half == place_ref[0], g_refs[2 * d][...], g_refs[2 * d + 1][...])
            g = gd if g is None else jnp.where(layer == d, gd, g)
        delta, m_new, v_new = _adamw(w_ref[...], g, m_ref[...], v_ref[...])
        for ref, val in zip(outs, (g, delta, m_new, v_new)):
            ref[...] = val

    full = pl.BlockSpec((None, None, tile, cols), lambda l, h, i, p: (l, h, i, 0))
    g_specs = []
    for d in range(depth):
        g_specs += [pl.BlockSpec((tile, cols), functools.partial(lambda l, h, i, p, d: (jnp.where(l == d, i, 0), 0), d=d))] * 2
    operands = [x for pair in zip(mine, theirs) for x in pair]
    return pl.pallas_call(
        body, name=name,
        grid_spec=pltpu.PrefetchScalarGridSpec(
            num_scalar_prefetch=1, grid=(depth, 2, rows // tile),
            in_specs=[full] * 3 + g_specs, out_specs=[full] * 4),
        out_shape=[jax.ShapeDtypeStruct(w.shape, F32)] * 4,
        compiler_params=_params("arbitrary", "arbitrary", "arbitrary"),
    )(place, w, m, v, *operands)


def _allreduce_small(flat):
    rows = flat.shape[0] + (-flat.shape[0]) % (2 * SUBLANES)
    half = rows // 2
    padded = jnp.pad(flat, ((0, rows - flat.shape[0]), (0, 0)))
    swap, final = 0, 4

    def body(in_ref, out_ref, sibling_ref, chips_ref, total_ref, send_sems, recv_sems):
        x, y, c, chips = _place()
        mine, sibling = 2 * x + y, (x, y, 1 - c)
        my_rows = pl.ds(pl.multiple_of(c * half, SUBLANES), half)
        their_rows = pl.ds(pl.multiple_of((1 - c) * half, SUBLANES), half)
        cp = _remote(in_ref, sibling_ref, send_sems.at[swap], recv_sems.at[swap], sibling)
        cp.start()
        cp.wait()
        chips_ref[mine] = in_ref[my_rows, :] + sibling_ref[my_rows, :]
        sends = [_remote(chips_ref.at[mine], chips_ref.at[mine], send_sems.at[1 + k], recv_sems.at[1 + k], (cx, cy, c))
                 for k, (cx, cy) in enumerate(chips)]
        for cp in sends:
            cp.start()
        for k, (cx, cy) in enumerate(chips):
            slot = chips_ref.at[2 * cx + cy]
            _remote(slot, slot, send_sems.at[1 + k], recv_sems.at[1 + k], sibling).wait_recv()
        for cp in sends:
            cp.wait_send()
        total_ref[my_rows, :] = (chips_ref[0] + chips_ref[1]) + (chips_ref[2] + chips_ref[3])
        cp = _remote(total_ref.at[my_rows, :], total_ref.at[my_rows, :], send_sems.at[final], recv_sems.at[final], sibling)
        cp.start()
        _remote(total_ref.at[their_rows, :], total_ref.at[their_rows, :], send_sems.at[final], recv_sems.at[final], sibling).wait_recv()
        cp.wait_send()
        out_ref[...] = total_ref[...]

    out = pl.pallas_call(
        body, name="allreduce_small_grads",
        in_specs=[pl.BlockSpec(memory_space=pltpu.VMEM)], out_specs=pl.BlockSpec(memory_space=pltpu.VMEM),
        out_shape=jax.ShapeDtypeStruct((rows, LANES), F32),
        scratch_shapes=[pltpu.VMEM((rows, LANES), F32), pltpu.VMEM((4, half, LANES), F32), pltpu.VMEM((rows, LANES), F32),
                        pltpu.SemaphoreType.DMA((5,)), pltpu.SemaphoreType.DMA((5,))],
        compiler_params=pltpu.CompilerParams(has_side_effects=True, vmem_limit_bytes=VMEM_LIMIT_BYTES),
    )(padded)
    return out[:flat.shape[0]]


def _pool_tile(length):
    return min(256, length)


def _pool_fwd(proj, w_pool):
    length = proj.shape[0]
    ngroups, ch, _ = w_pool.shape
    width, tile = ngroups * ch, _pool_tile(length)

    def body(cur_ref, prev_ref, w_ref, pooled_ref, mixed_ref):
        i = pl.program_id(0)
        cur = cur_ref[...]
        tail = jnp.where(i > 0, prev_ref[tile - POOL_HALO:tile, :], 0.0)
        padded = jnp.concatenate([tail, cur], axis=0)
        pos = (lax.broadcasted_iota(jnp.int32, (tile, 1), 0) + i * tile + 1).astype(F32)
        for g, window in enumerate(POOL_WINDOWS):
            cols = slice(g * ch, (g + 1) * ch)
            run, shift = padded[:, cols], 1
            while shift < window:
                run = run + pltpu.roll(run, shift, 0)
                shift *= 2
            pooled = (run[POOL_HALO:, :] / jnp.minimum(pos, float(window)) - cur[:, cols]).astype(BF16)
            pooled_ref[:, cols] = pooled
            mixed_ref[:, cols] = _dot(pooled, w_ref[g])

    return pl.pallas_call(
        body, name="pool_fwd", grid=(length // tile,),
        in_specs=[pl.BlockSpec((tile, width), lambda i: (i, 0)),
                  pl.BlockSpec((tile, width), lambda i: (jnp.maximum(i - 1, 0), 0)),
                  pl.BlockSpec(w_pool.shape, lambda i: (0, 0, 0))],
        out_specs=[pl.BlockSpec((tile, width), lambda i: (i, 0))] * 2,
        out_shape=[jax.ShapeDtypeStruct((length, width), BF16), jax.ShapeDtypeStruct((length, width), F32)],
        compiler_params=_params("parallel"),
    )(proj, proj, w_pool)


def _pool_bwd_mix(d_mixed, pooled, w_pool):
    length, width = d_mixed.shape
    ngroups, ch, _ = w_pool.shape
    tile = _pool_tile(length)

    def body(dm_ref, pooled_ref, w_ref, dp_ref, dw_ref):
        @pl.when(pl.program_id(0) == 0)
        def _():
            dw_ref[...] = jnp.zeros_like(dw_ref)
        for g in range(ngroups):
            cols = slice(g * ch, (g + 1) * ch)
            dm = dm_ref[:, cols].astype(BF16)
            dp_ref[:, cols] = _dot(dm, w_ref[g], NT)
            dw_ref[g] += _dot(pooled_ref[:, cols], dm, TN)

    return pl.pallas_call(
        body, name="pool_bwd_mix", grid=(length // tile,),
        in_specs=[pl.BlockSpec((tile, width), lambda i: (i, 0)), pl.BlockSpec((tile, width), lambda i: (i, 0)),
                  pl.BlockSpec(w_pool.shape, lambda i: (0, 0, 0))],
        out_specs=[pl.BlockSpec((tile, width), lambda i: (i, 0)), pl.BlockSpec(w_pool.shape, lambda i: (0, 0, 0))],
        out_shape=[jax.ShapeDtypeStruct((length, width), F32), jax.ShapeDtypeStruct(w_pool.shape, F32)],
        compiler_params=_params("arbitrary"),
    )(d_mixed, pooled, w_pool)


def _pool_bwd_window(d_pooled, ngroups):
    length, width = d_pooled.shape
    ch, tile = width // ngroups, _pool_tile(length)
    last = length // tile - 1

    def body(cur_ref, next_ref, dx_ref):
        i = pl.program_id(0)
        cur = cur_ref[...]
        head = jnp.where(i < last, next_ref[0:POOL_HALO, :], 0.0)
        padded = jnp.concatenate([cur, head], axis=0)
        rows = tile + POOL_HALO
        pos = (lax.broadcasted_iota(jnp.int32, (rows, 1), 0) + i * tile + 1).astype(F32)
        for g, window in enumerate(POOL_WINDOWS):
            cols = slice(g * ch, (g + 1) * ch)
            run, shift = padded[:, cols] / jnp.minimum(pos, float(window)), 1
            while shift < window:
                run = run + pltpu.roll(run, rows - shift, 0)
                shift *= 2
            dx_ref[:, cols] = (run[0:tile, :] - cur[:, cols]).astype(BF16)

    return pl.pallas_call(
        body, name="pool_bwd_window", grid=(length // tile,),
        in_specs=[pl.BlockSpec((tile, width), lambda i: (i, 0)),
                  pl.BlockSpec((tile, width), lambda i: (jnp.minimum(i + 1, last), 0))],
        out_specs=pl.BlockSpec((tile, width), lambda i: (i, 0)),
        out_shape=jax.ShapeDtypeStruct((length, width), BF16),
        compiler_params=_params("parallel"),
    )(d_pooled, d_pooled)


ATTN_TILE = 256
LOG_WEIGHT_FLOOR = -110.0


def _walk_back(n_chunks, chunk, carry):
    def cond(state):
        return jnp.logical_and(state[0] < n_chunks, jnp.max(state[1]) > LOG_WEIGHT_FLOOR)

    def step(state):
        return (state[0] + 1,) + tuple(chunk(n_chunks - 1 - state[0], tuple(state[1:])))

    return lax.while_loop(cond, step, (jnp.int32(0),) + tuple(carry))[1:]


def _stick_weights(q, kc, upper, run_log, mask):
    z = _dot(q, kc, NT)
    e = jnp.exp(-jnp.abs(z))
    softplus = jnp.maximum(z, 0.0) + jnp.log(1.0 + e)
    log_sig = z - softplus
    log_1m = -softplus if mask is None else jnp.where(mask, -softplus, 0.0)
    suffix = _dot(log_1m.astype(BF16), upper) + run_log
    w = jnp.exp(log_sig + suffix)
    if mask is not None:
        w = jnp.where(mask, w, 0.0)
    return w, log_sig, suffix[:, 0:1] + log_1m[:, 0:1]


def _attn_consts(tile):
    jj = lax.broadcasted_iota(jnp.int32, (tile, tile), 0)
    ss = lax.broadcasted_iota(jnp.int32, (tile, tile), 1)
    return (jj > ss).astype(BF16), (jj >= ss).astype(BF16), ss < jj


HEADS_PER_STEP = 2


def _heads_per_step(n_heads, *blocks):
    ok = n_heads % HEADS_PER_STEP == 0 and all(b % HEADS_PER_STEP == 0 for b in blocks)
    return HEADS_PER_STEP if ok else 1


def _slowest(run_logs):
    out = run_logs[0]
    for r in run_logs[1:]:
        out = jnp.maximum(out, r)
    return out


def _attn_fwd(proj, n_heads, q_blk, k_blk, v_blk):
    length = proj.shape[0]
    tile = min(ATTN_TILE, length)
    scale = HEAD_DIM ** -0.5

    hps = _heads_per_step(n_heads, q_blk, k_blk, v_blk)
    width = hps * HEAD_DIM

    def body(q_ref, k_ref, v_ref, o_ref):
        i = pl.program_id(1)
        cols = [slice(h * HEAD_DIM, (h + 1) * HEAD_DIM) for h in range(hps)]
        qs = [(q_ref[:, c] * scale).astype(BF16) for c in cols]
        upper, _, diag_mask = _attn_consts(tile)

        def chunk(j, carry, mask):
            start = pl.multiple_of(j * tile, tile)
            new = []
            for h, c in enumerate(cols):
                run_log, acc = carry[1 + 2 * h], carry[2 + 2 * h]
                kc = k_ref[pl.ds(start, tile), c].astype(BF16)
                vc = v_ref[pl.ds(start, tile), c].astype(BF16)
                w, _, run_log = _stick_weights(qs[h], kc, upper, run_log, mask)
                new += [run_log, acc + _dot(w.astype(BF16), vc)]
            return (_slowest(new[0::2]),) + tuple(new)

        zero = jnp.zeros((tile, 1), F32)
        carry = (zero,) + (zero, jnp.zeros((tile, HEAD_DIM), F32)) * hps
        carry = chunk(i, carry, diag_mask)
        carry = _walk_back(i, lambda j, cr: chunk(j, cr, None), carry)
        for h, c in enumerate(cols):
            o_ref[:, c] = carry[2 + 2 * h]

    return pl.pallas_call(
        body, name="attn_fwd", grid=(n_heads // hps, length // tile),
        in_specs=[pl.BlockSpec((tile, width), lambda h, i: (i, q_blk // hps + h)),
                  pl.BlockSpec((length, width), lambda h, i: (0, k_blk // hps + h)),
                  pl.BlockSpec((length, width), lambda h, i: (0, v_blk // hps + h))],
        out_specs=pl.BlockSpec((tile, width), lambda h, i: (i, h)),
        out_shape=jax.ShapeDtypeStruct((length, n_heads * HEAD_DIM), F32),
        compiler_params=_params("parallel", "parallel"),
    )(proj, proj, proj)


def _attn_bwd(proj, out, d_out, n_heads, q_blk, k_blk, v_blk):
    length = proj.shape[0]
    tile = min(ATTN_TILE, length)
    scale = HEAD_DIM ** -0.5

    hps = _heads_per_step(n_heads, q_blk, k_blk, v_blk)
    step_width = hps * HEAD_DIM

    def body(q_ref, k_ref, v_ref, o_ref, do_ref, dq_ref, dk_ref, dv_ref):
        i = pl.program_id(1)

        @pl.when(i == 0)
        def _():
            dk_ref[...] = jnp.zeros_like(dk_ref)
            dv_ref[...] = jnp.zeros_like(dv_ref)

        cols = [slice(h * HEAD_DIM, (h + 1) * HEAD_DIM) for h in range(hps)]
        qs = [(q_ref[:, c] * scale).astype(BF16) for c in cols]
        dos = [do_ref[:, c].astype(BF16) for c in cols]
        totals = [jnp.sum(do.astype(F32) * o_ref[:, c], axis=1, keepdims=True) for do, c in zip(dos, cols)]
        upper, upper_incl, diag_mask = _attn_consts(tile)

        def chunk(j, carry, mask):
            start = pl.multiple_of(j * tile, tile)
            new = []
            for h, c in enumerate(cols):
                run_log, run_g, dq = carry[1 + 3 * h:4 + 3 * h]
                q, do = qs[h], dos[h]
                kc = k_ref[pl.ds(start, tile), c].astype(BF16)
                vc = v_ref[pl.ds(start, tile), c].astype(BF16)
                w, log_sig, run_log = _stick_weights(q, kc, upper, run_log, mask)
                wb = w.astype(BF16)
                g = wb.astype(F32) * _dot(do, vc, NT)
                g_hi, g_lo = _split(g)
                g_suffix = _dot(g_hi, upper_incl) + _dot(g_lo, upper_incl) + run_g
                dz = g - jnp.exp(log_sig) * (g + (totals[h] - g_suffix))
                if mask is not None:
                    dz = jnp.where(mask, dz, 0.0)
                dzb = dz.astype(BF16)
                dk_ref[pl.ds(start, tile), c] += _dot(dzb, q, TN)
                dv_ref[pl.ds(start, tile), c] += _dot(wb, do, TN)
                new += [run_log, g_suffix[:, 0:1], dq + _dot(dzb, kc)]
            return (_slowest(new[0::3]),) + tuple(new)

        zero = jnp.zeros((tile, 1), F32)
        carry = (zero,) + (zero, zero, jnp.zeros((tile, HEAD_DIM), F32)) * hps
        carry = chunk(i, carry, diag_mask)
        carry = _walk_back(i, lambda j, cr: chunk(j, cr, None), carry)
        for h, c in enumerate(cols):
            dq_ref[:, c] = (carry[3 + 3 * h] * scale).astype(BF16)

    width = n_heads * HEAD_DIM
    tile_spec = pl.BlockSpec((tile, step_width), lambda h, i: (i, h))
    head_spec = pl.BlockSpec((length, step_width), lambda h, i: (0, h))
    return pl.pallas_call(
        body, name="attn_bwd", grid=(n_heads // hps, length // tile),
        in_specs=[pl.BlockSpec((tile, step_width), lambda h, i: (i, q_blk // hps + h)),
                  pl.BlockSpec((length, step_width), lambda h, i: (0, k_blk // hps + h)),
                  pl.BlockSpec((length, step_width), lambda h, i: (0, v_blk // hps + h)),
                  tile_spec, tile_spec],
        out_specs=[tile_spec, head_spec, head_spec],
        out_shape=[jax.ShapeDtypeStruct((length, width), BF16), jax.ShapeDtypeStruct((length, width), F32),
                   jax.ShapeDtypeStruct((length, width), F32)],
        compiler_params=_params("parallel", "arbitrary"),
    )(proj, proj, proj, out, d_out)


SCAN_CHUNK = 128


def _disc_lam(lam_re, lam_im, log_dt):
    dt = jnp.exp(log_dt)
    mag, phase = jnp.exp(lam_re * dt), lam_im * dt
    bar_re, bar_im = mag * jnp.cos(phase), mag * jnp.sin(phase)
    num_re, den = bar_re - 1.0, lam_re * lam_re + lam_im * lam_im
    return (bar_re, bar_im, (num_re * lam_re + bar_im * lam_im) / den, (bar_im * lam_re - num_re * lam_im) / den)


def _disc_b(cf_re, cf_im, b_re, b_im):
    return cf_re * b_re - cf_im * b_im, cf_re * b_im + cf_im * b_re


SCAN_UNROLL = 8
SUBLANES = 8


def _state_rows(per_group):
    return per_group.reshape(-1, SUBLANES, LANES)


def _swap_parts(x):
    pieces = []
    for k in range(x.shape[0] // (2 * SUBLANES)):
        base = 2 * SUBLANES * k
        pieces += [x[base + SUBLANES:base + 2 * SUBLANES], x[base:base + SUBLANES]]
    return jnp.concatenate(pieces, axis=0)


def _scan_coeffs(re_rows, im_rows, conj):
    same, cross = [], []
    for k in range(re_rows.shape[0]):
        same += [re_rows[k], re_rows[k]]
        cross += [im_rows[k], -im_rows[k]] if conj else [-im_rows[k], im_rows[k]]
    return jnp.concatenate(same, axis=0), jnp.concatenate(cross, axis=0)


def _scan_fwd(bu, bar_re, bar_im):
    length, groups, width = bu.shape
    chunk = min(SCAN_CHUNK, length)

    def body(bu_ref, re_ref, im_ref, st_ref, carry):
        @pl.when(pl.program_id(0) == 0)
        def _():
            carry[...] = jnp.zeros_like(carry)
        a_same, a_cross = _scan_coeffs(re_ref[...], im_ref[...], conj=False)

        def step(blk, x):
            for r in range(SCAN_UNROLL):
                t = blk * SCAN_UNROLL + r
                x = a_same * x + a_cross * _swap_parts(x) + bu_ref[t]
                st_ref[t] = x
            return x

        carry[...] = lax.fori_loop(0, chunk // SCAN_UNROLL, step, carry[...])

    blk = pl.BlockSpec((chunk, groups, width), lambda i: (i, 0, 0))
    par = pl.BlockSpec(bar_re.shape, lambda i: (0, 0, 0))
    return pl.pallas_call(
        body, name="s5_scan_fwd", grid=(length // chunk,), in_specs=[blk, par, par], out_specs=blk,
        out_shape=jax.ShapeDtypeStruct(bu.shape, F32), scratch_shapes=[pltpu.VMEM((groups, width), F32)],
        compiler_params=_params("arbitrary"),
    )(bu, bar_re, bar_im)


def _scan_bwd(d_states, states, bar_re, bar_im):
    length, groups, width = states.shape
    chunk = min(SCAN_CHUNK, length)
    last = length // chunk - 1

    def body(g_ref, st_ref, re_ref, im_ref, out_ref, same_ref, swap_ref, carry):
        @pl.when(pl.program_id(0) == 0)
        def _():
            carry[...] = jnp.zeros_like(carry)
            same_ref[...] = jnp.zeros_like(same_ref)
            swap_ref[...] = jnp.zeros_like(swap_ref)
        a_same, a_cross = _scan_coeffs(re_ref[...], im_ref[...], conj=True)

        def step(blk, cr):
            adj, acc_same, acc_swap = cr
            for r in range(SCAN_UNROLL):
                t = chunk - 1 - (blk * SCAN_UNROLL + r)
                s = st_ref[t]
                acc_same = acc_same + adj * s
                acc_swap = acc_swap + adj * _swap_parts(s)
                adj = g_ref[t] + a_same * adj + a_cross * _swap_parts(adj)
                out_ref[t] = adj
            return adj, acc_same, acc_swap

        adj, acc_same, acc_swap = lax.fori_loop(0, chunk // SCAN_UNROLL, step, (carry[...], same_ref[...], swap_ref[...]))
        carry[...] = adj
        same_ref[...] = acc_same
        swap_ref[...] = acc_swap

    blk = pl.BlockSpec((chunk, groups, width), lambda i: (last - i, 0, 0))
    par = pl.BlockSpec(bar_re.shape, lambda i: (0, 0, 0))
    acc = pl.BlockSpec((groups, width), lambda i: (0, 0))
    return pl.pallas_call(
        body, name="s5_scan_bwd", grid=(length // chunk,), in_specs=[blk, blk, par, par], out_specs=[blk, acc, acc],
        out_shape=[jax.ShapeDtypeStruct(states.shape, F32), jax.ShapeDtypeStruct((groups, width), F32),
                   jax.ShapeDtypeStruct((groups, width), F32)],
        scratch_shapes=[pltpu.VMEM((groups, width), F32)],
        compiler_params=_params("arbitrary"),
    )(d_states, states, bar_re, bar_im)


def _lam_bar_grad(acc_same, acc_swap):
    def fn(same, swap):
        g_re, g_im = [], []
        for k in range(same.shape[0] // (2 * SUBLANES)):
            re, im = slice(2 * SUBLANES * k, 2 * SUBLANES * k + SUBLANES), slice(2 * SUBLANES * k + SUBLANES, 2 * SUBLANES * (k + 1))
            g_re.append(same[re] + same[im])
            g_im.append(swap[im] - swap[re])
        return jnp.concatenate(g_re, axis=0), jnp.concatenate(g_im, axis=0)
    return _whole(fn, "s5_lam_bar_grad", [acc_same, acc_swap], [(acc_same.shape[0] // 2, LANES)] * 2)


def _block_diag(per_group):
    groups, a, b = per_group.shape
    nc = groups // GROUPS_PER_CHUNK
    eye = jnp.eye(GROUPS_PER_CHUNK, dtype=per_group.dtype)
    x = per_group.reshape(nc, GROUPS_PER_CHUNK, a, 1, b) * eye[None, :, None, :, None]
    return x.reshape(nc, GROUPS_PER_CHUNK * a, GROUPS_PER_CHUNK * b)


def _block_diag_part(chunks, a, b):
    nc = chunks.shape[0]
    x = chunks.reshape(nc, GROUPS_PER_CHUNK, a, GROUPS_PER_CHUNK, b)
    x = jnp.stack([x[:, g, :, g, :] for g in range(GROUPS_PER_CHUNK)], axis=1)
    return x.reshape(nc * GROUPS_PER_CHUNK, a, b)


def _epilogue(raw, gate, scale, g):
    return _rms(raw * scale, g) * (gate * _sigmoid(gate))


def _ssm_mid(y, u, d_skip):
    return _gelu(y + d_skip * u)


def _adamw(w, g, m, v):
    m = ADAM_B1 * m + (1.0 - ADAM_B1) * g
    v = ADAM_B2 * v + (1.0 - ADAM_B2) * (g * g)
    m_hat = m / (1.0 - ADAM_B1 ** ADAM_STEP)
    v_hat = v / (1.0 - ADAM_B2 ** ADAM_STEP)
    return -ADAM_LR * (m_hat / (jnp.sqrt(v_hat) + ADAM_EPS) + ADAM_WD * w), m, v


class _Dims:
    def __init__(self, d_model, length):
        self.d, self.length = d_model, length
        self.d_pool, self.d_attn = d_model // 4, d_model // 2
        self.d_ssm = d_model - self.d_pool - self.d_attn
        self.heads = self.d_attn // HEAD_DIM
        self.groups = self.d_ssm // SSM_GROUP
        self.d_in = 2 * self.d_pool + 4 * self.d_attn + 2 * self.d_ssm
        sizes = (self.d_pool, self.d_pool, self.d_attn, self.d_attn, self.d_attn, self.d_attn, self.d_ssm, self.d_ssm)
        offs = [0]
        for s in sizes[:-1]:
            offs.append(offs[-1] + s)
        (self.o_px, self.o_pgate, self.o_q, self.o_k, self.o_v, self.o_agate, self.o_u, self.o_sgate) = offs


def _ssm_operands(dm, p):
    groups, states = dm.groups, SSM_STATE
    bar_re, bar_im, cf_re, cf_im = _whole(_disc_lam, "s5_disc_lam", [p["lam_re"], p["lam_im"], p["log_dt"].reshape(groups, 1)],
                                          [(groups, states)] * 4)
    b_re2, b_im2 = p["b_re"].reshape(groups * states, SSM_GROUP), p["b_im"].reshape(groups * states, SSM_GROUP)
    bb_re, bb_im = _whole(_disc_b, "s5_disc_b", [cf_re.reshape(-1, 1), cf_im.reshape(-1, 1), b_re2, b_im2],
                          [(groups * states, SSM_GROUP)] * 2)
    per_group = lambda a: jnp.swapaxes(a.reshape(groups, states, SSM_GROUP), 1, 2)
    b_blk = jnp.concatenate([_block_diag(per_group(bb_re)), _block_diag(per_group(bb_im))], axis=2)
    c_blk = jnp.concatenate([_block_diag(jnp.swapaxes(p["c_re"], 1, 2)), _block_diag(jnp.swapaxes(-p["c_im"], 1, 2))], axis=1)
    return dict(bar_re=bar_re, bar_im=bar_im, bar_re_rows=_state_rows(bar_re), bar_im_rows=_state_rows(bar_im),
                cf_re=cf_re, cf_im=cf_im, b_re2=b_re2, b_im2=b_im2, b_blk=b_blk, c_blk=c_blk)


def _mm_carrying(carry, key, *args, **kwargs):
    stages = carry.get(key, [])
    exchanges = [make() for make, _ in stages]
    out = _mm(*args, comm=exchanges, **kwargs)
    for (_, deliver), exchange in zip(stages, exchanges):
        deliver(exchange.results)
    return out


def _layer_fwd(dm, x_in, p, gw, carry):
    length = dm.length
    blk = lambda off, w: off // w
    (h,), _ = _rowwise(lambda x, g: ((_rms(x, g),), ()), "rms_fwd", [(x_in, dm.d, 0)], [p["ln_g"]], [(dm.d, BF16)])
    proj = _mm_carrying(carry, "in_proj", h, gw["w_in"], NN, F32, "in_proj", n=dm.d_in, b_quarters="n", tn=WIDE_TILE)
    pooled, mixed = _pool_fwd(proj, gw["w_pool"])
    qb, kb, vb = dm.o_q // HEAD_DIM, dm.o_k // HEAD_DIM, dm.o_v // HEAD_DIM
    attn = _attn_fwd(proj, dm.heads, qb, kb, vb)
    so = _ssm_operands(dm, p)
    u_row = (proj, dm.d_ssm, blk(dm.o_u, dm.d_ssm))
    (u,), _ = _rowwise(lambda v: ((v,), ()), "take_u", [u_row], [], [(dm.d_ssm, F32)])
    bu = _mm3(u, so["b_blk"], NN, "s5_bu", rows3d="o")
    states = _scan_fwd(bu, so["bar_re_rows"], so["bar_im_rows"])
    y = _mm3(states, so["c_blk"], NN, "s5_y", rows3d="a")
    (hg,), _ = _rowwise(lambda yy, uu, dsk: ((_ssm_mid(yy, uu, dsk),), ()), "s5_mid_fwd",
                        [(y, dm.d_ssm, 0), (u, dm.d_ssm, 0)], [p["d_skip"]], [(dm.d_ssm, BF16)])
    z = _mm(hg, gw["w_glu"], NN, F32, "glu_proj", n=2 * dm.d_ssm, b_quarters="n")

    def glu(zz, bias):
        zz = zz + bias
        return (zz[:, :dm.d_ssm] * _sigmoid(zz[:, dm.d_ssm:]),), ()

    (ssm,), _ = _rowwise(glu, "glu_fwd", [(z, 2 * dm.d_ssm, 0)], [p["b_glu"]], [(dm.d_ssm, F32)])
    g_pool, g_attn, g_ssm = (p["branch_g"][:, :dm.d_pool], p["branch_g"][:, dm.d_pool:dm.d_pool + dm.d_attn],
                             p["branch_g"][:, dm.d_pool + dm.d_attn:])
    ones_attn, ones_ssm = jnp.ones((1, dm.d_attn), F32), jnp.ones((1, dm.d_ssm), F32)
    epi = lambda raw, gate, scale, g: ((_epilogue(raw, gate, scale, g),), ())
    branches = [("pool", mixed, dm.d_pool, dm.o_pgate, p["pool_scale"], g_pool),
                ("attn", attn, dm.d_attn, dm.o_agate, ones_attn, g_attn),
                ("ssm", ssm, dm.d_ssm, dm.o_sgate, ones_ssm, g_ssm)]
    ys = []
    for nm, raw, w, off, scale, g in branches:
        (yb,), _ = _rowwise(epi, "epilogue_fwd_" + nm, [(raw, w, 0), (proj, w, blk(off, w))], [scale, g], [(w, BF16)])
        ys.append(yb)
    y_cat = jnp.concatenate(ys, axis=1)
    x_out = _mm_carrying(carry, "out_proj", y_cat, gw["w_out"], NN, F32, "out_proj", add=x_in, tn=WIDE_TILE)
    saved = dict(x_in=x_in, h=h, proj=proj, pooled=pooled, mixed=mixed, attn=attn, so=so, u=u, states=states,
                 y=y, hg=hg, z=z, ssm=ssm, y_cat=y_cat, scales=(p["pool_scale"], ones_attn, ones_ssm), gs=(g_pool, g_attn, g_ssm))
    return x_out, saved


def _layer_bwd(dm, d_out, d_out_bf, p, gw, sv, want_bf, carry, big):
    length = dm.length
    blk = lambda off, w: off // w
    proj = sv["proj"]
    d_y = _mm_carrying(carry, "out_proj_dgrad", d_out_bf, gw["w_out"], NT, F32, "out_proj_dgrad", tn=WIDE_TILE)
    big["w_out"] = _mm(sv["y_cat"], d_out_bf, TN, F32, "out_proj_wgrad", out_quarters="rows", tn=WIDE_TILE)

    def epi_bwd(nseg):
        def fn(*vals):
            dys, (raw, gate, scale, g) = vals[:nseg], vals[nseg:]
            dyb = dys[0] if nseg == 1 else jnp.concatenate(dys, axis=1)
            _, vjp = jax.vjp(_epilogue, raw, gate, scale, g)
            d_raw, d_gate, d_scale, d_g = vjp(dyb)
            return (d_raw, d_gate), (d_scale, d_g)
        return fn

    branch = [("pool", sv["mixed"], dm.d_pool, dm.o_pgate, 0), ("attn", sv["attn"], dm.d_attn, dm.o_agate, dm.d_pool),
              ("ssm", sv["ssm"], dm.d_ssm, dm.o_sgate, dm.d_pool + dm.d_attn)]
    d_raws, d_gates, d_scales, d_gs = [], [], [], []
    for (nm, raw, w, off, yoff), scale, g in zip(branch, sv["scales"], sv["gs"]):
        seg = math.gcd(w, yoff) if yoff else w
        dy_rows = [(d_y, seg, yoff // seg + s) for s in range(w // seg)]
        (d_raw, d_gate), (d_scale, d_g) = _rowwise(
            epi_bwd(len(dy_rows)), "epilogue_bwd_" + nm, dy_rows + [(raw, w, 0), (proj, w, blk(off, w))], [scale, g],
            [(w, F32), (w, BF16)], [w, w])
        d_raws.append(d_raw); d_gates.append(d_gate); d_scales.append(d_scale); d_gs.append(d_g)
    d_pooled, g_w_pool = _pool_bwd_mix(d_raws[0], sv["pooled"], gw["w_pool"])
    ngr, ch = gw["w_pool"].shape[0], gw["w_pool"].shape[1]
    q_rows = ch // 4
    big["w_pool"] = g_w_pool.reshape(ngr, 4, 2, q_rows // 2, ch).transpose(2, 1, 0, 3, 4).reshape(2, 4, ngr * q_rows // 2, ch)
    d_px = _pool_bwd_window(d_pooled, len(POOL_WINDOWS))
    qb, kb, vb = dm.o_q // HEAD_DIM, dm.o_k // HEAD_DIM, dm.o_v // HEAD_DIM
    d_q, d_k, d_v = _attn_bwd(proj, sv["attn"], d_raws[1], dm.heads, qb, kb, vb)
    so = sv["so"]

    def glu_bwd(d_ssm, zz, bias):
        zz = zz + bias
        val, sg = zz[:, :dm.d_ssm], _sigmoid(zz[:, dm.d_ssm:])
        dz = jnp.concatenate([d_ssm * sg, d_ssm * val * sg * (1.0 - sg)], axis=1)
        return (dz,), (jnp.sum(dz, axis=0, keepdims=True),)

    (d_z,), (g_b_glu,) = _rowwise(glu_bwd, "glu_bwd", [(d_raws[2], dm.d_ssm, 0), (sv["z"], 2 * dm.d_ssm, 0)], [p["b_glu"]],
                                  [(2 * dm.d_ssm, BF16)], [2 * dm.d_ssm])
    d_hg = _mm(d_z, gw["w_glu"], NT, F32, "glu_dgrad", n=dm.d_ssm, b_quarters="k")
    big["w_glu"] = _mm(sv["hg"], d_z, TN, F32, "glu_wgrad", out_quarters="cols")

    def mid_bwd(dh, yy, uu, dsk):
        _, vjp = jax.vjp(_ssm_mid, yy, uu, dsk)
        dy_, du_, ddsk = vjp(dh)
        return (dy_, du_), (ddsk,)

    (d_yssm, d_u_direct), (g_d_skip,) = _rowwise(mid_bwd, "s5_mid_bwd", [(d_hg, dm.d_ssm, 0), (sv["y"], dm.d_ssm, 0), (sv["u"], dm.d_ssm, 0)],
                                                 [p["d_skip"]], [(dm.d_ssm, F32), (dm.d_ssm, F32)], [dm.d_ssm])
    d_states = _mm3(d_yssm, so["c_blk"], NT, "s5_y_dgrad", rows3d="o")
    d_c_blk = _mm3(sv["states"], d_yssm, TN, "s5_y_wgrad", nc=so["c_blk"].shape[0], rows3d="a")
    d_bu, acc_same, acc_swap = _scan_bwd(d_states, sv["states"], so["bar_re_rows"], so["bar_im_rows"])
    d_u_scan = _mm3(d_bu, so["b_blk"], NT, "s5_bu_dgrad", rows3d="a")
    d_b_blk = _mm3(sv["u"], d_bu, TN, "s5_bu_wgrad", nc=so["b_blk"].shape[0], rows3d="b")
    (d_u,) = _elementwise(lambda a, b: (a + b,), "s5_du", [d_u_direct, d_u_scan], [BF16])
    groups, states = dm.groups, SSM_STATE
    part = GROUPS_PER_CHUNK * states
    g_c_re = jnp.swapaxes(_block_diag_part(d_c_blk[:, :part], states, SSM_GROUP), 1, 2)
    g_c_im_neg = jnp.swapaxes(_block_diag_part(d_c_blk[:, part:], states, SSM_GROUP), 1, 2)
    d_bb_re = jnp.swapaxes(_block_diag_part(d_b_blk[:, :, :part], SSM_GROUP, states), 1, 2).reshape(-1, SSM_GROUP)
    d_bb_im = jnp.swapaxes(_block_diag_part(d_b_blk[:, :, part:], SSM_GROUP, states), 1, 2).reshape(-1, SSM_GROUP)

    def disc_b_bwd(cf_re, cf_im, b_re, b_im, g_re, g_im):
        _, vjp = jax.vjp(_disc_b, cf_re, cf_im, b_re, b_im)
        return vjp((g_re, g_im))

    d_cf_re, d_cf_im, g_b_re, g_b_im = _whole(
        disc_b_bwd, "s5_disc_b_bwd", [so["cf_re"].reshape(-1, 1), so["cf_im"].reshape(-1, 1), so["b_re2"], so["b_im2"], d_bb_re, d_bb_im],
        [(groups * states, 1)] * 2 + [(groups * states, SSM_GROUP)] * 2)

    def disc_lam_bwd(lam_re, lam_im, log_dt, g_bar_re, g_bar_im, g_cf_re, g_cf_im):
        _, vjp = jax.vjp(_disc_lam, lam_re, lam_im, log_dt)
        return vjp((g_bar_re, g_bar_im, g_cf_re, g_cf_im))

    g_bar_re, g_bar_im = _lam_bar_grad(acc_same, acc_swap)

    g_lam_re, g_lam_im, g_log_dt = _whole(
        disc_lam_bwd, "s5_disc_lam_bwd", [p["lam_re"], p["lam_im"], p["log_dt"].reshape(groups, 1), g_bar_re.reshape(groups, states), g_bar_im.reshape(groups, states),
                                          d_cf_re.reshape(groups, states), d_cf_im.reshape(groups, states)],
        [(groups, states)] * 2 + [(groups, 1)])
    (g_c_im,) = _elementwise(lambda a: (-a,), "s5_neg_c_im", [g_c_im_neg.reshape(groups * SSM_GROUP, states)], [F32])
    d_proj = jnp.concatenate([d_px, d_gates[0], d_q, d_k.astype(BF16), d_v.astype(BF16), d_gates[1], d_u, d_gates[2]], axis=1)
    d_h = _mm_carrying(carry, "in_proj_dgrad", d_proj, gw["w_in"], NT, F32, "in_proj_dgrad", n=dm.d, b_quarters="k", tm=WIDE_TILE)
    g_w_in = _mm_carrying(carry, "in_proj_wgrad", sv["h"], d_proj, TN, F32, "in_proj_wgrad", out_quarters="cols", tn=WIDE_TILE)

    def rms_bwd(dh, xx, dres, g):
        _, vjp = jax.vjp(_rms, xx, g)
        dx, dg = vjp(dh)
        dx = dx + dres
        return ((dx, dx) if want_bf else (dx,)), (dg,)

    d_xs, (g_ln_g,) = _rowwise(rms_bwd, "rms_bwd", [(d_h, dm.d, 0), (sv["x_in"], dm.d, 0), (d_out, dm.d, 0)], [p["ln_g"]],
                               [(dm.d, F32), (dm.d, BF16)] if want_bf else [(dm.d, F32)], [dm.d])
    big["w_in"] = g_w_in
    small = dict(ln_g=g_ln_g, pool_scale=d_scales[0], lam_re=g_lam_re, lam_im=g_lam_im, log_dt=g_log_dt.reshape(1, groups),
                 b_re=g_b_re, b_im=g_b_im, c_re=g_c_re, c_im=g_c_im, d_skip=g_d_skip, b_glu=g_b_glu,
                 branch_g=jnp.concatenate(d_gs, axis=1))
    return d_xs[0], (d_xs[1] if want_bf else None), small


SMALL_ROWS = 8


def _pack(arrays):
    parts = []
    for a in arrays:
        flat = a.reshape(-1)
        pad = (-flat.shape[0]) % (SMALL_ROWS * LANES)
        parts.append(jnp.pad(flat, (0, pad)).reshape(-1, LANES))
    return jnp.concatenate(parts, axis=0)


def _unpack(buf, like):
    res, row = [], 0
    for a in like:
        size = math.prod(a.shape)
        rows = -(-size // (SMALL_ROWS * LANES)) * SMALL_ROWS
        res.append(buf[row:row + rows].reshape(-1)[:size].reshape(a.shape))
        row += rows
    return res


def kernel(x, ln_g, w_in, w_pool, pool_scale, lam_re, lam_im, log_dt, b_re, b_im, c_re, c_im, d_skip, w_glu, b_glu, branch_g, w_out, final_g, loss_target, m_ln_g, m_w_in, m_w_pool, m_pool_scale, m_lam_re, m_lam_im, m_log_dt, m_b_re, m_b_im, m_c_re, m_c_im, m_d_skip, m_w_glu, m_b_glu, m_branch_g, m_w_out, m_final_g, v_ln_g, v_w_in, v_w_pool, v_pool_scale, v_lam_re, v_lam_im, v_log_dt, v_b_re, v_b_im, v_c_re, v_c_im, v_d_skip, v_w_glu, v_b_glu, v_branch_g, v_w_out, v_final_g):
    given = dict(locals())
    weights = {n: given[n] for n in WEIGHTS}
    depth = ln_g.shape[0]
    _, length, d_model = x.shape
    dm = _Dims(d_model, length)
    x0, target = x[0], loss_target[0]
    c_idx = lax.axis_index("c")
    my_quarter = 2 * lax.axis_index("x") + lax.axis_index("y")

    shard2d = {(n, l): weights[n][l].reshape(-1, weights[n].shape[-1]) for l in range(depth) for n in SHARDED}
    keys = list(shard2d)
    halves16 = {k: w.astype(BF16).reshape(2, w.shape[0] // 2, -1) for k, w in shard2d.items()}
    gw = [dict() for _ in range(depth)]

    def gather(group, name):
        return _allgather_quarters([halves16[k] for k in group], name)

    def deliver_weights(group):
        def deliver(results):
            for (n, l), g in zip(group, results):
                rows, cols = shard2d[(n, l)].shape
                g = g.reshape(4, rows, cols)
                if n == "w_out":
                    g = g.reshape(4 * rows, cols)
                if n == "w_pool":
                    ngr = w_pool.shape[1]
                    g = g.reshape(4, ngr, rows // ngr, cols).transpose(1, 0, 2, 3).reshape(ngr, 4 * rows // ngr, cols)
                gw[l][n] = g
        return deliver

    first = [("w_in", 0), ("w_pool", 0), ("w_out", 0)]
    behind_in_proj = [("w_glu", 0)] + ([("w_in", 1)] if depth > 1 else [])
    behind_out_proj = [k for k in keys if k not in first + behind_in_proj]
    deliver_weights(first)(gather(first, "allgather_first").run())
    fwd_carry = [dict() for _ in range(depth)]
    fwd_carry[0]["in_proj"] = [(lambda: gather(behind_in_proj, "allgather_behind_in_proj"), deliver_weights(behind_in_proj))]
    if behind_out_proj:
        fwd_carry[0]["out_proj"] = [(lambda: gather(behind_out_proj, "allgather_behind_out_proj"), deliver_weights(behind_out_proj))]
    small_names = [n for n in WEIGHTS if n not in SHARDED and n != "final_g"]
    ps = [{n: (weights[n][l].reshape(1, -1) if weights[n][l].ndim == 1 else weights[n][l]) for n in small_names} for l in range(depth)]

    acts, saved = x0, []
    for l in range(depth):
        acts, sv = _layer_fwd(dm, acts, ps[l], gw[l], fwd_carry[l])
        saved.append(sv)

    def final(xx, tt, g):
        def loss_fn(xv, gv):
            err = _rms(xv, gv) - tt
            return 0.5 * jnp.sum(jnp.mean(err * err, axis=-1))
        val, (dx, dg) = jax.value_and_grad(loss_fn, argnums=(0, 1))(xx, g)
        return (dx, dx), (val.reshape(1, 1), dg)

    (d_act, d_act_bf), (loss_part, g_final_g) = _rowwise(
        final, "final_norm_loss", [(acts, dm.d, 0), (target, dm.d, 0)], [final_g.reshape(1, -1)], [(dm.d, F32), (dm.d, BF16)], [1, dm.d])
    loss = lax.psum(loss_part[0, 0], AXES)

    place = jnp.stack([c_idx, my_quarter]).astype(jnp.int32)
    big, small, mine, theirs = [dict() for _ in range(depth)], [None] * depth, {}, {}
    chip = {}

    def to_sibling_stage(group, tag):
        def deliver(from_sibling):
            for (n, l), r in zip(group, from_sibling):
                chip[(n, l)] = _chip_sum(big[l][n], r, place)
        return (lambda: _to_sibling([big[l][n] for n, l in group], "grads_to_sibling" + tag, other_half=True), deliver)

    def to_owner_stage(group, tag):
        def deliver(from_chips):
            for k, r in zip(group, from_chips):
                mine[k] = _owner_sum(chip[k][1], r)
        return (lambda: _to_owner_chips([chip[k][0] for k in group], "grads_to_owner_chips" + tag), deliver)

    def halves_stage(group, tag):
        def deliver(from_sibling):
            theirs.update(zip(group, from_sibling))
        return (lambda: _to_sibling([mine[k] for k in group], "reduced_half_to_sibling" + tag), deliver)

    deferred = {}
    for l in reversed(range(depth)):
        rest = [(n, l) for n in SHARDED if n != "w_in"]
        tag = "_%d" % l
        carry = {key: list(stages) for key, stages in deferred.items()}
        carry.setdefault("in_proj_dgrad", []).append(to_sibling_stage(rest, "_rest" + tag))
        carry.setdefault("in_proj_wgrad", []).append(to_owner_stage(rest, "_rest" + tag))
        d_act, d_act_bf, small[l] = _layer_bwd(dm, d_act, d_act_bf, ps[l], gw[l], saved[l], want_bf=l > 0, carry=carry, big=big[l])
        deferred = {"out_proj_dgrad": [to_sibling_stage([("w_in", l)], "_w_in" + tag)],
                    "in_proj_dgrad": [to_owner_stage([("w_in", l)], "_w_in" + tag)],
                    "in_proj_wgrad": [halves_stage([(n, l) for n in SHARDED], tag)]}
    for key in ("out_proj_dgrad", "in_proj_dgrad", "in_proj_wgrad"):
        for make, deliver in deferred[key]:
            deliver(make().run())
    grad_x = d_act[None]

    small_all = [n for n in WEIGHTS if n not in SHARDED]
    packed = _pack([small[l][n] for l in range(depth) for n in small_names] + [g_final_g])
    summed = _unpack(_allreduce_small(packed), [weights[n][l] for l in range(depth) for n in small_names] + [final_g])
    g_small = {n: jnp.stack([summed[l * len(small_names) + i] for l in range(depth)]) for i, n in enumerate(small_names)}
    g_small["final_g"] = summed[-1]

    out_g, out_d, out_m, out_v = {}, {}, {}, {}
    for n in SHARDED:
        shape = weights[n].shape
        if n == "w_pool":
            ngr = shape[1]
            both = jnp.stack([jnp.where(c_idx == 0, jnp.stack([mine[(n, l)], theirs[(n, l)]]), jnp.stack([theirs[(n, l)], mine[(n, l)]]))
                              for l in range(depth)])
            g = both.reshape(depth, 2, ngr, -1, shape[-1]).transpose(0, 2, 1, 3, 4).reshape(shape)
            res = [g] + _elementwise(lambda *a: _adamw(*a), "adamw_" + n, [weights[n], g, given["m_" + n], given["v_" + n]], [F32] * 3)
        else:
            halves = lambda a: a.reshape(depth, 2, -1, shape[-1])
            res = _adamw_halves(halves(weights[n]), halves(given["m_" + n]), halves(given["v_" + n]),
                                [mine[(n, l)] for l in range(depth)], [theirs[(n, l)] for l in range(depth)], place, "adamw_" + n)
        out_g[n], out_d[n], out_m[n], out_v[n] = [r.reshape(shape) for r in res]
    d, m, v = _whole(_adamw, "adamw_small", [_pack([weights[n] for n in small_all]), _pack([g_small[n] for n in small_all]),
                                            _pack([given["m_" + n] for n in small_all]), _pack([given["v_" + n] for n in small_all])],
                     [_pack([weights[n] for n in small_all]).shape] * 3)
    like = [weights[n] for n in small_all]
    for n, dd, mm, vv in zip(small_all, _unpack(d, like), _unpack(m, like), _unpack(v, like)):
        out_g[n], out_d[n], out_m[n], out_v[n] = g_small[n], dd, mm, vv
    return (loss, grad_x, *[out_g[n] for n in WEIGHTS], *[out_d[n] for n in WEIGHTS],
            *[out_m[n] for n in WEIGHTS], *[out_v[n] for n in WEIGHTS])
```

```python
import functools
import math

import jax
import jax.numpy as jnp
from jax import lax
from jax.experimental import pallas as pl
from jax.experimental.pallas import tpu as pltpu

F32 = jnp.float32
BF16 = jnp.bfloat16
EPS = 1e-6
POOL_WINDOWS = (2, 4, 8, 16)
POOL_HALO = 16
HEAD_DIM = 128
SSM_GROUP = 16
SSM_STATE = 64
GROUPS_PER_CHUNK = 16
LANES = 128
VMEM_LIMIT_BYTES = 56 * 1024 * 1024
ROW_TILE_ELEMS = 2 * 1024 * 1024
WIDE_TILE = 1024
ADAM_LR, ADAM_B1, ADAM_B2, ADAM_EPS, ADAM_WD, ADAM_STEP = 0.001, 0.9, 0.999, 1e-08, 0.01, 10
MESH = pl.DeviceIdType.MESH
AXES = ("x", "y", "c")
WEIGHTS = ("ln_g", "w_in", "w_pool", "pool_scale", "lam_re", "lam_im", "log_dt", "b_re", "b_im",
           "c_re", "c_im", "d_skip", "w_glu", "b_glu", "branch_g", "w_out", "final_g")
SHARDED = ("w_in", "w_pool", "w_glu", "w_out")


def _params(*sem):
    return pltpu.CompilerParams(dimension_semantics=sem or None, vmem_limit_bytes=VMEM_LIMIT_BYTES)


def _dot(a, b, dims=((1,), (0,))):
    return lax.dot_general(a, b, (dims, ((), ())), preferred_element_type=F32)


NN, NT, TN = ((1,), (0,)), ((1,), (1,)), ((0,), (0,))


def _split(x):
    hi = x.astype(BF16)
    return hi, (x - hi.astype(F32)).astype(BF16)


def _dot_rounded(a, b, dims):
    return _dot(a.astype(BF16), b.astype(BF16), dims)


def _sigmoid(x):
    return 1.0 / (1.0 + jnp.exp(-x))


def _gelu(x):
    return 0.5 * x * (1.0 + jnp.tanh(0.7978845608028654 * (x + 0.044715 * x * x * x)))


def _rms(x, g):
    return x * lax.rsqrt(jnp.mean(x * x, axis=-1, keepdims=True) + EPS) * g


HBM_SPEC = pl.BlockSpec(memory_space=pltpu.HBM)


def _place():
    x, y, c = lax.axis_index("x"), lax.axis_index("y"), lax.axis_index("c")
    chips = [(1 - x, y), (x, 1 - y), (1 - x, 1 - y)]
    return x, y, c, chips


def _remote(src, dst, send_sem, recv_sem, target):
    return pltpu.make_async_remote_copy(src_ref=src, dst_ref=dst, send_sem=send_sem, recv_sem=recv_sem,
                                        device_id=target, device_id_type=MESH)


class _Exchange:
    def __init__(self, name, inputs, out_shapes, sems, start, finish):
        self.name, self.inputs, self.out_shapes, self.sems = name, list(inputs), list(out_shapes), sems
        self.start, self.finish, self.results = start, finish, None

    def run(self):
        n_in, n_out = len(self.inputs), len(self.out_shapes)

        def body(*refs):
            parts = (refs[:n_in], refs[n_in:n_in + n_out], refs[-2], refs[-1])
            self.start(*parts)
            self.finish(*parts)

        self.results = pl.pallas_call(
            body, name=self.name, in_specs=[HBM_SPEC] * n_in, out_specs=[HBM_SPEC] * n_out, out_shape=self.out_shapes,
            scratch_shapes=[pltpu.SemaphoreType.DMA(self.sems), pltpu.SemaphoreType.DMA(self.sems)],
            compiler_params=pltpu.CompilerParams(has_side_effects=True),
        )(*self.inputs)
        return self.results


def _allgather_quarters(shards, name):
    n, own = len(shards), 6

    def start(ins, outs, send_sems, recv_sems):
        x, y, c, chips = _place()
        mine = 2 * x + y
        for t in range(n):
            for k, (cx, cy) in enumerate(chips):
                _remote(ins[t].at[c], outs[t].at[mine, c], send_sems.at[t, k], recv_sems.at[t, k], (cx, cy, c)).start()
            _remote(ins[t], outs[t].at[mine], send_sems.at[t, own], recv_sems.at[t, own], (x, y, 1 - c)).start()

    def finish(ins, outs, send_sems, recv_sems):
        x, y, c, chips = _place()
        mine, sibling = 2 * x + y, (x, y, 1 - c)
        for t in range(n):
            for k, (cx, cy) in enumerate(chips):
                landed = outs[t].at[2 * cx + cy, c]
                _remote(landed, landed, send_sems.at[t, k], recv_sems.at[t, k], sibling).wait_recv()
                _remote(landed, landed, send_sems.at[t, 3 + k], recv_sems.at[t, 3 + k], sibling).start()
        for t in range(n):
            got = outs[t].at[mine]
            _remote(got, got, send_sems.at[t, own], recv_sems.at[t, own], sibling).wait_recv()
            _remote(ins[t], got, send_sems.at[t, own], recv_sems.at[t, own], sibling).wait_send()
            for k, (cx, cy) in enumerate(chips):
                got = outs[t].at[2 * cx + cy, 1 - c]
                _remote(got, got, send_sems.at[t, 3 + k], recv_sems.at[t, 3 + k], sibling).wait_recv()
                sent = outs[t].at[2 * cx + cy, c]
                _remote(sent, sent, send_sems.at[t, 3 + k], recv_sems.at[t, 3 + k], sibling).wait_send()
                _remote(ins[t].at[c], sent, send_sems.at[t, k], recv_sems.at[t, k], sibling).wait_send()

    return _Exchange(name, shards, [jax.ShapeDtypeStruct((4,) + s.shape, s.dtype) for s in shards], (n, 7), start, finish)


def _to_sibling(arrays, name, other_half=False):
    n = len(arrays)

    def copies(ins, outs, send_sems, recv_sems):
        x, y, c, _ = _place()
        return [_remote(ins[t].at[1 - c] if other_half else ins[t], outs[t], send_sems.at[t], recv_sems.at[t], (x, y, 1 - c))
                for t in range(n)]

    def start(*refs):
        for cp in copies(*refs):
            cp.start()

    def finish(*refs):
        for cp in copies(*refs):
            cp.wait()

    shapes = [jax.ShapeDtypeStruct(a.shape[1:] if other_half else a.shape, a.dtype) for a in arrays]
    return _Exchange(name, arrays, shapes, (n,), start, finish)


def _to_owner_chips(arrays, name):
    n = len(arrays)

    def copies(ins, outs, send_sems, recv_sems):
        x, y, c, chips = _place()
        return [_remote(ins[t].at[2 * cx + cy], outs[t].at[k], send_sems.at[t, k], recv_sems.at[t, k], (cx, cy, c))
                for t in range(n) for k, (cx, cy) in enumerate(chips)]

    def start(*refs):
        for cp in copies(*refs):
            cp.start()

    def finish(*refs):
        for cp in copies(*refs):
            cp.wait()

    shapes = [jax.ShapeDtypeStruct((3,) + a.shape[1:], a.dtype) for a in arrays]
    return _Exchange(name, arrays, shapes, (n, 3), start, finish)


def _fit(tile, dim):
    tile = min(tile, dim)
    step = LANES if tile >= LANES else 8
    tile -= tile % step
    while dim % tile:
        tile -= step
    return tile


def _mm(a, b, dims, out_dtype, name, *, n=None, b_quarters=None, out_quarters=None, add=None, comm=None, tm=512, tn=512, tk=4096):
    if dims == TN:
        k_dim, m_dim = a.shape
    else:
        m_dim, k_dim = a.shape
    if n is None:
        n = b.shape[0] if dims == NT else b.shape[1]
    m_unit = {None: m_dim, "cols": m_dim // 2, "rows": m_dim // 8}[out_quarters]
    n_unit = n // 4 if (b_quarters == "n" or out_quarters == "cols") else n
    k_unit = k_dim // 4 if b_quarters == "k" else k_dim
    tm, tn, tk = _fit(tm, m_unit), _fit(tn, n_unit), _fit(tk, k_unit)
    gm, gn, gk = m_dim // tm, n // tn, k_dim // tk
    mb, nb, kb = m_unit // tm, n_unit // tn, k_unit // tk
    if dims == TN:
        a_spec = pl.BlockSpec((tk, tm), lambda i, j, k: (k, i))
    else:
        a_spec = pl.BlockSpec((tm, tk), lambda i, j, k: (i, k))
    if b_quarters == "n":
        bspec = pl.BlockSpec((None, tk, tn), lambda i, j, k: (j // nb, k, j % nb))
    elif b_quarters == "k":
        bspec = pl.BlockSpec((None, tn, tk), lambda i, j, k: (k // kb, j, k % kb))
    elif dims == NT:
        bspec = pl.BlockSpec((tn, tk), lambda i, j, k: (j, k))
    else:
        bspec = pl.BlockSpec((tk, tn), lambda i, j, k: (k, j))
    if out_quarters == "cols":
        out_shape = (2, 4, m_unit, n_unit)
        out_spec = pl.BlockSpec((None, None, tm, tn), lambda i, j, k: (i // mb, j // nb, i % mb, j % nb))
    elif out_quarters == "rows":
        out_shape = (2, 4, m_unit, n)
        out_spec = pl.BlockSpec((None, None, tm, tn), lambda i, j, k: ((i // mb) % 2, i // (2 * mb), i % mb, j))
    else:
        out_shape, out_spec = (m_dim, n), pl.BlockSpec((tm, tn), lambda i, j, k: (i, j))
    in_specs, operands = [a_spec, bspec], [a, b]
    if add is not None:
        in_specs.append(pl.BlockSpec((tm, tn), lambda i, j, k: (i, j)))
        operands.append(add)

    n_in = len(operands)
    comms = list(comm) if comm else []
    n_cin, n_cout = sum(len(e.inputs) for e in comms), sum(len(e.out_shapes) for e in comms)
    n_acc = 1 if gk > 1 else 0

    def body(*refs):
        a_ref, b_ref = refs[0], refs[1]
        add_ref = refs[2] if add is not None else None
        o_ref = refs[n_in + n_cin]
        if comms:
            c_refs, i_at, o_at, s_at = [], n_in, n_in + n_cin + 1, n_in + n_cin + 1 + n_cout + n_acc
            for e in comms:
                c_refs.append((refs[i_at:i_at + len(e.inputs)], refs[o_at:o_at + len(e.out_shapes)], refs[s_at], refs[s_at + 1]))
                i_at, o_at, s_at = i_at + len(e.inputs), o_at + len(e.out_shapes), s_at + 2
            step = (pl.program_id(0) * gn + pl.program_id(1)) * gk + pl.program_id(2)

            @pl.when(step == 0)
            def _():
                for e, r in zip(comms, c_refs):
                    e.start(*r)

        def finish(r):
            if add_ref is not None:
                r = r + add_ref[...]
            o_ref[...] = r.astype(o_ref.dtype)

        if gk == 1:
            finish(_dot(a_ref[...], b_ref[...], dims))
        else:
            acc = refs[n_in + n_cin + 1 + n_cout]
            k = pl.program_id(2)

            @pl.when(k == 0)
            def _():
                acc[...] = jnp.zeros_like(acc)

            acc[...] += _dot(a_ref[...], b_ref[...], dims)

            @pl.when(k == gk - 1)
            def _():
                finish(acc[...])

        if comms:
            @pl.when(step == gm * gn * gk - 1)
            def _():
                for e, r in zip(comms, c_refs):
                    e.finish(*r)

    scratch = [pltpu.VMEM((tm, tn), F32)] if gk > 1 else []
    if not comms:
        return pl.pallas_call(
            body, name=name, grid=(gm, gn, gk), in_specs=in_specs, out_specs=out_spec,
            out_shape=jax.ShapeDtypeStruct(out_shape, out_dtype), scratch_shapes=scratch,
            compiler_params=_params("parallel", "parallel", "arbitrary"),
        )(*operands)
    for e in comms:
        scratch += [pltpu.SemaphoreType.DMA(e.sems), pltpu.SemaphoreType.DMA(e.sems)]
    res = pl.pallas_call(
        body, name=name, grid=(gm, gn, gk), in_specs=in_specs + [HBM_SPEC] * n_cin, out_specs=[out_spec] + [HBM_SPEC] * n_cout,
        out_shape=[jax.ShapeDtypeStruct(out_shape, out_dtype)] + [s for e in comms for s in e.out_shapes],
        scratch_shapes=scratch,
        compiler_params=pltpu.CompilerParams(dimension_semantics=("arbitrary",) * 3, vmem_limit_bytes=VMEM_LIMIT_BYTES,
                                             has_side_effects=True),
    )(*operands, *[x for e in comms for x in e.inputs])
    at = 1
    for e in comms:
        e.results = list(res[at:at + len(e.out_shapes)])
        at += len(e.out_shapes)
    return res[0]


def _mm3(a, b, dims, name, nc=None, tm=512, rows3d=()):
    rows = a.shape[0]
    tm = min(tm, rows)
    gm = rows // tm

    def row_spec(width, three_d):
        if three_d:
            return pl.BlockSpec((tm, width // LANES, LANES), lambda c, i: (i, c, 0))
        return pl.BlockSpec((tm, width), lambda c, i: (i, c))

    def load(ref, three_d):
        if not three_d:
            return ref[...]
        return jnp.concatenate([ref[:, r, :] for r in range(ref.shape[1])], axis=1)

    if dims == TN:
        ka = a.shape[1] * (a.shape[2] if "a" in rows3d else 1) // nc
        nb = b.shape[1] * (b.shape[2] if "b" in rows3d else 1) // nc

        def body(a_ref, b_ref, o_ref):
            @pl.when(pl.program_id(1) == 0)
            def _():
                o_ref[...] = jnp.zeros_like(o_ref)
            o_ref[...] += _dot_rounded(load(a_ref, "a" in rows3d), load(b_ref, "b" in rows3d), TN)

        return pl.pallas_call(
            body, name=name, grid=(nc, gm),
            in_specs=[row_spec(ka, "a" in rows3d), row_spec(nb, "b" in rows3d)],
            out_specs=pl.BlockSpec((None, ka, nb), lambda c, i: (c, 0, 0)),
            out_shape=jax.ShapeDtypeStruct((nc, ka, nb), F32),
            compiler_params=_params("parallel", "arbitrary"),
        )(a, b)
    nc, ka, nb = b.shape
    wa, wo = (ka, nb) if dims == NN else (nb, ka)

    def body(a_ref, b_ref, o_ref):
        res = _dot_rounded(load(a_ref, "a" in rows3d), b_ref[...], dims)
        if "o" in rows3d:
            for r in range(wo // LANES):
                o_ref[:, r, :] = res[:, r * LANES:(r + 1) * LANES]
        else:
            o_ref[...] = res

    out_shape = (rows, nc * wo // LANES, LANES) if "o" in rows3d else (rows, nc * wo)
    return pl.pallas_call(
        body, name=name, grid=(nc, gm),
        in_specs=[row_spec(wa, "a" in rows3d), pl.BlockSpec((None, ka, nb), lambda c, i: (c, 0, 0))],
        out_specs=row_spec(wo, "o" in rows3d),
        out_shape=jax.ShapeDtypeStruct(out_shape, F32),
        compiler_params=_params("parallel", "parallel"),
    )(a, b)


def _rowwise(fn, name, rows, vecs=(), outs=(), sums=()):
    length = rows[0][0].shape[0]
    total = sum(w for _, w, _ in rows) + sum(w for w, _ in outs)
    tile = 8
    while tile * 2 <= min(length, 512) and tile * 2 * total <= ROW_TILE_ELEMS:
        tile *= 2
    assert length % tile == 0
    n_r, n_v, n_o = len(rows), len(vecs), len(outs)

    def body(*refs):
        vals = [r[...] for r in refs[:n_r + n_v]]
        o_refs = refs[n_r + n_v:n_r + n_v + n_o]
        s_refs = refs[n_r + n_v + n_o:]
        res_o, res_s = fn(*vals)
        for ref, val in zip(o_refs, res_o):
            ref[...] = val.astype(ref.dtype)
        if s_refs:
            @pl.when(pl.program_id(0) == 0)
            def _():
                for ref in s_refs:
                    ref[...] = jnp.zeros_like(ref)
            for ref, val in zip(s_refs, res_s):
                ref[...] += val

    def row_spec(w, cb):
        return pl.BlockSpec((tile, w), lambda i: (i, cb))

    res = pl.pallas_call(
        body, name=name, grid=(length // tile,),
        in_specs=[row_spec(w, cb) for _, w, cb in rows] + [pl.BlockSpec(v.shape, lambda i: (0, 0)) for v in vecs],
        out_specs=[row_spec(w, 0) for w, _ in outs] + [pl.BlockSpec((1, w), lambda i: (0, 0)) for w in sums],
        out_shape=[jax.ShapeDtypeStruct((length, w), dt) for w, dt in outs]
        + [jax.ShapeDtypeStruct((1, w), F32) for w in sums],
        compiler_params=_params("arbitrary" if sums else "parallel"),
    )(*[a for a, _, _ in rows], *vecs)
    return res[:n_o], res[n_o:]


def _elementwise(fn, name, arrays, out_dtypes):
    shape = arrays[0].shape
    cols = shape[-1]
    flat = [a.reshape(-1, cols) for a in arrays]
    rows = flat[0].shape[0]
    tile = 8
    while tile * 2 <= rows and rows % (tile * 2) == 0 and tile * 2 * cols * (len(arrays) + len(out_dtypes)) <= ROW_TILE_ELEMS:
        tile *= 2
    assert rows % tile == 0
    n_in = len(flat)

    def body(*refs):
        res = fn(*[r[...] for r in refs[:n_in]])
        for ref, val in zip(refs[n_in:], res):
            ref[...] = val.astype(ref.dtype)

    spec = pl.BlockSpec((tile, cols), lambda i: (i, 0))
    res = pl.pallas_call(
        body, name=name, grid=(rows // tile,), in_specs=[spec] * n_in, out_specs=[spec] * len(out_dtypes),
        out_shape=[jax.ShapeDtypeStruct((rows, cols), dt) for dt in out_dtypes],
        compiler_params=_params("parallel"),
    )(*flat)
    return [r.reshape(shape) for r in res]


def _whole(fn, name, arrays, out_shapes):
    n_in = len(arrays)

    def body(*refs):
        res = fn(*[r[...] for r in refs[:n_in]])
        for ref, val in zip(refs[n_in:], res):
            ref[...] = val

    return pl.pallas_call(
        body, name=name, out_shape=[jax.ShapeDtypeStruct(s, F32) for s in out_shapes],
        compiler_params=_params(),
    )(*arrays)


def _grad_tile(rows, cols, arrays_per_step):
    tile = 8
    while tile * 2 <= rows and rows % (tile * 2) == 0 and tile * 2 * cols * arrays_per_step <= ROW_TILE_ELEMS:
        tile *= 2
    return tile


def _chip_sum(partial, from_sibling, place):
    _, _, rows, cols = partial.shape
    tile = _grad_tile(rows, cols, 4)

    def body(place_ref, a_ref, b_ref, o16_ref, o32_ref):
        total = a_ref[...] + b_ref[...]
        o16_ref[...] = total.astype(BF16)

        @pl.when(pl.program_id(1) == place_ref[1])
        def _():
            o32_ref[...] = total

    return pl.pallas_call(
        body, name="chip_sum",
        grid_spec=pltpu.PrefetchScalarGridSpec(
            num_scalar_prefetch=1, grid=(rows // tile, 4),
            in_specs=[pl.BlockSpec((None, None, tile, cols), lambda i, q, p: (p[0], q, i, 0)),
                      pl.BlockSpec((None, tile, cols), lambda i, q, p: (q, i, 0))],
            out_specs=[pl.BlockSpec((None, tile, cols), lambda i, q, p: (q, i, 0)),
                       pl.BlockSpec((tile, cols), lambda i, q, p: (i, 0))]),
        out_shape=[jax.ShapeDtypeStruct((4, rows, cols), BF16), jax.ShapeDtypeStruct((rows, cols), F32)],
        compiler_params=_params("parallel", "arbitrary"),
    )(place, partial, from_sibling)


def _owner_sum(own, from_chips):
    rows, cols = own.shape
    tile = _grad_tile(rows, cols, 4)

    def body(a_ref, r0_ref, r1_ref, r2_ref, o_ref):
        o_ref[...] = a_ref[...] + r0_ref[...].astype(F32) + r1_ref[...].astype(F32) + r2_ref[...].astype(F32)

    spec = pl.BlockSpec((tile, cols), lambda i: (i, 0))
    return pl.pallas_call(
        body, name="owner_sum", grid=(rows // tile,),
        in_specs=[spec] + [pl.BlockSpec((None, tile, cols), functools.partial(lambda i, k: (k, i, 0), k=k)) for k in range(3)],
        out_specs=spec, out_shape=jax.ShapeDtypeStruct((rows, cols), F32), compiler_params=_params("parallel"),
    )(own, from_chips, from_chips, from_chips)


def _adamw_halves(w, m, v, mine, theirs, place, name):
    depth, _, rows, cols = w.shape
    tile = _grad_tile(rows, cols, 9)

    def body(place_ref, w_ref, m_ref, v_ref, *refs):
        g_refs, outs = refs[:2 * depth], refs[2 * depth:]
        layer, half = pl.program_id(0), pl.program_id(1)
        g = None
        for d in range(depth):
            gd = jnp.where(half == place_ref[0], g_refs[2 * d][...], g_refs[2 * d + 1][...])
            g = gd if g is None else jnp.where(layer == d, gd, g)
        delta, m_new, v_new = _adamw(w_ref[...], g, m_ref[...], v_ref[...])
        for ref, val in zip(outs, (g, delta, m_new, v_new)):
            ref[...] = val

    full = pl.BlockSpec((None, None, tile, cols), lambda l, h, i, p: (l, h, i, 0))
    g_specs = []
    for d in range(depth):
        g_specs += [pl.BlockSpec((tile, cols), functools.partial(lambda l, h, i, p, d: (jnp.where(l == d, i, 0), 0), d=d))] * 2
    operands = [x for pair in zip(mine, theirs) for x in pair]
    return pl.pallas_call(
        body, name=name,
        grid_spec=pltpu.PrefetchScalarGridSpec(
            num_scalar_prefetch=1, grid=(depth, 2, rows // tile),
            in_specs=[full] * 3 + g_specs, out_specs=[full] * 4),
        out_shape=[jax.ShapeDtypeStruct(w.shape, F32)] * 4,
        compiler_params=_params("arbitrary", "arbitrary", "arbitrary"),
    )(place, w, m, v, *operands)


def _allreduce_small(flat):
    rows = flat.shape[0] + (-flat.shape[0]) % (2 * SUBLANES)
    half = rows // 2
    padded = jnp.pad(flat, ((0, rows - flat.shape[0]), (0, 0)))
    swap, final = 0, 4

    def body(in_ref, out_ref, sibling_ref, chips_ref, total_ref, send_sems, recv_sems):
        x, y, c, chips = _place()
        mine, sibling = 2 * x + y, (x, y, 1 - c)
        my_rows = pl.ds(pl.multiple_of(c * half, SUBLANES), half)
        their_rows = pl.ds(pl.multiple_of((1 - c) * half, SUBLANES), half)
        cp = _remote(in_ref, sibling_ref, send_sems.at[swap], recv_sems.at[swap], sibling)
        cp.start()
        cp.wait()
        chips_ref[mine] = in_ref[my_rows, :] + sibling_ref[my_rows, :]
        sends = [_remote(chips_ref.at[mine], chips_ref.at[mine], send_sems.at[1 + k], recv_sems.at[1 + k], (cx, cy, c))
                 for k, (cx, cy) in enumerate(chips)]
        for cp in sends:
            cp.start()
        for k, (cx, cy) in enumerate(chips):
            slot = chips_ref.at[2 * cx + cy]
            _remote(slot, slot, send_sems.at[1 + k], recv_sems.at[1 + k], sibling).wait_recv()
        for cp in sends:
            cp.wait_send()
        total_ref[my_rows, :] = (chips_ref[0] + chips_ref[1]) + (chips_ref[2] + chips_ref[3])
        cp = _remote(total_ref.at[my_rows, :], total_ref.at[my_rows, :], send_sems.at[final], recv_sems.at[final], sibling)
        cp.start()
        _remote(total_ref.at[their_rows, :], total_ref.at[their_rows, :], send_sems.at[final], recv_sems.at[final], sibling).wait_recv()
        cp.wait_send()
        out_ref[...] = total_ref[...]

    out = pl.pallas_call(
        body, name="allreduce_small_grads",
        in_specs=[pl.BlockSpec(memory_space=pltpu.VMEM)], out_specs=pl.BlockSpec(memory_space=pltpu.VMEM),
        out_shape=jax.ShapeDtypeStruct((rows, LANES), F32),
        scratch_shapes=[pltpu.VMEM((rows, LANES), F32), pltpu.VMEM((4, half, LANES), F32), pltpu.VMEM((rows, LANES), F32),
                        pltpu.SemaphoreType.DMA((5,)), pltpu.SemaphoreType.DMA((5,))],
        compiler_params=pltpu.CompilerParams(has_side_effects=True, vmem_limit_bytes=VMEM_LIMIT_BYTES),
    )(padded)
    return out[:flat.shape[0]]


def _pool_tile(length):
    return min(256, length)


def _pool_fwd(proj, w_pool):
    length = proj.shape[0]
    ngroups, ch, _ = w_pool.shape
    width, tile = ngroups * ch, _pool_tile(length)

    def body(cur_ref, prev_ref, w_ref, pooled_ref, mixed_ref):
        i = pl.program_id(0)
        cur = cur_ref[...]
        tail = jnp.where(i > 0, prev_ref[tile - POOL_HALO:tile, :], 0.0)
        padded = jnp.concatenate([tail, cur], axis=0)
        pos = (lax.broadcasted_iota(jnp.int32, (tile, 1), 0) + i * tile + 1).astype(F32)
        for g, window in enumerate(POOL_WINDOWS):
            cols = slice(g * ch, (g + 1) * ch)
            run, shift = padded[:, cols], 1
            while shift < window:
                run = run + pltpu.roll(run, shift, 0)
                shift *= 2
            pooled = (run[POOL_HALO:, :] / jnp.minimum(pos, float(window)) - cur[:, cols]).astype(BF16)
            pooled_ref[:, cols] = pooled
            mixed_ref[:, cols] = _dot(pooled, w_ref[g])

    return pl.pallas_call(
        body, name="pool_fwd", grid=(length // tile,),
        in_specs=[pl.BlockSpec((tile, width), lambda i: (i, 0)),
                  pl.BlockSpec((tile, width), lambda i: (jnp.maximum(i - 1, 0), 0)),
                  pl.BlockSpec(w_pool.shape, lambda i: (0, 0, 0))],
        out_specs=[pl.BlockSpec((tile, width), lambda i: (i, 0))] * 2,
        out_shape=[jax.ShapeDtypeStruct((length, width), BF16), jax.ShapeDtypeStruct((length, width), F32)],
        compiler_params=_params("parallel"),
    )(proj, proj, w_pool)


def _pool_bwd_mix(d_mixed, pooled, w_pool):
    length, width = d_mixed.shape
    ngroups, ch, _ = w_pool.shape
    tile = _pool_tile(length)

    def body(dm_ref, pooled_ref, w_ref, dp_ref, dw_ref):
        @pl.when(pl.program_id(0) == 0)
        def _():
            dw_ref[...] = jnp.zeros_like(dw_ref)
        for g in range(ngroups):
            cols = slice(g * ch, (g + 1) * ch)
            dm = dm_ref[:, cols].astype(BF16)
            dp_ref[:, cols] = _dot(dm, w_ref[g], NT)
            dw_ref[g] += _dot(pooled_ref[:, cols], dm, TN)

    return pl.pallas_call(
        body, name="pool_bwd_mix", grid=(length // tile,),
        in_specs=[pl.BlockSpec((tile, width), lambda i: (i, 0)), pl.BlockSpec((tile, width), lambda i: (i, 0)),
                  pl.BlockSpec(w_pool.shape, lambda i: (0, 0, 0))],
        out_specs=[pl.BlockSpec((tile, width), lambda i: (i, 0)), pl.BlockSpec(w_pool.shape, lambda i: (0, 0, 0))],
        out_shape=[jax.ShapeDtypeStruct((length, width), F32), jax.ShapeDtypeStruct(w_pool.shape, F32)],
        compiler_params=_params("arbitrary"),
    )(d_mixed, pooled, w_pool)


def _pool_bwd_window(d_pooled, ngroups):
    length, width = d_pooled.shape
    ch, tile = width // ngroups, _pool_tile(length)
    last = length // tile - 1

    def body(cur_ref, next_ref, dx_ref):
        i = pl.program_id(0)
        cur = cur_ref[...]
        head = jnp.where(i < last, next_ref[0:POOL_HALO, :], 0.0)
        padded = jnp.concatenate([cur, head], axis=0)
        rows = tile + POOL_HALO
        pos = (lax.broadcasted_iota(jnp.int32, (rows, 1), 0) + i * tile + 1).astype(F32)
        for g, window in enumerate(POOL_WINDOWS):
            cols = slice(g * ch, (g + 1) * ch)
            run, shift = padded[:, cols] / jnp.minimum(pos, float(window)), 1
            while shift < window:
                run = run + pltpu.roll(run, rows - shift, 0)
                shift *= 2
            dx_ref[:, cols] = (run[0:tile, :] - cur[:, cols]).astype(BF16)

    return pl.pallas_call(
        body, name="pool_bwd_window", grid=(length // tile,),
        in_specs=[pl.BlockSpec((tile, width), lambda i: (i, 0)),
                  pl.BlockSpec((tile, width), lambda i: (jnp.minimum(i + 1, last), 0))],
        out_specs=pl.BlockSpec((tile, width), lambda i: (i, 0)),
        out_shape=jax.ShapeDtypeStruct((length, width), BF16),
        compiler_params=_params("parallel"),
    )(d_pooled, d_pooled)


ATTN_TILE = 256
LOG_WEIGHT_FLOOR = -110.0


def _walk_back(n_chunks, chunk, carry):
    def cond(state):
        return jnp.logical_and(state[0] < n_chunks, jnp.max(state[1]) > LOG_WEIGHT_FLOOR)

    def step(state):
        return (state[0] + 1,) + tuple(chunk(n_chunks - 1 - state[0], tuple(state[1:])))

    return lax.while_loop(cond, step, (jnp.int32(0),) + tuple(carry))[1:]


def _stick_weights(q, kc, upper, run_log, mask):
    z = _dot(q, kc, NT)
    e = jnp.exp(-jnp.abs(z))
    softplus = jnp.maximum(z, 0.0) + jnp.log(1.0 + e)
    log_sig = z - softplus
    log_1m = -softplus if mask is None else jnp.where(mask, -softplus, 0.0)
    suffix = _dot(log_1m.astype(BF16), upper) + run_log
    w = jnp.exp(log_sig + suffix)
    if mask is not None:
        w = jnp.where(mask, w, 0.0)
    return w, log_sig, suffix[:, 0:1] + log_1m[:, 0:1]


def _attn_consts(tile):
    jj = lax.broadcasted_iota(jnp.int32, (tile, tile), 0)
    ss = lax.broadcasted_iota(jnp.int32, (tile, tile), 1)
    return (jj > ss).astype(BF16), (jj >= ss).astype(BF16), ss < jj


HEADS_PER_STEP = 2


def _heads_per_step(n_heads, *blocks):
    ok = n_heads % HEADS_PER_STEP == 0 and all(b % HEADS_PER_STEP == 0 for b in blocks)
    return HEADS_PER_STEP if ok else 1


def _slowest(run_logs):
    out = run_logs[0]
    for r in run_logs[1:]:
        out = jnp.maximum(out, r)
    return out


def _attn_fwd(proj, n_heads, q_blk, k_blk, v_blk):
    length = proj.shape[0]
    tile = min(ATTN_TILE, length)
    scale = HEAD_DIM ** -0.5

    hps = _heads_per_step(n_heads, q_blk, k_blk, v_blk)
    width = hps * HEAD_DIM

    def body(q_ref, k_ref, v_ref, o_ref):
        i = pl.program_id(1)
        cols = [slice(h * HEAD_DIM, (h + 1) * HEAD_DIM) for h in range(hps)]
        qs = [(q_ref[:, c] * scale).astype(BF16) for c in cols]
        upper, _, diag_mask = _attn_consts(tile)

        def chunk(j, carry, mask):
            start = pl.multiple_of(j * tile, tile)
            new = []
            for h, c in enumerate(cols):
                run_log, acc = carry[1 + 2 * h], carry[2 + 2 * h]
                kc = k_ref[pl.ds(start, tile), c].astype(BF16)
                vc = v_ref[pl.ds(start, tile), c].astype(BF16)
                w, _, run_log = _stick_weights(qs[h], kc, upper, run_log, mask)
                new += [run_log, acc + _dot(w.astype(BF16), vc)]
            return (_slowest(new[0::2]),) + tuple(new)

        zero = jnp.zeros((tile, 1), F32)
        carry = (zero,) + (zero, jnp.zeros((tile, HEAD_DIM), F32)) * hps
        carry = chunk(i, carry, diag_mask)
        carry = _walk_back(i, lambda j, cr: chunk(j, cr, None), carry)
        for h, c in enumerate(cols):
            o_ref[:, c] = carry[2 + 2 * h]

    return pl.pallas_call(
        body, name="attn_fwd", grid=(n_heads // hps, length // tile),
        in_specs=[pl.BlockSpec((tile, width), lambda h, i: (i, q_blk // hps + h)),
                  pl.BlockSpec((length, width), lambda h, i: (0, k_blk // hps + h)),
                  pl.BlockSpec((length, width), lambda h, i: (0, v_blk // hps + h))],
        out_specs=pl.BlockSpec((tile, width), lambda h, i: (i, h)),
        out_shape=jax.ShapeDtypeStruct((length, n_heads * HEAD_DIM), F32),
        compiler_params=_params("parallel", "parallel"),
    )(proj, proj, proj)


def _attn_bwd(proj, out, d_out, n_heads, q_blk, k_blk, v_blk):
    length = proj.shape[0]
    tile = min(ATTN_TILE, length)
    scale = HEAD_DIM ** -0.5

    hps = _heads_per_step(n_heads, q_blk, k_blk, v_blk)
    step_width = hps * HEAD_DIM

    def body(q_ref, k_ref, v_ref, o_ref, do_ref, dq_ref, dk_ref, dv_ref):
        i = pl.program_id(1)

        @pl.when(i == 0)
        def _():
            dk_ref[...] = jnp.zeros_like(dk_ref)
            dv_ref[...] = jnp.zeros_like(dv_ref)

        cols = [slice(h * HEAD_DIM, (h + 1) * HEAD_DIM) for h in range(hps)]
        qs = [(q_ref[:, c] * scale).astype(BF16) for c in cols]
        dos = [do_ref[:, c].astype(BF16) for c in cols]
        totals = [jnp.sum(do.astype(F32) * o_ref[:, c], axis=1, keepdims=True) for do, c in zip(dos, cols)]
        upper, upper_incl, diag_mask = _attn_consts(tile)

        def chunk(j, carry, mask):
            start = pl.multiple_of(j * tile, tile)
            new = []
            for h, c in enumerate(cols):
                run_log, run_g, dq = carry[1 + 3 * h:4 + 3 * h]
                q, do = qs[h], dos[h]
                kc = k_ref[pl.ds(start, tile), c].astype(BF16)
                vc = v_ref[pl.ds(start, tile), c].astype(BF16)
                w, log_sig, run_log = _stick_weights(q, kc, upper, run_log, mask)
                wb = w.astype(BF16)
                g = wb.astype(F32) * _dot(do, vc, NT)
                g_hi, g_lo = _split(g)
                g_suffix = _dot(g_hi, upper_incl) + _dot(g_lo, upper_incl) + run_g
                dz = g - jnp.exp(log_sig) * (g + (totals[h] - g_suffix))
                if mask is not None:
                    dz = jnp.where(mask, dz, 0.0)
                dzb = dz.astype(BF16)
                dk_ref[pl.ds(start, tile), c] += _dot(dzb, q, TN)
                dv_ref[pl.ds(start, tile), c] += _dot(wb, do, TN)
                new += [run_log, g_suffix[:, 0:1], dq + _dot(dzb, kc)]
            return (_slowest(new[0::3]),) + tuple(new)

        zero = jnp.zeros((tile, 1), F32)
        carry = (zero,) + (zero, zero, jnp.zeros((tile, HEAD_DIM), F32)) * hps
        carry = chunk(i, carry, diag_mask)
        carry = _walk_back(i, lambda j, cr: chunk(j, cr, None), carry)
        for h, c in enumerate(cols):
            dq_ref[:, c] = (carry[3 + 3 * h] * scale).astype(BF16)

    width = n_heads * HEAD_DIM
    tile_spec = pl.BlockSpec((tile, step_width), lambda h, i: (i, h))
    head_spec = pl.BlockSpec((length, step_width), lambda h, i: (0, h))
    return pl.pallas_call(
        body, name="attn_bwd", grid=(n_heads // hps, length // tile),
        in_specs=[pl.BlockSpec((tile, step_width), lambda h, i: (i, q_blk // hps + h)),
                  pl.BlockSpec((length, step_width), lambda h, i: (0, k_blk // hps + h)),
                  pl.BlockSpec((length, step_width), lambda h, i: (0, v_blk // hps + h)),
                  tile_spec, tile_spec],
        out_specs=[tile_spec, head_spec, head_spec],
        out_shape=[jax.ShapeDtypeStruct((length, width), BF16), jax.ShapeDtypeStruct((length, width), F32),
                   jax.ShapeDtypeStruct((length, width), F32)],
        compiler_params=_params("parallel", "arbitrary"),
    )(proj, proj, proj, out, d_out)


SCAN_CHUNK = 128


def _disc_lam(lam_re, lam_im, log_dt):
    dt = jnp.exp(log_dt)
    mag, phase = jnp.exp(lam_re * dt), lam_im * dt
    bar_re, bar_im = mag * jnp.cos(phase), mag * jnp.sin(phase)
    num_re, den = bar_re - 1.0, lam_re * lam_re + lam_im * lam_im
    return (bar_re, bar_im, (num_re * lam_re + bar_im * lam_im) / den, (bar_im * lam_re - num_re * lam_im) / den)


def _disc_b(cf_re, cf_im, b_re, b_im):
    return cf_re * b_re - cf_im * b_im, cf_re * b_im + cf_im * b_re


SCAN_UNROLL = 8
SUBLANES = 8


def _state_rows(per_group):
    return per_group.reshape(-1, SUBLANES, LANES)


def _swap_parts(x):
    pieces = []
    for k in range(x.shape[0] // (2 * SUBLANES)):
        base = 2 * SUBLANES * k
        pieces += [x[base + SUBLANES:base + 2 * SUBLANES], x[base:base + SUBLANES]]
    return jnp.concatenate(pieces, axis=0)


def _scan_coeffs(re_rows, im_rows, conj):
    same, cross = [], []
    for k in range(re_rows.shape[0]):
        same += [re_rows[k], re_rows[k]]
        cross += [im_rows[k], -im_rows[k]] if conj else [-im_rows[k], im_rows[k]]
    return jnp.concatenate(same, axis=0), jnp.concatenate(cross, axis=0)


def _scan_fwd(bu, bar_re, bar_im):
    length, groups, width = bu.shape
    chunk = min(SCAN_CHUNK, length)

    def body(bu_ref, re_ref, im_ref, st_ref, carry):
        @pl.when(pl.program_id(0) == 0)
        def _():
            carry[...] = jnp.zeros_like(carry)
        a_same, a_cross = _scan_coeffs(re_ref[...], im_ref[...], conj=False)

        def step(blk, x):
            for r in range(SCAN_UNROLL):
                t = blk * SCAN_UNROLL + r
                x = a_same * x + a_cross * _swap_parts(x) + bu_ref[t]
                st_ref[t] = x
            return x

        carry[...] = lax.fori_loop(0, chunk // SCAN_UNROLL, step, carry[...])

    blk = pl.BlockSpec((chunk, groups, width), lambda i: (i, 0, 0))
    par = pl.BlockSpec(bar_re.shape, lambda i: (0, 0, 0))
    return pl.pallas_call(
        body, name="s5_scan_fwd", grid=(length // chunk,), in_specs=[blk, par, par], out_specs=blk,
        out_shape=jax.ShapeDtypeStruct(bu.shape, F32), scratch_shapes=[pltpu.VMEM((groups, width), F32)],
        compiler_params=_params("arbitrary"),
    )(bu, bar_re, bar_im)


def _scan_bwd(d_states, states, bar_re, bar_im):
    length, groups, width = states.shape
    chunk = min(SCAN_CHUNK, length)
    last = length // chunk - 1

    def body(g_ref, st_ref, re_ref, im_ref, out_ref, same_ref, swap_ref, carry):
        @pl.when(pl.program_id(0) == 0)
        def _():
            carry[...] = jnp.zeros_like(carry)
            same_ref[...] = jnp.zeros_like(same_ref)
            swap_ref[...] = jnp.zeros_like(swap_ref)
        a_same, a_cross = _scan_coeffs(re_ref[...], im_ref[...], conj=True)

        def step(blk, cr):
            adj, acc_same, acc_swap = cr
            for r in range(SCAN_UNROLL):
                t = chunk - 1 - (blk * SCAN_UNROLL + r)
                s = st_ref[t]
                acc_same = acc_same + adj * s
                acc_swap = acc_swap + adj * _swap_parts(s)
                adj = g_ref[t] + a_same * adj + a_cross * _swap_parts(adj)
                out_ref[t] = adj
            return adj, acc_same, acc_swap

        adj, acc_same, acc_swap = lax.fori_loop(0, chunk // SCAN_UNROLL, step, (carry[...], same_ref[...], swap_ref[...]))
        carry[...] = adj
        same_ref[...] = acc_same
        swap_ref[...] = acc_swap

    blk = pl.BlockSpec((chunk, groups, width), lambda i: (last - i, 0, 0))
    par = pl.BlockSpec(bar_re.shape, lambda i: (0, 0, 0))
    acc = pl.BlockSpec((groups, width), lambda i: (0, 0))
    return pl.pallas_call(
        body, name="s5_scan_bwd", grid=(length // chunk,), in_specs=[blk, blk, par, par], out_specs=[blk, acc, acc],
        out_shape=[jax.ShapeDtypeStruct(states.shape, F32), jax.ShapeDtypeStruct((groups, width), F32),
                   jax.ShapeDtypeStruct((groups, width), F32)],
        scratch_shapes=[pltpu.VMEM((groups, width), F32)],
        compiler_params=_params("arbitrary"),
    )(d_states, states, bar_re, bar_im)


def _lam_bar_grad(acc_same, acc_swap):
    def fn(same, swap):
        g_re, g_im = [], []
        for k in range(same.shape[0] // (2 * SUBLANES)):
            re, im = slice(2 * SUBLANES * k, 2 * SUBLANES * k + SUBLANES), slice(2 * SUBLANES * k + SUBLANES, 2 * SUBLANES * (k + 1))
            g_re.append(same[re] + same[im])
            g_im.append(swap[im] - swap[re])
        return jnp.concatenate(g_re, axis=0), jnp.concatenate(g_im, axis=0)
    return _whole(fn, "s5_lam_bar_grad", [acc_same, acc_swap], [(acc_same.shape[0] // 2, LANES)] * 2)


def _block_diag(per_group):
    groups, a, b = per_group.shape
    nc = groups // GROUPS_PER_CHUNK
    eye = jnp.eye(GROUPS_PER_CHUNK, dtype=per_group.dtype)
    x = per_group.reshape(nc, GROUPS_PER_CHUNK, a, 1, b) * eye[None, :, None, :, None]
    return x.reshape(nc, GROUPS_PER_CHUNK * a, GROUPS_PER_CHUNK * b)


def _block_diag_part(chunks, a, b):
    nc = chunks.shape[0]
    x = chunks.reshape(nc, GROUPS_PER_CHUNK, a, GROUPS_PER_CHUNK, b)
    x = jnp.stack([x[:, g, :, g, :] for g in range(GROUPS_PER_CHUNK)], axis=1)
    return x.reshape(nc * GROUPS_PER_CHUNK, a, b)


def _epilogue(raw, gate, scale, g):
    return _rms(raw * scale, g) * (gate * _sigmoid(gate))


def _ssm_mid(y, u, d_skip):
    return _gelu(y + d_skip * u)


def _adamw(w, g, m, v):
    m = ADAM_B1 * m + (1.0 - ADAM_B1) * g
    v = ADAM_B2 * v + (1.0 - ADAM_B2) * (g * g)
    m_hat = m / (1.0 - ADAM_B1 ** ADAM_STEP)
    v_hat = v / (1.0 - ADAM_B2 ** ADAM_STEP)
    return -ADAM_LR * (m_hat / (jnp.sqrt(v_hat) + ADAM_EPS) + ADAM_WD * w), m, v


class _Dims:
    def __init__(self, d_model, length):
        self.d, self.length = d_model, length
        self.d_pool, self.d_attn = d_model // 4, d_model // 2
        self.d_ssm = d_model - self.d_pool - self.d_attn
        self.heads = self.d_attn // HEAD_DIM
        self.groups = self.d_ssm // SSM_GROUP
        self.d_in = 2 * self.d_pool + 4 * self.d_attn + 2 * self.d_ssm
        sizes = (self.d_pool, self.d_pool, self.d_attn, self.d_attn, self.d_attn, self.d_attn, self.d_ssm, self.d_ssm)
        offs = [0]
        for s in sizes[:-1]:
            offs.append(offs[-1] + s)
        (self.o_px, self.o_pgate, self.o_q, self.o_k, self.o_v, self.o_agate, self.o_u, self.o_sgate) = offs


def _ssm_operands(dm, p):
    groups, states = dm.groups, SSM_STATE
    bar_re, bar_im, cf_re, cf_im = _whole(_disc_lam, "s5_disc_lam", [p["lam_re"], p["lam_im"], p["log_dt"].reshape(groups, 1)],
                                          [(groups, states)] * 4)
    b_re2, b_im2 = p["b_re"].reshape(groups * states, SSM_GROUP), p["b_im"].reshape(groups * states, SSM_GROUP)
    bb_re, bb_im = _whole(_disc_b, "s5_disc_b", [cf_re.reshape(-1, 1), cf_im.reshape(-1, 1), b_re2, b_im2],
                          [(groups * states, SSM_GROUP)] * 2)
    per_group = lambda a: jnp.swapaxes(a.reshape(groups, states, SSM_GROUP), 1, 2)
    b_blk = jnp.concatenate([_block_diag(per_group(bb_re)), _block_diag(per_group(bb_im))], axis=2)
    c_blk = jnp.concatenate([_block_diag(jnp.swapaxes(p["c_re"], 1, 2)), _block_diag(jnp.swapaxes(-p["c_im"], 1, 2))], axis=1)
    return dict(bar_re=bar_re, bar_im=bar_im, bar_re_rows=_state_rows(bar_re), bar_im_rows=_state_rows(bar_im),
                cf_re=cf_re, cf_im=cf_im, b_re2=b_re2, b_im2=b_im2, b_blk=b_blk, c_blk=c_blk)


def _mm_carrying(carry, key, *args, **kwargs):
    stages = carry.get(key, [])
    exchanges = [make() for make, _ in stages]
    out = _mm(*args, comm=exchanges, **kwargs)
    for (_, deliver), exchange in zip(stages, exchanges):
        deliver(exchange.results)
    return out


def _layer_fwd(dm, x_in, p, gw, carry):
    length = dm.length
    blk = lambda off, w: off // w
    (h,), _ = _rowwise(lambda x, g: ((_rms(x, g),), ()), "rms_fwd", [(x_in, dm.d, 0)], [p["ln_g"]], [(dm.d, BF16)])
    proj = _mm_carrying(carry, "in_proj", h, gw["w_in"], NN, F32, "in_proj", n=dm.d_in, b_quarters="n", tn=WIDE_TILE)
    pooled, mixed = _pool_fwd(proj, gw["w_pool"])
    qb, kb, vb = dm.o_q // HEAD_DIM, dm.o_k // HEAD_DIM, dm.o_v // HEAD_DIM
    attn = _attn_fwd(proj, dm.heads, qb, kb, vb)
    so = _ssm_operands(dm, p)
    u_row = (proj, dm.d_ssm, blk(dm.o_u, dm.d_ssm))
    (u,), _ = _rowwise(lambda v: ((v,), ()), "take_u", [u_row], [], [(dm.d_ssm, F32)])
    bu = _mm3(u, so["b_blk"], NN, "s5_bu", rows3d="o")
    states = _scan_fwd(bu, so["bar_re_rows"], so["bar_im_rows"])
    y = _mm3(states, so["c_blk"], NN, "s5_y", rows3d="a")
    (hg,), _ = _rowwise(lambda yy, uu, dsk: ((_ssm_mid(yy, uu, dsk),), ()), "s5_mid_fwd",
                        [(y, dm.d_ssm, 0), (u, dm.d_ssm, 0)], [p["d_skip"]], [(dm.d_ssm, BF16)])
    z = _mm(hg, gw["w_glu"], NN, F32, "glu_proj", n=2 * dm.d_ssm, b_quarters="n")

    def glu(zz, bias):
        zz = zz + bias
        return (zz[:, :dm.d_ssm] * _sigmoid(zz[:, dm.d_ssm:]),), ()

    (ssm,), _ = _rowwise(glu, "glu_fwd", [(z, 2 * dm.d_ssm, 0)], [p["b_glu"]], [(dm.d_ssm, F32)])
    g_pool, g_attn, g_ssm = (p["branch_g"][:, :dm.d_pool], p["branch_g"][:, dm.d_pool:dm.d_pool + dm.d_attn],
                             p["branch_g"][:, dm.d_pool + dm.d_attn:])
    ones_attn, ones_ssm = jnp.ones((1, dm.d_attn), F32), jnp.ones((1, dm.d_ssm), F32)
    epi = lambda raw, gate, scale, g: ((_epilogue(raw, gate, scale, g),), ())
    branches = [("pool", mixed, dm.d_pool, dm.o_pgate, p["pool_scale"], g_pool),
                ("attn", attn, dm.d_attn, dm.o_agate, ones_attn, g_attn),
                ("ssm", ssm, dm.d_ssm, dm.o_sgate, ones_ssm, g_ssm)]
    ys = []
    for nm, raw, w, off, scale, g in branches:
        (yb,), _ = _rowwise(epi, "epilogue_fwd_" + nm, [(raw, w, 0), (proj, w, blk(off, w))], [scale, g], [(w, BF16)])
        ys.append(yb)
    y_cat = jnp.concatenate(ys, axis=1)
    x_out = _mm_carrying(carry, "out_proj", y_cat, gw["w_out"], NN, F32, "out_proj", add=x_in, tn=WIDE_TILE)
    saved = dict(x_in=x_in, h=h, proj=proj, pooled=pooled, mixed=mixed, attn=attn, so=so, u=u, states=states,
                 y=y, hg=hg, z=z, ssm=ssm, y_cat=y_cat, scales=(p["pool_scale"], ones_attn, ones_ssm), gs=(g_pool, g_attn, g_ssm))
    return x_out, saved


def _layer_bwd(dm, d_out, d_out_bf, p, gw, sv, want_bf, carry, big):
    length = dm.length
    blk = lambda off, w: off // w
    proj = sv["proj"]
    d_y = _mm_carrying(carry, "out_proj_dgrad", d_out_bf, gw["w_out"], NT, F32, "out_proj_dgrad", tn=WIDE_TILE)
    big["w_out"] = _mm(sv["y_cat"], d_out_bf, TN, F32, "out_proj_wgrad", out_quarters="rows", tn=WIDE_TILE)

    def epi_bwd(nseg):
        def fn(*vals):
            dys, (raw, gate, scale, g) = vals[:nseg], vals[nseg:]
            dyb = dys[0] if nseg == 1 else jnp.concatenate(dys, axis=1)
            _, vjp = jax.vjp(_epilogue, raw, gate, scale, g)
            d_raw, d_gate, d_scale, d_g = vjp(dyb)
            return (d_raw, d_gate), (d_scale, d_g)
        return fn

    branch = [("pool", sv["mixed"], dm.d_pool, dm.o_pgate, 0), ("attn", sv["attn"], dm.d_attn, dm.o_agate, dm.d_pool),
              ("ssm", sv["ssm"], dm.d_ssm, dm.o_sgate, dm.d_pool + dm.d_attn)]
    d_raws, d_gates, d_scales, d_gs = [], [], [], []
    for (nm, raw, w, off, yoff), scale, g in zip(branch, sv["scales"], sv["gs"]):
        seg = math.gcd(w, yoff) if yoff else w
        dy_rows = [(d_y, seg, yoff // seg + s) for s in range(w // seg)]
        (d_raw, d_gate), (d_scale, d_g) = _rowwise(
            epi_bwd(len(dy_rows)), "epilogue_bwd_" + nm, dy_rows + [(raw, w, 0), (proj, w, blk(off, w))], [scale, g],
            [(w, F32), (w, BF16)], [w, w])
        d_raws.append(d_raw); d_gates.append(d_gate); d_scales.append(d_scale); d_gs.append(d_g)
    d_pooled, g_w_pool = _pool_bwd_mix(d_raws[0], sv["pooled"], gw["w_pool"])
    ngr, ch = gw["w_pool"].shape[0], gw["w_pool"].shape[1]
    q_rows = ch // 4
    big["w_pool"] = g_w_pool.reshape(ngr, 4, 2, q_rows // 2, ch).transpose(2, 1, 0, 3, 4).reshape(2, 4, ngr * q_rows // 2, ch)
    d_px = _pool_bwd_window(d_pooled, len(POOL_WINDOWS))
    qb, kb, vb = dm.o_q // HEAD_DIM, dm.o_k // HEAD_DIM, dm.o_v // HEAD_DIM
    d_q, d_k, d_v = _attn_bwd(proj, sv["attn"], d_raws[1], dm.heads, qb, kb, vb)
    so = sv["so"]

    def glu_bwd(d_ssm, zz, bias):
        zz = zz + bias
        val, sg = zz[:, :dm.d_ssm], _sigmoid(zz[:, dm.d_ssm:])
        dz = jnp.concatenate([d_ssm * sg, d_ssm * val * sg * (1.0 - sg)], axis=1)
        return (dz,), (jnp.sum(dz, axis=0, keepdims=True),)

    (d_z,), (g_b_glu,) = _rowwise(glu_bwd, "glu_bwd", [(d_raws[2], dm.d_ssm, 0), (sv["z"], 2 * dm.d_ssm, 0)], [p["b_glu"]],
                                  [(2 * dm.d_ssm, BF16)], [2 * dm.d_ssm])
    d_hg = _mm(d_z, gw["w_glu"], NT, F32, "glu_dgrad", n=dm.d_ssm, b_quarters="k")
    big["w_glu"] = _mm(sv["hg"], d_z, TN, F32, "glu_wgrad", out_quarters="cols")

    def mid_bwd(dh, yy, uu, dsk):
        _, vjp = jax.vjp(_ssm_mid, yy, uu, dsk)
        dy_, du_, ddsk = vjp(dh)
        return (dy_, du_), (ddsk,)

    (d_yssm, d_u_direct), (g_d_skip,) = _rowwise(mid_bwd, "s5_mid_bwd", [(d_hg, dm.d_ssm, 0), (sv["y"], dm.d_ssm, 0), (sv["u"], dm.d_ssm, 0)],
                                                 [p["d_skip"]], [(dm.d_ssm, F32), (dm.d_ssm, F32)], [dm.d_ssm])
    d_states = _mm3(d_yssm, so["c_blk"], NT, "s5_y_dgrad", rows3d="o")
    d_c_blk = _mm3(sv["states"], d_yssm, TN, "s5_y_wgrad", nc=so["c_blk"].shape[0], rows3d="a")
    d_bu, acc_same, acc_swap = _scan_bwd(d_states, sv["states"], so["bar_re_rows"], so["bar_im_rows"])
    d_u_scan = _mm3(d_bu, so["b_blk"], NT, "s5_bu_dgrad", rows3d="a")
    d_b_blk = _mm3(sv["u"], d_bu, TN, "s5_bu_wgrad", nc=so["b_blk"].shape[0], rows3d="b")
    (d_u,) = _elementwise(lambda a, b: (a + b,), "s5_du", [d_u_direct, d_u_scan], [BF16])
    groups, states = dm.groups, SSM_STATE
    part = GROUPS_PER_CHUNK * states
    g_c_re = jnp.swapaxes(_block_diag_part(d_c_blk[:, :part], states, SSM_GROUP), 1, 2)
    g_c_im_neg = jnp.swapaxes(_block_diag_part(d_c_blk[:, part:], states, SSM_GROUP), 1, 2)
    d_bb_re = jnp.swapaxes(_block_diag_part(d_b_blk[:, :, :part], SSM_GROUP, states), 1, 2).reshape(-1, SSM_GROUP)
    d_bb_im = jnp.swapaxes(_block_diag_part(d_b_blk[:, :, part:], SSM_GROUP, states), 1, 2).reshape(-1, SSM_GROUP)

    def disc_b_bwd(cf_re, cf_im, b_re, b_im, g_re, g_im):
        _, vjp = jax.vjp(_disc_b, cf_re, cf_im, b_re, b_im)
        return vjp((g_re, g_im))

    d_cf_re, d_cf_im, g_b_re, g_b_im = _whole(
        disc_b_bwd, "s5_disc_b_bwd", [so["cf_re"].reshape(-1, 1), so["cf_im"].reshape(-1, 1), so["b_re2"], so["b_im2"], d_bb_re, d_bb_im],
        [(groups * states, 1)] * 2 + [(groups * states, SSM_GROUP)] * 2)

    def disc_lam_bwd(lam_re, lam_im, log_dt, g_bar_re, g_bar_im, g_cf_re, g_cf_im):
        _, vjp = jax.vjp(_disc_lam, lam_re, lam_im, log_dt)
        return vjp((g_bar_re, g_bar_im, g_cf_re, g_cf_im))

    g_bar_re, g_bar_im = _lam_bar_grad(acc_same, acc_swap)

    g_lam_re, g_lam_im, g_log_dt = _whole(
        disc_lam_bwd, "s5_disc_lam_bwd", [p["lam_re"], p["lam_im"], p["log_dt"].reshape(groups, 1), g_bar_re.reshape(groups, states), g_bar_im.reshape(groups, states),
                                          d_cf_re.reshape(groups, states), d_cf_im.reshape(groups, states)],
        [(groups, states)] * 2 + [(groups, 1)])
    (g_c_im,) = _elementwise(lambda a: (-a,), "s5_neg_c_im", [g_c_im_neg.reshape(groups * SSM_GROUP, states)], [F32])
    d_proj = jnp.concatenate([d_px, d_gates[0], d_q, d_k.astype(BF16), d_v.astype(BF16), d_gates[1], d_u, d_gates[2]], axis=1)
    big["w_in"] = _mm_carrying(carry, "in_proj_wgrad", sv["h"], d_proj, TN, F32, "in_proj_wgrad", out_quarters="cols", tn=WIDE_TILE)
    d_h = _mm_carrying(carry, "in_proj_dgrad", d_proj, gw["w_in"], NT, F32, "in_proj_dgrad", n=dm.d, b_quarters="k", tm=WIDE_TILE)

    def rms_bwd(dh, xx, dres, g):
        _, vjp = jax.vjp(_rms, xx, g)
        dx, dg = vjp(dh)
        dx = dx + dres
        return ((dx, dx) if want_bf else (dx,)), (dg,)

    d_xs, (g_ln_g,) = _rowwise(rms_bwd, "rms_bwd", [(d_h, dm.d, 0), (sv["x_in"], dm.d, 0), (d_out, dm.d, 0)], [p["ln_g"]],
                               [(dm.d, F32), (dm.d, BF16)] if want_bf else [(dm.d, F32)], [dm.d])
    small = dict(ln_g=g_ln_g, pool_scale=d_scales[0], lam_re=g_lam_re, lam_im=g_lam_im, log_dt=g_log_dt.reshape(1, groups),
                 b_re=g_b_re, b_im=g_b_im, c_re=g_c_re, c_im=g_c_im, d_skip=g_d_skip, b_glu=g_b_glu,
                 branch_g=jnp.concatenate(d_gs, axis=1))
    return d_xs[0], (d_xs[1] if want_bf else None), small


SMALL_ROWS = 8


def _pack(arrays):
    parts = []
    for a in arrays:
        flat = a.reshape(-1)
        pad = (-flat.shape[0]) % (SMALL_ROWS * LANES)
        parts.append(jnp.pad(flat, (0, pad)).reshape(-1, LANES))
    return jnp.concatenate(parts, axis=0)


def _unpack(buf, like):
    res, row = [], 0
    for a in like:
        size = math.prod(a.shape)
        rows = -(-size // (SMALL_ROWS * LANES)) * SMALL_ROWS
        res.append(buf[row:row + rows].reshape(-1)[:size].reshape(a.shape))
        row += rows
    return res


def kernel(x, ln_g, w_in, w_pool, pool_scale, lam_re, lam_im, log_dt, b_re, b_im, c_re, c_im, d_skip, w_glu, b_glu, branch_g, w_out, final_g, loss_target, m_ln_g, m_w_in, m_w_pool, m_pool_scale, m_lam_re, m_lam_im, m_log_dt, m_b_re, m_b_im, m_c_re, m_c_im, m_d_skip, m_w_glu, m_b_glu, m_branch_g, m_w_out, m_final_g, v_ln_g, v_w_in, v_w_pool, v_pool_scale, v_lam_re, v_lam_im, v_log_dt, v_b_re, v_b_im, v_c_re, v_c_im, v_d_skip, v_w_glu, v_b_glu, v_branch_g, v_w_out, v_final_g):
    given = dict(locals())
    weights = {n: given[n] for n in WEIGHTS}
    depth = ln_g.shape[0]
    _, length, d_model = x.shape
    dm = _Dims(d_model, length)
    x0, target = x[0], loss_target[0]
    c_idx = lax.axis_index("c")
    my_quarter = 2 * lax.axis_index("x") + lax.axis_index("y")

    shard2d = {(n, l): weights[n][l].reshape(-1, weights[n].shape[-1]) for l in range(depth) for n in SHARDED}
    keys = list(shard2d)
    halves16 = {k: w.astype(BF16).reshape(2, w.shape[0] // 2, -1) for k, w in shard2d.items()}
    gw = [dict() for _ in range(depth)]

    def gather(group, name):
        return _allgather_quarters([halves16[k] for k in group], name)

    def deliver_weights(group):
        def deliver(results):
            for (n, l), g in zip(group, results):
                rows, cols = shard2d[(n, l)].shape
                g = g.reshape(4, rows, cols)
                if n == "w_out":
                    g = g.reshape(4 * rows, cols)
                if n == "w_pool":
                    ngr = w_pool.shape[1]
                    g = g.reshape(4, ngr, rows // ngr, cols).transpose(1, 0, 2, 3).reshape(ngr, 4 * rows // ngr, cols)
                gw[l][n] = g
        return deliver

    first = [("w_in", 0), ("w_pool", 0), ("w_out", 0)]
    behind_in_proj = [("w_glu", 0)] + ([("w_in", 1)] if depth > 1 else [])
    behind_out_proj = [k for k in keys if k not in first + behind_in_proj]
    deliver_weights(first)(gather(first, "allgather_first").run())
    fwd_carry = [dict() for _ in range(depth)]
    fwd_carry[0]["in_proj"] = [(lambda: gather(behind_in_proj, "allgather_behind_in_proj"), deliver_weights(behind_in_proj))]
    if behind_out_proj:
        fwd_carry[0]["out_proj"] = [(lambda: gather(behind_out_proj, "allgather_behind_out_proj"), deliver_weights(behind_out_proj))]
    small_names = [n for n in WEIGHTS if n not in SHARDED and n != "final_g"]
    ps = [{n: (weights[n][l].reshape(1, -1) if weights[n][l].ndim == 1 else weights[n][l]) for n in small_names} for l in range(depth)]

    acts, saved = x0, []
    for l in range(depth):
        acts, sv = _layer_fwd(dm, acts, ps[l], gw[l], fwd_carry[l])
        saved.append(sv)

    def final(xx, tt, g):
        def loss_fn(xv, gv):
            err = _rms(xv, gv) - tt
            return 0.5 * jnp.sum(jnp.mean(err * err, axis=-1))
        val, (dx, dg) = jax.value_and_grad(loss_fn, argnums=(0, 1))(xx, g)
        return (dx, dx), (val.reshape(1, 1), dg)

    (d_act, d_act_bf), (loss_part, g_final_g) = _rowwise(
        final, "final_norm_loss", [(acts, dm.d, 0), (target, dm.d, 0)], [final_g.reshape(1, -1)], [(dm.d, F32), (dm.d, BF16)], [1, dm.d])
    loss = lax.psum(loss_part[0, 0], AXES)

    place = jnp.stack([c_idx, my_quarter]).astype(jnp.int32)
    big, small, mine, theirs = [dict() for _ in range(depth)], [None] * depth, {}, {}
    chip = {}

    def to_sibling_stage(group, tag):
        def deliver(from_sibling):
            for (n, l), r in zip(group, from_sibling):
                chip[(n, l)] = _chip_sum(big[l][n], r, place)
        return (lambda: _to_sibling([big[l][n] for n, l in group], "grads_to_sibling" + tag, other_half=True), deliver)

    def to_owner_stage(group, tag):
        def deliver(from_chips):
            for k, r in zip(group, from_chips):
                mine[k] = _owner_sum(chip[k][1], r)
        return (lambda: _to_owner_chips([chip[k][0] for k in group], "grads_to_owner_chips" + tag), deliver)

    def halves_stage(group, tag):
        def deliver(from_sibling):
            theirs.update(zip(group, from_sibling))
        return (lambda: _to_sibling([mine[k] for k in group], "reduced_half_to_sibling" + tag), deliver)

    deferred = {}
    for l in reversed(range(depth)):
        rest = [(n, l) for n in SHARDED if n != "w_in"]
        tag = "_%d" % l
        carry = {key: list(stages) for key, stages in deferred.items()}
        carry.setdefault("in_proj_wgrad", []).append(to_sibling_stage(rest, "_rest" + tag))
        carry.setdefault("in_proj_dgrad", []).extend([to_owner_stage(rest, "_rest" + tag), to_sibling_stage([("w_in", l)], "_w_in" + tag)])
        d_act, d_act_bf, small[l] = _layer_bwd(dm, d_act, d_act_bf, ps[l], gw[l], saved[l], want_bf=l > 0, carry=carry, big=big[l])
        deferred = {"in_proj_wgrad": [to_owner_stage([("w_in", l)], "_w_in" + tag)],
                    "in_proj_dgrad": [halves_stage([(n, l) for n in SHARDED], tag)]}
    for key in ("in_proj_wgrad", "in_proj_dgrad"):
        for make, deliver in deferred[key]:
            deliver(make().run())
    grad_x = d_act[None]

    small_all = [n for n in WEIGHTS if n not in SHARDED]
    packed = _pack([small[l][n] for l in range(depth) for n in small_names] + [g_final_g])
    summed = _unpack(_allreduce_small(packed), [weights[n][l] for l in range(depth) for n in small_names] + [final_g])
    g_small = {n: jnp.stack([summed[l * len(small_names) + i] for l in range(depth)]) for i, n in enumerate(small_names)}
    g_small["final_g"] = summed[-1]

    out_g, out_d, out_m, out_v = {}, {}, {}, {}
    for n in SHARDED:
        shape = weights[n].shape
        if n == "w_pool":
            ngr = shape[1]
            both = jnp.stack([jnp.where(c_idx == 0, jnp.stack([mine[(n, l)], theirs[(n, l)]]), jnp.stack([theirs[(n, l)], mine[(n, l)]]))
                              for l in range(depth)])
            g = both.reshape(depth, 2, ngr, -1, shape[-1]).transpose(0, 2, 1, 3, 4).reshape(shape)
            res = [g] + _elementwise(lambda *a: _adamw(*a), "adamw_" + n, [weights[n], g, given["m_" + n], given["v_" + n]], [F32] * 3)
        else:
            halves = lambda a: a.reshape(depth, 2, -1, shape[-1])
            res = _adamw_halves(halves(weights[n]), halves(given["m_" + n]), halves(given["v_" + n]),
                                [mine[(n, l)] for l in range(depth)], [theirs[(n, l)] for l in range(depth)], place, "adamw_" + n)
        out_g[n], out_d[n], out_m[n], out_v[n] = [r.reshape(shape) for r in res]
    d, m, v = _whole(_adamw, "adamw_small", [_pack([weights[n] for n in small_all]), _pack([g_small[n] for n in small_all]),
                                            _pack([given["m_" + n] for n in small_all]), _pack([given["v_" + n] for n in small_all])],
                     [_pack([weights[n] for n in small_all]).shape] * 3)
    like = [weights[n] for n in small_all]
    for n, dd, mm, vv in zip(small_all, _unpack(d, like), _unpack(m, like), _unpack(v, like)):
        out_g[n], out_d[n], out_m[n], out_v[n] = g_small[n], dd, mm, vv
    return (loss, grad_x, *[out_g[n] for n in WEIGHTS], *[out_d[n] for n in WEIGHTS],
            *[out_m[n] for n in WEIGHTS], *[out_v[n] for n in WEIGHTS])
```

```python
import functools
import math

import jax
import jax.numpy as jnp
from jax import lax
from jax.experimental import pallas as pl
from jax.experimental.pallas import tpu as pltpu

F32 = jnp.float32
BF16 = jnp.bfloat16
EPS = 1e-6
POOL_WINDOWS = (2, 4, 8, 16)
POOL_HALO = 16
HEAD_DIM = 128
SSM_GROUP = 16
SSM_STATE = 64
GROUPS_PER_CHUNK = 16
LANES = 128
VMEM_LIMIT_BYTES = 56 * 1024 * 1024
ROW_TILE_ELEMS = 2 * 1024 * 1024
WIDE_TILE = 1024
ADAM_LR, ADAM_B1, ADAM_B2, ADAM_EPS, ADAM_WD, ADAM_STEP = 0.001, 0.9, 0.999, 1e-08, 0.01, 10
MESH = pl.DeviceIdType.MESH
AXES = ("x", "y", "c")
WEIGHTS = ("ln_g", "w_in", "w_pool", "pool_scale", "lam_re", "lam_im", "log_dt", "b_re", "b_im",
           "c_re", "c_im", "d_skip", "w_glu", "b_glu", "branch_g", "w_out", "final_g")
SHARDED = ("w_in", "w_pool", "w_glu", "w_out")


def _params(*sem):
    return pltpu.CompilerParams(dimension_semantics=sem or None, vmem_limit_bytes=VMEM_LIMIT_BYTES)


def _dot(a, b, dims=((1,), (0,))):
    return lax.dot_general(a, b, (dims, ((), ())), preferred_element_type=F32)


NN, NT, TN = ((1,), (0,)), ((1,), (1,)), ((0,), (0,))


def _split(x):
    hi = x.astype(BF16)
    return hi, (x - hi.astype(F32)).astype(BF16)


def _dot_rounded(a, b, dims):
    return _dot(a.astype(BF16), b.astype(BF16), dims)


def _sigmoid(x):
    return 1.0 / (1.0 + jnp.exp(-x))


def _gelu(x):
    return 0.5 * x * (1.0 + jnp.tanh(0.7978845608028654 * (x + 0.044715 * x * x * x)))


def _rms(x, g):
    return x * lax.rsqrt(jnp.mean(x * x, axis=-1, keepdims=True) + EPS) * g


HBM_SPEC = pl.BlockSpec(memory_space=pltpu.HBM)


def _place():
    x, y, c = lax.axis_index("x"), lax.axis_index("y"), lax.axis_index("c")
    chips = [(1 - x, y), (x, 1 - y), (1 - x, 1 - y)]
    return x, y, c, chips


def _remote(src, dst, send_sem, recv_sem, target):
    return pltpu.make_async_remote_copy(src_ref=src, dst_ref=dst, send_sem=send_sem, recv_sem=recv_sem,
                                        device_id=target, device_id_type=MESH)


class _Exchange:
    def __init__(self, name, inputs, out_shapes, sems, start, finish):
        self.name, self.inputs, self.out_shapes, self.sems = name, list(inputs), list(out_shapes), sems
        self.start, self.finish, self.results = start, finish, None

    def run(self):
        n_in, n_out = len(self.inputs), len(self.out_shapes)

        def body(*refs):
            parts = (refs[:n_in], refs[n_in:n_in + n_out], refs[-2], refs[-1])
            self.start(*parts)
            self.finish(*parts)

        self.results = pl.pallas_call(
            body, name=self.name, in_specs=[HBM_SPEC] * n_in, out_specs=[HBM_SPEC] * n_out, out_shape=self.out_shapes,
            scratch_shapes=[pltpu.SemaphoreType.DMA(self.sems), pltpu.SemaphoreType.DMA(self.sems)],
            compiler_params=pltpu.CompilerParams(has_side_effects=True),
        )(*self.inputs)
        return self.results


def _allgather_quarters(shards, name):
    n, own = len(shards), 6

    def start(ins, outs, send_sems, recv_sems):
        x, y, c, chips = _place()
        mine = 2 * x + y
        for t in range(n):
            for k, (cx, cy) in enumerate(chips):
                _remote(ins[t].at[c], outs[t].at[mine, c], send_sems.at[t, k], recv_sems.at[t, k], (cx, cy, c)).start()
            _remote(ins[t], outs[t].at[mine], send_sems.at[t, own], recv_sems.at[t, own], (x, y, 1 - c)).start()

    def finish(ins, outs, send_sems, recv_sems):
        x, y, c, chips = _place()
        mine, sibling = 2 * x + y, (x, y, 1 - c)
        for t in range(n):
            for k, (cx, cy) in enumerate(chips):
                landed = outs[t].at[2 * cx + cy, c]
                _remote(landed, landed, send_sems.at[t, k], recv_sems.at[t, k], sibling).wait_recv()
                _remote(landed, landed, send_sems.at[t, 3 + k], recv_sems.at[t, 3 + k], sibling).start()
        for t in range(n):
            got = outs[t].at[mine]
            _remote(got, got, send_sems.at[t, own], recv_sems.at[t, own], sibling).wait_recv()
            _remote(ins[t], got, send_sems.at[t, own], recv_sems.at[t, own], sibling).wait_send()
            for k, (cx, cy) in enumerate(chips):
                got = outs[t].at[2 * cx + cy, 1 - c]
                _remote(got, got, send_sems.at[t, 3 + k], recv_sems.at[t, 3 + k], sibling).wait_recv()
                sent = outs[t].at[2 * cx + cy, c]
                _remote(sent, sent, send_sems.at[t, 3 + k], recv_sems.at[t, 3 + k], sibling).wait_send()
                _remote(ins[t].at[c], sent, send_sems.at[t, k], recv_sems.at[t, k], sibling).wait_send()

    return _Exchange(name, shards, [jax.ShapeDtypeStruct((4,) + s.shape, s.dtype) for s in shards], (n, 7), start, finish)


def _to_sibling(arrays, name, other_half=False):
    n = len(arrays)

    def copies(ins, outs, send_sems, recv_sems):
        x, y, c, _ = _place()
        return [_remote(ins[t].at[1 - c] if other_half else ins[t], outs[t], send_sems.at[t], recv_sems.at[t], (x, y, 1 - c))
                for t in range(n)]

    def start(*refs):
        for cp in copies(*refs):
            cp.start()

    def finish(*refs):
        for cp in copies(*refs):
            cp.wait()

    shapes = [jax.ShapeDtypeStruct(a.shape[1:] if other_half else a.shape, a.dtype) for a in arrays]
    return _Exchange(name, arrays, shapes, (n,), start, finish)


def _to_owner_chips(arrays, name):
    n = len(arrays)

    def copies(ins, outs, send_sems, recv_sems):
        x, y, c, chips = _place()
        return [_remote(ins[t].at[2 * cx + cy], outs[t].at[k], send_sems.at[t, k], recv_sems.at[t, k], (cx, cy, c))
                for t in range(n) for k, (cx, cy) in enumerate(chips)]

    def start(*refs):
        for cp in copies(*refs):
            cp.start()

    def finish(*refs):
        for cp in copies(*refs):
            cp.wait()

    shapes = [jax.ShapeDtypeStruct((3,) + a.shape[1:], a.dtype) for a in arrays]
    return _Exchange(name, arrays, shapes, (n, 3), start, finish)


def _fit(tile, dim):
    tile = min(tile, dim)
    step = LANES if tile >= LANES else 8
    tile -= tile % step
    while dim % tile:
        tile -= step
    return tile


def _mm(a, b, dims, out_dtype, name, *, n=None, b_quarters=None, out_quarters=None, add=None, comm=None, tm=512, tn=512, tk=4096):
    if dims == TN:
        k_dim, m_dim = a.shape
    else:
        m_dim, k_dim = a.shape
    if n is None:
        n = b.shape[0] if dims == NT else b.shape[1]
    m_unit = {None: m_dim, "cols": m_dim // 2, "rows": m_dim // 8}[out_quarters]
    n_unit = n // 4 if (b_quarters == "n" or out_quarters == "cols") else n
    k_unit = k_dim // 4 if b_quarters == "k" else k_dim
    tm, tn, tk = _fit(tm, m_unit), _fit(tn, n_unit), _fit(tk, k_unit)
    gm, gn, gk = m_dim // tm, n // tn, k_dim // tk
    mb, nb, kb = m_unit // tm, n_unit // tn, k_unit // tk
    if dims == TN:
        a_spec = pl.BlockSpec((tk, tm), lambda i, j, k: (k, i))
    else:
        a_spec = pl.BlockSpec((tm, tk), lambda i, j, k: (i, k))
    if b_quarters == "n":
        bspec = pl.BlockSpec((None, tk, tn), lambda i, j, k: (j // nb, k, j % nb))
    elif b_quarters == "k":
        bspec = pl.BlockSpec((None, tn, tk), lambda i, j, k: (k // kb, j, k % kb))
    elif dims == NT:
        bspec = pl.BlockSpec((tn, tk), lambda i, j, k: (j, k))
    else:
        bspec = pl.BlockSpec((tk, tn), lambda i, j, k: (k, j))
    if out_quarters == "cols":
        out_shape = (2, 4, m_unit, n_unit)
        out_spec = pl.BlockSpec((None, None, tm, tn), lambda i, j, k: (i // mb, j // nb, i % mb, j % nb))
    elif out_quarters == "rows":
        out_shape = (2, 4, m_unit, n)
        out_spec = pl.BlockSpec((None, None, tm, tn), lambda i, j, k: ((i // mb) % 2, i // (2 * mb), i % mb, j))
    else:
        out_shape, out_spec = (m_dim, n), pl.BlockSpec((tm, tn), lambda i, j, k: (i, j))
    in_specs, operands = [a_spec, bspec], [a, b]
    if add is not None:
        in_specs.append(pl.BlockSpec((tm, tn), lambda i, j, k: (i, j)))
        operands.append(add)

    n_in = len(operands)
    comms = list(comm) if comm else []
    n_cin, n_cout = sum(len(e.inputs) for e in comms), sum(len(e.out_shapes) for e in comms)
    n_acc = 1 if gk > 1 else 0

    def body(*refs):
        a_ref, b_ref = refs[0], refs[1]
        add_ref = refs[2] if add is not None else None
        o_ref = refs[n_in + n_cin]
        if comms:
            c_refs, i_at, o_at, s_at = [], n_in, n_in + n_cin + 1, n_in + n_cin + 1 + n_cout + n_acc
            for e in comms:
                c_refs.append((refs[i_at:i_at + len(e.inputs)], refs[o_at:o_at + len(e.out_shapes)], refs[s_at], refs[s_at + 1]))
                i_at, o_at, s_at = i_at + len(e.inputs), o_at + len(e.out_shapes), s_at + 2
            step = (pl.program_id(0) * gn + pl.program_id(1)) * gk + pl.program_id(2)

            @pl.when(step == 0)
            def _():
                for e, r in zip(comms, c_refs):
                    e.start(*r)

        def finish(r):
            if add_ref is not None:
                r = r + add_ref[...]
            o_ref[...] = r.astype(o_ref.dtype)

        if gk == 1:
            finish(_dot(a_ref[...], b_ref[...], dims))
        else:
            acc = refs[n_in + n_cin + 1 + n_cout]
            k = pl.program_id(2)

            @pl.when(k == 0)
            def _():
                acc[...] = jnp.zeros_like(acc)

            acc[...] += _dot(a_ref[...], b_ref[...], dims)

            @pl.when(k == gk - 1)
            def _():
                finish(acc[...])

        if comms:
            @pl.when(step == gm * gn * gk - 1)
            def _():
                for e, r in zip(comms, c_refs):
                    e.finish(*r)

    scratch = [pltpu.VMEM((tm, tn), F32)] if gk > 1 else []
    if not comms:
        return pl.pallas_call(
            body, name=name, grid=(gm, gn, gk), in_specs=in_specs, out_specs=out_spec,
            out_shape=jax.ShapeDtypeStruct(out_shape, out_dtype), scratch_shapes=scratch,
            compiler_params=_params("parallel", "parallel", "arbitrary"),
        )(*operands)
    for e in comms:
        scratch += [pltpu.SemaphoreType.DMA(e.sems), pltpu.SemaphoreType.DMA(e.sems)]
    res = pl.pallas_call(
        body, name=name, grid=(gm, gn, gk), in_specs=in_specs + [HBM_SPEC] * n_cin, out_specs=[out_spec] + [HBM_SPEC] * n_cout,
        out_shape=[jax.ShapeDtypeStruct(out_shape, out_dtype)] + [s for e in comms for s in e.out_shapes],
        scratch_shapes=scratch,
        compiler_params=pltpu.CompilerParams(dimension_semantics=("arbitrary",) * 3, vmem_limit_bytes=VMEM_LIMIT_BYTES,
                                             has_side_effects=True),
    )(*operands, *[x for e in comms for x in e.inputs])
    at = 1
    for e in comms:
        e.results = list(res[at:at + len(e.out_shapes)])
        at += len(e.out_shapes)
    return res[0]


def _mm3(a, b, dims, name, nc=None, tm=512, rows3d=()):
    rows = a.shape[0]
    tm = min(tm, rows)
    gm = rows // tm

    def row_spec(width, three_d):
        if three_d:
            return pl.BlockSpec((tm, width // LANES, LANES), lambda c, i: (i, c, 0))
        return pl.BlockSpec((tm, width), lambda c, i: (i, c))

    def load(ref, three_d):
        if not three_d:
            return ref[...]
        return jnp.concatenate([ref[:, r, :] for r in range(ref.shape[1])], axis=1)

    if dims == TN:
        ka = a.shape[1] * (a.shape[2] if "a" in rows3d else 1) // nc
        nb = b.shape[1] * (b.shape[2] if "b" in rows3d else 1) // nc

        def body(a_ref, b_ref, o_ref):
            @pl.when(pl.program_id(1) == 0)
            def _():
                o_ref[...] = jnp.zeros_like(o_ref)
            o_ref[...] += _dot_rounded(load(a_ref, "a" in rows3d), load(b_ref, "b" in rows3d), TN)

        return pl.pallas_call(
            body, name=name, grid=(nc, gm),
            in_specs=[row_spec(ka, "a" in rows3d), row_spec(nb, "b" in rows3d)],
            out_specs=pl.BlockSpec((None, ka, nb), lambda c, i: (c, 0, 0)),
            out_shape=jax.ShapeDtypeStruct((nc, ka, nb), F32),
            compiler_params=_params("parallel", "arbitrary"),
        )(a, b)
    nc, ka, nb = b.shape
    wa, wo = (ka, nb) if dims == NN else (nb, ka)

    def body(a_ref, b_ref, o_ref):
        res = _dot_rounded(load(a_ref, "a" in rows3d), b_ref[...], dims)
        if "o" in rows3d:
            for r in range(wo // LANES):
                o_ref[:, r, :] = res[:, r * LANES:(r + 1) * LANES]
        else:
            o_ref[...] = res

    out_shape = (rows, nc * wo // LANES, LANES) if "o" in rows3d else (rows, nc * wo)
    return pl.pallas_call(
        body, name=name, grid=(nc, gm),
        in_specs=[row_spec(wa, "a" in rows3d), pl.BlockSpec((None, ka, nb), lambda c, i: (c, 0, 0))],
        out_specs=row_spec(wo, "o" in rows3d),
        out_shape=jax.ShapeDtypeStruct(out_shape, F32),
        compiler_params=_params("parallel", "parallel"),
    )(a, b)


def _rowwise(fn, name, rows, vecs=(), outs=(), sums=()):
    length = rows[0][0].shape[0]
    total = sum(w for _, w, _ in rows) + sum(w for w, _ in outs)
    tile = 8
    while tile * 2 <= min(length, 512) and tile * 2 * total <= ROW_TILE_ELEMS:
        tile *= 2
    assert length % tile == 0
    n_r, n_v, n_o = len(rows), len(vecs), len(outs)

    def body(*refs):
        vals = [r[...] for r in refs[:n_r + n_v]]
        o_refs = refs[n_r + n_v:n_r + n_v + n_o]
        s_refs = refs[n_r + n_v + n_o:]
        res_o, res_s = fn(*vals)
        for ref, val in zip(o_refs, res_o):
            ref[...] = val.astype(ref.dtype)
        if s_refs:
            @pl.when(pl.program_id(0) == 0)
            def _():
                for ref in s_refs:
                    ref[...] = jnp.zeros_like(ref)
            for ref, val in zip(s_refs, res_s):
                ref[...] += val

    def row_spec(w, cb):
        return pl.BlockSpec((tile, w), lambda i: (i, cb))

    res = pl.pallas_call(
        body, name=name, grid=(length // tile,),
        in_specs=[row_spec(w, cb) for _, w, cb in rows] + [pl.BlockSpec(v.shape, lambda i: (0, 0)) for v in vecs],
        out_specs=[row_spec(w, 0) for w, _ in outs] + [pl.BlockSpec((1, w), lambda i: (0, 0)) for w in sums],
        out_shape=[jax.ShapeDtypeStruct((length, w), dt) for w, dt in outs]
        + [jax.ShapeDtypeStruct((1, w), F32) for w in sums],
        compiler_params=_params("arbitrary" if sums else "parallel"),
    )(*[a for a, _, _ in rows], *vecs)
    return res[:n_o], res[n_o:]


def _elementwise(fn, name, arrays, out_dtypes):
    shape = arrays[0].shape
    cols = shape[-1]
    flat = [a.reshape(-1, cols) for a in arrays]
    rows = flat[0].shape[0]
    tile = 8
    while tile * 2 <= rows and rows % (tile * 2) == 0 and tile * 2 * cols * (len(arrays) + len(out_dtypes)) <= ROW_TILE_ELEMS:
        tile *= 2
    assert rows % tile == 0
    n_in = len(flat)

    def body(*refs):
        res = fn(*[r[...] for r in refs[:n_in]])
        for ref, val in zip(refs[n_in:], res):
            ref[...] = val.astype(ref.dtype)

    spec = pl.BlockSpec((tile, cols), lambda i: (i, 0))
    res = pl.pallas_call(
        body, name=name, grid=(rows // tile,), in_specs=[spec] * n_in, out_specs=[spec] * len(out_dtypes),
        out_shape=[jax.ShapeDtypeStruct((rows, cols), dt) for dt in out_dtypes],
        compiler_params=_params("parallel"),
    )(*flat)
    return [r.reshape(shape) for r in res]


def _whole(fn, name, arrays, out_shapes):
    n_in = len(arrays)

    def body(*refs):
        res = fn(*[r[...] for r in refs[:n_in]])
        for ref, val in zip(refs[n_in:], res):
            ref[...] = val

    return pl.pallas_call(
        body, name=name, out_shape=[jax.ShapeDtypeStruct(s, F32) for s in out_shapes],
        compiler_params=_params(),
    )(*arrays)


def _grad_tile(rows, cols, arrays_per_step):
    tile = 8
    while tile * 2 <= rows and rows % (tile * 2) == 0 and tile * 2 * cols * arrays_per_step <= ROW_TILE_ELEMS:
        tile *= 2
    return tile


def _chip_sum(partial, from_sibling, place):
    _, _, rows, cols = partial.shape
    tile = _grad_tile(rows, cols, 4)

    def body(place_ref, a_ref, b_ref, o16_ref, o32_ref):
        total = a_ref[...] + b_ref[...]
        o16_ref[...] = total.astype(BF16)

        @pl.when(pl.program_id(1) == place_ref[1])
        def _():
            o32_ref[...] = total

    return pl.pallas_call(
        body, name="chip_sum",
        grid_spec=pltpu.PrefetchScalarGridSpec(
            num_scalar_prefetch=1, grid=(rows // tile, 4),
            in_specs=[pl.BlockSpec((None, None, tile, cols), lambda i, q, p: (p[0], q, i, 0)),
                      pl.BlockSpec((None, tile, cols), lambda i, q, p: (q, i, 0))],
            out_specs=[pl.BlockSpec((None, tile, cols), lambda i, q, p: (q, i, 0)),
                       pl.BlockSpec((tile, cols), lambda i, q, p: (i, 0))]),
        out_shape=[jax.ShapeDtypeStruct((4, rows, cols), BF16), jax.ShapeDtypeStruct((rows, cols), F32)],
        compiler_params=_params("parallel", "arbitrary"),
    )(place, partial, from_sibling)


def _owner_sum(own, from_chips):
    rows, cols = own.shape
    tile = _grad_tile(rows, cols, 4)

    def body(a_ref, r0_ref, r1_ref, r2_ref, o_ref):
        o_ref[...] = a_ref[...] + r0_ref[...].astype(F32) + r1_ref[...].astype(F32) + r2_ref[...].astype(F32)

    spec = pl.BlockSpec((tile, cols), lambda i: (i, 0))
    return pl.pallas_call(
        body, name="owner_sum", grid=(rows // tile,),
        in_specs=[spec] + [pl.BlockSpec((None, tile, cols), functools.partial(lambda i, k: (k, i, 0), k=k)) for k in range(3)],
        out_specs=spec, out_shape=jax.ShapeDtypeStruct((rows, cols), F32), compiler_params=_params("parallel"),
    )(own, from_chips, from_chips, from_chips)


def _adamw_halves(w, m, v, mine, theirs, place, name):
    depth, _, rows, cols = w.shape
    tile = _grad_tile(rows, cols, 9)

    def body(place_ref, w_ref, m_ref, v_ref, *refs):
        g_refs, outs = refs[:2 * depth], refs[2 * depth:]
        layer, half = pl.program_id(0), pl.program_id(1)
        g = None
        for d in range(depth):
            gd = jnp.where(half == place_ref[0], g_refs[2 * d][...], g_refs[2 * d + 1][...])
            g = gd if g is None else jnp.where(layer == d, gd, g)
        delta, m_new, v_new = _adamw(w_ref[...], g, m_ref[...], v_ref[...])
        for ref, val in zip(outs, (g, delta, m_new, v_new)):
            ref[...] = val

    full = pl.BlockSpec((None, None, tile, cols), lambda l, h, i, p: (l, h, i, 0))
    g_specs = []
    for d in range(depth):
        g_specs += [pl.BlockSpec((tile, cols), functools.partial(lambda l, h, i, p, d: (jnp.where(l == d, i, 0), 0), d=d))] * 2
    operands = [x for pair in zip(mine, theirs) for x in pair]
    return pl.pallas_call(
        body, name=name,
        grid_spec=pltpu.PrefetchScalarGridSpec(
            num_scalar_prefetch=1, grid=(depth, 2, rows // tile),
            in_specs=[full] * 3 + g_specs, out_specs=[full] * 4),
        out_shape=[jax.ShapeDtypeStruct(w.shape, F32)] * 4,
        compiler_params=_params("arbitrary", "arbitrary", "arbitrary"),
    )(place, w, m, v, *operands)


def _allreduce_small(flat):
    rows = flat.shape[0] + (-flat.shape[0]) % (2 * SUBLANES)
    half = rows // 2
    padded = jnp.pad(flat, ((0, rows - flat.shape[0]), (0, 0)))
    swap, final = 0, 4

    def body(in_ref, out_ref, sibling_ref, chips_ref, total_ref, send_sems, recv_sems):
        x, y, c, chips = _place()
        mine, sibling = 2 * x + y, (x, y, 1 - c)
        my_rows = pl.ds(pl.multiple_of(c * half, SUBLANES), half)
        their_rows = pl.ds(pl.multiple_of((1 - c) * half, SUBLANES), half)
        cp = _remote(in_ref, sibling_ref, send_sems.at[swap], recv_sems.at[swap], sibling)
        cp.start()
        cp.wait()
        chips_ref[mine] = in_ref[my_rows, :] + sibling_ref[my_rows, :]
        sends = [_remote(chips_ref.at[mine], chips_ref.at[mine], send_sems.at[1 + k], recv_sems.at[1 + k], (cx, cy, c))
                 for k, (cx, cy) in enumerate(chips)]
        for cp in sends:
            cp.start()
        for k, (cx, cy) in enumerate(chips):
            slot = chips_ref.at[2 * cx + cy]
            _remote(slot, slot, send_sems.at[1 + k], recv_sems.at[1 + k], sibling).wait_recv()
        for cp in sends:
            cp.wait_send()
        total_ref[my_rows, :] = (chips_ref[0] + chips_ref[1]) + (chips_ref[2] + chips_ref[3])
        cp = _remote(total_ref.at[my_rows, :], total_ref.at[my_rows, :], send_sems.at[final], recv_sems.at[final], sibling)
        cp.start()
        _remote(total_ref.at[their_rows, :], total_ref.at[their_rows, :], send_sems.at[final], recv_sems.at[final], sibling).wait_recv()
        cp.wait_send()
        out_ref[...] = total_ref[...]

    out = pl.pallas_call(
        body, name="allreduce_small_grads",
        in_specs=[pl.BlockSpec(memory_space=pltpu.VMEM)], out_specs=pl.BlockSpec(memory_space=pltpu.VMEM),
        out_shape=jax.ShapeDtypeStruct((rows, LANES), F32),
        scratch_shapes=[pltpu.VMEM((rows, LANES), F32), pltpu.VMEM((4, half, LANES), F32), pltpu.VMEM((rows, LANES), F32),
                        pltpu.SemaphoreType.DMA((5,)), pltpu.SemaphoreType.DMA((5,))],
        compiler_params=pltpu.CompilerParams(has_side_effects=True, vmem_limit_bytes=VMEM_LIMIT_BYTES),
    )(padded)
    return out[:flat.shape[0]]


def _pool_tile(length):
    return min(256, length)


def _pool_fwd(proj, w_pool):
    length = proj.shape[0]
    ngroups, ch, _ = w_pool.shape
    width, tile = ngroups * ch, _pool_tile(length)

    def body(cur_ref, prev_ref, w_ref, pooled_ref, mixed_ref):
        i = pl.program_id(0)
        cur = cur_ref[...]
        tail = jnp.where(i > 0, prev_ref[tile - POOL_HALO:tile, :], 0.0)
        padded = jnp.concatenate([tail, cur], axis=0)
        pos = (lax.broadcasted_iota(jnp.int32, (tile, 1), 0) + i * tile + 1).astype(F32)
        for g, window in enumerate(POOL_WINDOWS):
            cols = slice(g * ch, (g + 1) * ch)
            run, shift = padded[:, cols], 1
            while shift < window:
                run = run + pltpu.roll(run, shift, 0)
                shift *= 2
            pooled = (run[POOL_HALO:, :] / jnp.minimum(pos, float(window)) - cur[:, cols]).astype(BF16)
            pooled_ref[:, cols] = pooled
            mixed_ref[:, cols] = _dot(pooled, w_ref[g])

    return pl.pallas_call(
        body, name="pool_fwd", grid=(length // tile,),
        in_specs=[pl.BlockSpec((tile, width), lambda i: (i, 0)),
                  pl.BlockSpec((tile, width), lambda i: (jnp.maximum(i - 1, 0), 0)),
                  pl.BlockSpec(w_pool.shape, lambda i: (0, 0, 0))],
        out_specs=[pl.BlockSpec((tile, width), lambda i: (i, 0))] * 2,
        out_shape=[jax.ShapeDtypeStruct((length, width), BF16), jax.ShapeDtypeStruct((length, width), F32)],
        compiler_params=_params("parallel"),
    )(proj, proj, w_pool)


def _pool_bwd_mix(d_mixed, pooled, w_pool):
    length, width = d_mixed.shape
    ngroups, ch, _ = w_pool.shape
    tile = _pool_tile(length)

    def body(dm_ref, pooled_ref, w_ref, dp_ref, dw_ref):
        @pl.when(pl.program_id(0) == 0)
        def _():
            dw_ref[...] = jnp.zeros_like(dw_ref)
        for g in range(ngroups):
            cols = slice(g * ch, (g + 1) * ch)
            dm = dm_ref[:, cols].astype(BF16)
            dp_ref[:, cols] = _dot(dm, w_ref[g], NT)
            dw_ref[g] += _dot(pooled_ref[:, cols], dm, TN)

    return pl.pallas_call(
        body, name="pool_bwd_mix", grid=(length // tile,),
        in_specs=[pl.BlockSpec((tile, width), lambda i: (i, 0)), pl.BlockSpec((tile, width), lambda i: (i, 0)),
                  pl.BlockSpec(w_pool.shape, lambda i: (0, 0, 0))],
        out_specs=[pl.BlockSpec((tile, width), lambda i: (i, 0)), pl.BlockSpec(w_pool.shape, lambda i: (0, 0, 0))],
        out_shape=[jax.ShapeDtypeStruct((length, width), F32), jax.ShapeDtypeStruct(w_pool.shape, F32)],
        compiler_params=_params("arbitrary"),
    )(d_mixed, pooled, w_pool)


def _pool_bwd_window(d_pooled, ngroups):
    length, width = d_pooled.shape
    ch, tile = width // ngroups, _pool_tile(length)
    last = length // tile - 1

    def body(cur_ref, next_ref, dx_ref):
        i = pl.program_id(0)
        cur = cur_ref[...]
        head = jnp.where(i < last, next_ref[0:POOL_HALO, :], 0.0)
        padded = jnp.concatenate([cur, head], axis=0)
        rows = tile + POOL_HALO
        pos = (lax.broadcasted_iota(jnp.int32, (rows, 1), 0) + i * tile + 1).astype(F32)
        for g, window in enumerate(POOL_WINDOWS):
            cols = slice(g * ch, (g + 1) * ch)
            run, shift = padded[:, cols] / jnp.minimum(pos, float(window)), 1
            while shift < window:
                run = run + pltpu.roll(run, rows - shift, 0)
                shift *= 2
            dx_ref[:, cols] = (run[0:tile, :] - cur[:, cols]).astype(BF16)

    return pl.pallas_call(
        body, name="pool_bwd_window", grid=(length // tile,),
        in_specs=[pl.BlockSpec((tile, width), lambda i: (i, 0)),
                  pl.BlockSpec((tile, width), lambda i: (jnp.minimum(i + 1, last), 0))],
        out_specs=pl.BlockSpec((tile, width), lambda i: (i, 0)),
        out_shape=jax.ShapeDtypeStruct((length, width), BF16),
        compiler_params=_params("parallel"),
    )(d_pooled, d_pooled)


ATTN_TILE = 256
LOG_WEIGHT_FLOOR = -110.0


def _walk_back(n_chunks, chunk, carry):
    def cond(state):
        return jnp.logical_and(state[0] < n_chunks, jnp.max(state[1]) > LOG_WEIGHT_FLOOR)

    def step(state):
        return (state[0] + 1,) + tuple(chunk(n_chunks - 1 - state[0], tuple(state[1:])))

    return lax.while_loop(cond, step, (jnp.int32(0),) + tuple(carry))[1:]


def _stick_weights(q, kc, upper, run_log, mask):
    z = _dot(q, kc, NT)
    e = jnp.exp(-jnp.abs(z))
    softplus = jnp.maximum(z, 0.0) + jnp.log(1.0 + e)
    log_sig = z - softplus
    log_1m = -softplus if mask is None else jnp.where(mask, -softplus, 0.0)
    suffix = _dot(log_1m.astype(BF16), upper) + run_log
    w = jnp.exp(log_sig + suffix)
    if mask is not None:
        w = jnp.where(mask, w, 0.0)
    return w, log_sig, suffix[:, 0:1] + log_1m[:, 0:1]


def _attn_consts(tile):
    jj = lax.broadcasted_iota(jnp.int32, (tile, tile), 0)
    ss = lax.broadcasted_iota(jnp.int32, (tile, tile), 1)
    return (jj > ss).astype(BF16), (jj >= ss).astype(BF16), ss < jj


HEADS_PER_STEP = 2


def _heads_per_step(n_heads, *blocks):
    ok = n_heads % HEADS_PER_STEP == 0 and all(b % HEADS_PER_STEP == 0 for b in blocks)
    return HEADS_PER_STEP if ok else 1


def _slowest(run_logs):
    out = run_logs[0]
    for r in run_logs[1:]:
        out = jnp.maximum(out, r)
    return out


def _attn_fwd(proj, n_heads, q_blk, k_blk, v_blk):
    length = proj.shape[0]
    tile = min(ATTN_TILE, length)
    scale = HEAD_DIM ** -0.5

    hps = _heads_per_step(n_heads, q_blk, k_blk, v_blk)
    width = hps * HEAD_DIM

    def body(q_ref, k_ref, v_ref, o_ref):
        i = pl.program_id(1)
        cols = [slice(h * HEAD_DIM, (h + 1) * HEAD_DIM) for h in range(hps)]
        qs = [(q_ref[:, c] * scale).astype(BF16) for c in cols]
        upper, _, diag_mask = _attn_consts(tile)

        def chunk(j, carry, mask):
            start = pl.multiple_of(j * tile, tile)
            new = []
            for h, c in enumerate(cols):
                run_log, acc = carry[1 + 2 * h], carry[2 + 2 * h]
                kc = k_ref[pl.ds(start, tile), c].astype(BF16)
                vc = v_ref[pl.ds(start, tile), c].astype(BF16)
                w, _, run_log = _stick_weights(qs[h], kc, upper, run_log, mask)
                new += [run_log, acc + _dot(w.astype(BF16), vc)]
            return (_slowest(new[0::2]),) + tuple(new)

        zero = jnp.zeros((tile, 1), F32)
        carry = (zero,) + (zero, jnp.zeros((tile, HEAD_DIM), F32)) * hps
        carry = chunk(i, carry, diag_mask)
        carry = _walk_back(i, lambda j, cr: chunk(j, cr, None), carry)
        for h, c in enumerate(cols):
            o_ref[:, c] = carry[2 + 2 * h]

    return pl.pallas_call(
        body, name="attn_fwd", grid=(n_heads // hps, length // tile),
        in_specs=[pl.BlockSpec((tile, width), lambda h, i: (i, q_blk // hps + h)),
                  pl.BlockSpec((length, width), lambda h, i: (0, k_blk // hps + h)),
                  pl.BlockSpec((length, width), lambda h, i: (0, v_blk // hps + h))],
        out_specs=pl.BlockSpec((tile, width), lambda h, i: (i, h)),
        out_shape=jax.ShapeDtypeStruct((length, n_heads * HEAD_DIM), F32),
        compiler_params=_params("parallel", "parallel"),
    )(proj, proj, proj)


def _attn_bwd(proj, out, d_out, n_heads, q_blk, k_blk, v_blk):
    length = proj.shape[0]
    tile = min(ATTN_TILE, length)
    scale = HEAD_DIM ** -0.5

    hps = _heads_per_step(n_heads, q_blk, k_blk, v_blk)
    step_width = hps * HEAD_DIM

    def body(q_ref, k_ref, v_ref, o_ref, do_ref, dq_ref, dk_ref, dv_ref):
        i = pl.program_id(1)

        @pl.when(i == 0)
        def _():
            dk_ref[...] = jnp.zeros_like(dk_ref)
            dv_ref[...] = jnp.zeros_like(dv_ref)

        cols = [slice(h * HEAD_DIM, (h + 1) * HEAD_DIM) for h in range(hps)]
        qs = [(q_ref[:, c] * scale).astype(BF16) for c in cols]
        dos = [do_ref[:, c].astype(BF16) for c in cols]
        totals = [jnp.sum(do.astype(F32) * o_ref[:, c], axis=1, keepdims=True) for do, c in zip(dos, cols)]
        upper, upper_incl, diag_mask = _attn_consts(tile)

        def chunk(j, carry, mask):
            start = pl.multiple_of(j * tile, tile)
            new = []
            for h, c in enumerate(cols):
                run_log, run_g, dq = carry[1 + 3 * h:4 + 3 * h]
                q, do = qs[h], dos[h]
                kc = k_ref[pl.ds(start, tile), c].astype(BF16)
                vc = v_ref[pl.ds(start, tile), c].astype(BF16)
                w, log_sig, run_log = _stick_weights(q, kc, upper, run_log, mask)
                wb = w.astype(BF16)
                g = wb.astype(F32) * _dot(do, vc, NT)
                g_hi, g_lo = _split(g)
                g_suffix = _dot(g_hi, upper_incl) + _dot(g_lo, upper_incl) + run_g
                dz = g - jnp.exp(log_sig) * (g + (totals[h] - g_suffix))
                if mask is not None:
                    dz = jnp.where(mask, dz, 0.0)
                dzb = dz.astype(BF16)
                dk_ref[pl.ds(start, tile), c] += _dot(dzb, q, TN)
                dv_ref[pl.ds(start, tile), c] += _dot(wb, do, TN)
                new += [run_log, g_suffix[:, 0:1], dq + _dot(dzb, kc)]
            return (_slowest(new[0::3]),) + tuple(new)

        zero = jnp.zeros((tile, 1), F32)
        carry = (zero,) + (zero, zero, jnp.zeros((tile, HEAD_DIM), F32)) * hps
        carry = chunk(i, carry, diag_mask)
        carry = _walk_back(i, lambda j, cr: chunk(j, cr, None), carry)
        for h, c in enumerate(cols):
            dq_ref[:, c] = (carry[3 + 3 * h] * scale).astype(BF16)

    width = n_heads * HEAD_DIM
    tile_spec = pl.BlockSpec((tile, step_width), lambda h, i: (i, h))
    head_spec = pl.BlockSpec((length, step_width), lambda h, i: (0, h))
    return pl.pallas_call(
        body, name="attn_bwd", grid=(n_heads // hps, length // tile),
        in_specs=[pl.BlockSpec((tile, step_width), lambda h, i: (i, q_blk // hps + h)),
                  pl.BlockSpec((length, step_width), lambda h, i: (0, k_blk // hps + h)),
                  pl.BlockSpec((length, step_width), lambda h, i: (0, v_blk // hps + h)),
                  tile_spec, tile_spec],
        out_specs=[tile_spec, head_spec, head_spec],
        out_shape=[jax.ShapeDtypeStruct((length, width), BF16), jax.ShapeDtypeStruct((length, width), F32),
                   jax.ShapeDtypeStruct((length, width), F32)],
        compiler_params=_params("parallel", "arbitrary"),
    )(proj, proj, proj, out, d_out)


SCAN_CHUNK = 128


def _disc_lam(lam_re, lam_im, log_dt):
    dt = jnp.exp(log_dt)
    mag, phase = jnp.exp(lam_re * dt), lam_im * dt
    bar_re, bar_im = mag * jnp.cos(phase), mag * jnp.sin(phase)
    num_re, den = bar_re - 1.0, lam_re * lam_re + lam_im * lam_im
    return (bar_re, bar_im, (num_re * lam_re + bar_im * lam_im) / den, (bar_im * lam_re - num_re * lam_im) / den)


def _disc_b(cf_re, cf_im, b_re, b_im):
    return cf_re * b_re - cf_im * b_im, cf_re * b_im + cf_im * b_re


SCAN_UNROLL = 8
SUBLANES = 8


def _state_rows(per_group):
    return per_group.reshape(-1, SUBLANES, LANES)


def _swap_parts(x):
    pieces = []
    for k in range(x.shape[0] // (2 * SUBLANES)):
        base = 2 * SUBLANES * k
        pieces += [x[base + SUBLANES:base + 2 * SUBLANES], x[base:base + SUBLANES]]
    return jnp.concatenate(pieces, axis=0)


def _scan_coeffs(re_rows, im_rows, conj):
    same, cross = [], []
    for k in range(re_rows.shape[0]):
        same += [re_rows[k], re_rows[k]]
        cross += [im_rows[k], -im_rows[k]] if conj else [-im_rows[k], im_rows[k]]
    return jnp.concatenate(same, axis=0), jnp.concatenate(cross, axis=0)


def _scan_fwd(bu, bar_re, bar_im):
    length, groups, width = bu.shape
    chunk = min(SCAN_CHUNK, length)

    def body(bu_ref, re_ref, im_ref, st_ref, carry):
        @pl.when(pl.program_id(0) == 0)
        def _():
            carry[...] = jnp.zeros_like(carry)
        a_same, a_cross = _scan_coeffs(re_ref[...], im_ref[...], conj=False)

        def step(blk, x):
            for r in range(SCAN_UNROLL):
                t = blk * SCAN_UNROLL + r
                x = a_same * x + a_cross * _swap_parts(x) + bu_ref[t]
                st_ref[t] = x
            return x

        carry[...] = lax.fori_loop(0, chunk // SCAN_UNROLL, step, carry[...])

    blk = pl.BlockSpec((chunk, groups, width), lambda i: (i, 0, 0))
    par = pl.BlockSpec(bar_re.shape, lambda i: (0, 0, 0))
    return pl.pallas_call(
        body, name="s5_scan_fwd", grid=(length // chunk,), in_specs=[blk, par, par], out_specs=blk,
        out_shape=jax.ShapeDtypeStruct(bu.shape, F32), scratch_shapes=[pltpu.VMEM((groups, width), F32)],
        compiler_params=_params("arbitrary"),
    )(bu, bar_re, bar_im)


def _scan_bwd(d_states, states, bar_re, bar_im):
    length, groups, width = states.shape
    chunk = min(SCAN_CHUNK, length)
    last = length // chunk - 1

    def body(g_ref, st_ref, re_ref, im_ref, out_ref, same_ref, swap_ref, carry):
        @pl.when(pl.program_id(0) == 0)
        def _():
            carry[...] = jnp.zeros_like(carry)
            same_ref[...] = jnp.zeros_like(same_ref)
            swap_ref[...] = jnp.zeros_like(swap_ref)
        a_same, a_cross = _scan_coeffs(re_ref[...], im_ref[...], conj=True)

        def step(blk, cr):
            adj, acc_same, acc_swap = cr
            for r in range(SCAN_UNROLL):
                t = chunk - 1 - (blk * SCAN_UNROLL + r)
                s = st_ref[t]
                acc_same = acc_same + adj * s
                acc_swap = acc_swap + adj * _swap_parts(s)
                adj = g_ref[t] + a_same * adj + a_cross * _swap_parts(adj)
                out_ref[t] = adj
            return adj, acc_same, acc_swap

        adj, acc_same, acc_swap = lax.fori_loop(0, chunk // SCAN_UNROLL, step, (carry[...], same_ref[...], swap_ref[...]))
        carry[...] = adj
        same_ref[...] = acc_same
        swap_ref[...] = acc_swap

    blk = pl.BlockSpec((chunk, groups, width), lambda i: (last - i, 0, 0))
    par = pl.BlockSpec(bar_re.shape, lambda i: (0, 0, 0))
    acc = pl.BlockSpec((groups, width), lambda i: (0, 0))
    return pl.pallas_call(
        body, name="s5_scan_bwd", grid=(length // chunk,), in_specs=[blk, blk, par, par], out_specs=[blk, acc, acc],
        out_shape=[jax.ShapeDtypeStruct(states.shape, F32), jax.ShapeDtypeStruct((groups, width), F32),
                   jax.ShapeDtypeStruct((groups, width), F32)],
        scratch_shapes=[pltpu.VMEM((groups, width), F32)],
        compiler_params=_params("arbitrary"),
    )(d_states, states, bar_re, bar_im)


def _lam_bar_grad(acc_same, acc_swap):
    def fn(same, swap):
        g_re, g_im = [], []
        for k in range(same.shape[0] // (2 * SUBLANES)):
            re, im = slice(2 * SUBLANES * k, 2 * SUBLANES * k + SUBLANES), slice(2 * SUBLANES * k + SUBLANES, 2 * SUBLANES * (k + 1))
            g_re.append(same[re] + same[im])
            g_im.append(swap[im] - swap[re])
        return jnp.concatenate(g_re, axis=0), jnp.concatenate(g_im, axis=0)
    return _whole(fn, "s5_lam_bar_grad", [acc_same, acc_swap], [(acc_same.shape[0] // 2, LANES)] * 2)


def _block_diag(per_group):
    groups, a, b = per_group.shape
    nc = groups // GROUPS_PER_CHUNK
    eye = jnp.eye(GROUPS_PER_CHUNK, dtype=per_group.dtype)
    x = per_group.reshape(nc, GROUPS_PER_CHUNK, a, 1, b) * eye[None, :, None, :, None]
    return x.reshape(nc, GROUPS_PER_CHUNK * a, GROUPS_PER_CHUNK * b)


def _block_diag_part(chunks, a, b):
    nc = chunks.shape[0]
    x = chunks.reshape(nc, GROUPS_PER_CHUNK, a, GROUPS_PER_CHUNK, b)
    x = jnp.stack([x[:, g, :, g, :] for g in range(GROUPS_PER_CHUNK)], axis=1)
    return x.reshape(nc * GROUPS_PER_CHUNK, a, b)


def _epilogue(raw, gate, scale, g):
    return _rms(raw * scale, g) * (gate * _sigmoid(gate))


def _ssm_mid(y, u, d_skip):
    return _gelu(y + d_skip * u)


def _adamw(w, g, m, v):
    m = ADAM_B1 * m + (1.0 - ADAM_B1) * g
    v = ADAM_B2 * v + (1.0 - ADAM_B2) * (g * g)
    m_hat = m / (1.0 - ADAM_B1 ** ADAM_STEP)
    v_hat = v / (1.0 - ADAM_B2 ** ADAM_STEP)
    return -ADAM_LR * (m_hat / (jnp.sqrt(v_hat) + ADAM_EPS) + ADAM_WD * w), m, v


class _Dims:
    def __init__(self, d_model, length):
        self.d, self.length = d_model, length
        self.d_pool, self.d_attn = d_model // 4, d_model // 2
        self.d_ssm = d_model - self.d_pool - self.d_attn
        self.heads = self.d_attn // HEAD_DIM
        self.groups = self.d_ssm // SSM_GROUP
        self.d_in = 2 * self.d_pool + 4 * self.d_attn + 2 * self.d_ssm
        sizes = (self.d_pool, self.d_pool, self.d_attn, self.d_attn, self.d_attn, self.d_attn, self.d_ssm, self.d_ssm)
        offs = [0]
        for s in sizes[:-1]:
            offs.append(offs[-1] + s)
        (self.o_px, self.o_pgate, self.o_q, self.o_k, self.o_v, self.o_agate, self.o_u, self.o_sgate) = offs


def _ssm_operands(dm, p):
    groups, states = dm.groups, SSM_STATE
    bar_re, bar_im, cf_re, cf_im = _whole(_disc_lam, "s5_disc_lam", [p["lam_re"], p["lam_im"], p["log_dt"].reshape(groups, 1)],
                                          [(groups, states)] * 4)
    b_re2, b_im2 = p["b_re"].reshape(groups * states, SSM_GROUP), p["b_im"].reshape(groups * states, SSM_GROUP)
    bb_re, bb_im = _whole(_disc_b, "s5_disc_b", [cf_re.reshape(-1, 1), cf_im.reshape(-1, 1), b_re2, b_im2],
                          [(groups * states, SSM_GROUP)] * 2)
    per_group = lambda a: jnp.swapaxes(a.reshape(groups, states, SSM_GROUP), 1, 2)
    b_blk = jnp.concatenate([_block_diag(per_group(bb_re)), _block_diag(per_group(bb_im))], axis=2)
    c_blk = jnp.concatenate([_block_diag(jnp.swapaxes(p["c_re"], 1, 2)), _block_diag(jnp.swapaxes(-p["c_im"], 1, 2))], axis=1)
    return dict(bar_re=bar_re, bar_im=bar_im, bar_re_rows=_state_rows(bar_re), bar_im_rows=_state_rows(bar_im),
                cf_re=cf_re, cf_im=cf_im, b_re2=b_re2, b_im2=b_im2, b_blk=b_blk, c_blk=c_blk)


def _mm_carrying(carry, key, *args, **kwargs):
    stages = carry.get(key, [])
    exchanges = [make() for make, _ in stages]
    out = _mm(*args, comm=exchanges, **kwargs)
    for (_, deliver), exchange in zip(stages, exchanges):
        deliver(exchange.results)
    return out


def _layer_fwd(dm, x_in, p, gw, carry):
    length = dm.length
    blk = lambda off, w: off // w
    (h,), _ = _rowwise(lambda x, g: ((_rms(x, g),), ()), "rms_fwd", [(x_in, dm.d, 0)], [p["ln_g"]], [(dm.d, BF16)])
    proj = _mm_carrying(carry, "in_proj", h, gw["w_in"], NN, F32, "in_proj", n=dm.d_in, b_quarters="n", tn=WIDE_TILE)
    pooled, mixed = _pool_fwd(proj, gw["w_pool"])
    qb, kb, vb = dm.o_q // HEAD_DIM, dm.o_k // HEAD_DIM, dm.o_v // HEAD_DIM
    attn = _attn_fwd(proj, dm.heads, qb, kb, vb)
    so = _ssm_operands(dm, p)
    u_row = (proj, dm.d_ssm, blk(dm.o_u, dm.d_ssm))
    (u,), _ = _rowwise(lambda v: ((v,), ()), "take_u", [u_row], [], [(dm.d_ssm, F32)])
    bu = _mm3(u, so["b_blk"], NN, "s5_bu", rows3d="o")
    states = _scan_fwd(bu, so["bar_re_rows"], so["bar_im_rows"])
    y = _mm3(states, so["c_blk"], NN, "s5_y", rows3d="a")
    (hg,), _ = _rowwise(lambda yy, uu, dsk: ((_ssm_mid(yy, uu, dsk),), ()), "s5_mid_fwd",
                        [(y, dm.d_ssm, 0), (u, dm.d_ssm, 0)], [p["d_skip"]], [(dm.d_ssm, BF16)])
    z = _mm(hg, gw["w_glu"], NN, F32, "glu_proj", n=2 * dm.d_ssm, b_quarters="n")

    def glu(zz, bias):
        zz = zz + bias
        return (zz[:, :dm.d_ssm] * _sigmoid(zz[:, dm.d_ssm:]),), ()

    (ssm,), _ = _rowwise(glu, "glu_fwd", [(z, 2 * dm.d_ssm, 0)], [p["b_glu"]], [(dm.d_ssm, F32)])
    g_pool, g_attn, g_ssm = (p["branch_g"][:, :dm.d_pool], p["branch_g"][:, dm.d_pool:dm.d_pool + dm.d_attn],
                             p["branch_g"][:, dm.d_pool + dm.d_attn:])
    ones_attn, ones_ssm = jnp.ones((1, dm.d_attn), F32), jnp.ones((1, dm.d_ssm), F32)
    epi = lambda raw, gate, scale, g: ((_epilogue(raw, gate, scale, g),), ())
    branches = [("pool", mixed, dm.d_pool, dm.o_pgate, p["pool_scale"], g_pool),
                ("attn", attn, dm.d_attn, dm.o_agate, ones_attn, g_attn),
                ("ssm", ssm, dm.d_ssm, dm.o_sgate, ones_ssm, g_ssm)]
    ys = []
    for nm, raw, w, off, scale, g in branches:
        (yb,), _ = _rowwise(epi, "epilogue_fwd_" + nm, [(raw, w, 0), (proj, w, blk(off, w))], [scale, g], [(w, BF16)])
        ys.append(yb)
    y_cat = jnp.concatenate(ys, axis=1)
    x_out = _mm_carrying(carry, "out_proj", y_cat, gw["w_out"], NN, F32, "out_proj", add=x_in, tn=WIDE_TILE)
    saved = dict(x_in=x_in, h=h, proj=proj, pooled=pooled, mixed=mixed, attn=attn, so=so, u=u, states=states,
                 y=y, hg=hg, z=z, ssm=ssm, y_cat=y_cat, scales=(p["pool_scale"], ones_attn, ones_ssm), gs=(g_pool, g_attn, g_ssm))
    return x_out, saved


def _layer_bwd(dm, d_out, d_out_bf, p, gw, sv, want_bf, carry, big):
    length = dm.length
    blk = lambda off, w: off // w
    proj = sv["proj"]
    d_y = _mm_carrying(carry, "out_proj_dgrad", d_out_bf, gw["w_out"], NT, F32, "out_proj_dgrad", tn=WIDE_TILE)
    big["w_out"] = _mm(sv["y_cat"], d_out_bf, TN, F32, "out_proj_wgrad", out_quarters="rows", tn=WIDE_TILE)

    def epi_bwd(nseg):
        def fn(*vals):
            dys, (raw, gate, scale, g) = vals[:nseg], vals[nseg:]
            dyb = dys[0] if nseg == 1 else jnp.concatenate(dys, axis=1)
            _, vjp = jax.vjp(_epilogue, raw, gate, scale, g)
            d_raw, d_gate, d_scale, d_g = vjp(dyb)
            return (d_raw, d_gate), (d_scale, d_g)
        return fn

    branch = [("pool", sv["mixed"], dm.d_pool, dm.o_pgate, 0), ("attn", sv["attn"], dm.d_attn, dm.o_agate, dm.d_pool),
              ("ssm", sv["ssm"], dm.d_ssm, dm.o_sgate, dm.d_pool + dm.d_attn)]
    d_raws, d_gates, d_scales, d_gs = [], [], [], []
    for (nm, raw, w, off, yoff), scale, g in zip(branch, sv["scales"], sv["gs"]):
        seg = math.gcd(w, yoff) if yoff else w
        dy_rows = [(d_y, seg, yoff // seg + s) for s in range(w // seg)]
        (d_raw, d_gate), (d_scale, d_g) = _rowwise(
            epi_bwd(len(dy_rows)), "epilogue_bwd_" + nm, dy_rows + [(raw, w, 0), (proj, w, blk(off, w))], [scale, g],
            [(w, F32), (w, BF16)], [w, w])
        d_raws.append(d_raw); d_gates.append(d_gate); d_scales.append(d_scale); d_gs.append(d_g)
    d_pooled, g_w_pool = _pool_bwd_mix(d_raws[0], sv["pooled"], gw["w_pool"])
    ngr, ch = gw["w_pool"].shape[0], gw["w_pool"].shape[1]
    q_rows = ch // 4
    big["w_pool"] = g_w_pool.reshape(ngr, 4, 2, q_rows // 2, ch).transpose(2, 1, 0, 3, 4).reshape(2, 4, ngr * q_rows // 2, ch)
    d_px = _pool_bwd_window(d_pooled, len(POOL_WINDOWS))
    qb, kb, vb = dm.o_q // HEAD_DIM, dm.o_k // HEAD_DIM, dm.o_v // HEAD_DIM
    d_q, d_k, d_v = _attn_bwd(proj, sv["attn"], d_raws[1], dm.heads, qb, kb, vb)
    so = sv["so"]

    def glu_bwd(d_ssm, zz, bias):
        zz = zz + bias
        val, sg = zz[:, :dm.d_ssm], _sigmoid(zz[:, dm.d_ssm:])
        dz = jnp.concatenate([d_ssm * sg, d_ssm * val * sg * (1.0 - sg)], axis=1)
        return (dz,), (jnp.sum(dz, axis=0, keepdims=True),)

    (d_z,), (g_b_glu,) = _rowwise(glu_bwd, "glu_bwd", [(d_raws[2], dm.d_ssm, 0), (sv["z"], 2 * dm.d_ssm, 0)], [p["b_glu"]],
                                  [(2 * dm.d_ssm, BF16)], [2 * dm.d_ssm])
    d_hg = _mm(d_z, gw["w_glu"], NT, F32, "glu_dgrad", n=dm.d_ssm, b_quarters="k")
    big["w_glu"] = _mm(sv["hg"], d_z, TN, F32, "glu_wgrad", out_quarters="cols")

    def mid_bwd(dh, yy, uu, dsk):
        _, vjp = jax.vjp(_ssm_mid, yy, uu, dsk)
        dy_, du_, ddsk = vjp(dh)
        return (dy_, du_), (ddsk,)

    (d_yssm, d_u_direct), (g_d_skip,) = _rowwise(mid_bwd, "s5_mid_bwd", [(d_hg, dm.d_ssm, 0), (sv["y"], dm.d_ssm, 0), (sv["u"], dm.d_ssm, 0)],
                                                 [p["d_skip"]], [(dm.d_ssm, F32), (dm.d_ssm, F32)], [dm.d_ssm])
    d_states = _mm3(d_yssm, so["c_blk"], NT, "s5_y_dgrad", rows3d="o")
    d_c_blk = _mm3(sv["states"], d_yssm, TN, "s5_y_wgrad", nc=so["c_blk"].shape[0], rows3d="a")
    d_bu, acc_same, acc_swap = _scan_bwd(d_states, sv["states"], so["bar_re_rows"], so["bar_im_rows"])
    d_u_scan = _mm3(d_bu, so["b_blk"], NT, "s5_bu_dgrad", rows3d="a")
    d_b_blk = _mm3(sv["u"], d_bu, TN, "s5_bu_wgrad", nc=so["b_blk"].shape[0], rows3d="b")
    (d_u,) = _elementwise(lambda a, b: (a + b,), "s5_du", [d_u_direct, d_u_scan], [BF16])
    groups, states = dm.groups, SSM_STATE
    part = GROUPS_PER_CHUNK * states
    g_c_re = jnp.swapaxes(_block_diag_part(d_c_blk[:, :part], states, SSM_GROUP), 1, 2)
    g_c_im_neg = jnp.swapaxes(_block_diag_part(d_c_blk[:, part:], states, SSM_GROUP), 1, 2)
    d_bb_re = jnp.swapaxes(_block_diag_part(d_b_blk[:, :, :part], SSM_GROUP, states), 1, 2).reshape(-1, SSM_GROUP)
    d_bb_im = jnp.swapaxes(_block_diag_part(d_b_blk[:, :, part:], SSM_GROUP, states), 1, 2).reshape(-1, SSM_GROUP)

    def disc_b_bwd(cf_re, cf_im, b_re, b_im, g_re, g_im):
        _, vjp = jax.vjp(_disc_b, cf_re, cf_im, b_re, b_im)
        return vjp((g_re, g_im))

    d_cf_re, d_cf_im, g_b_re, g_b_im = _whole(
        disc_b_bwd, "s5_disc_b_bwd", [so["cf_re"].reshape(-1, 1), so["cf_im"].reshape(-1, 1), so["b_re2"], so["b_im2"], d_bb_re, d_bb_im],
        [(groups * states, 1)] * 2 + [(groups * states, SSM_GROUP)] * 2)

    def disc_lam_bwd(lam_re, lam_im, log_dt, g_bar_re, g_bar_im, g_cf_re, g_cf_im):
        _, vjp = jax.vjp(_disc_lam, lam_re, lam_im, log_dt)
        return vjp((g_bar_re, g_bar_im, g_cf_re, g_cf_im))

    g_bar_re, g_bar_im = _lam_bar_grad(acc_same, acc_swap)

    g_lam_re, g_lam_im, g_log_dt = _whole(
        disc_lam_bwd, "s5_disc_lam_bwd", [p["lam_re"], p["lam_im"], p["log_dt"].reshape(groups, 1), g_bar_re.reshape(groups, states), g_bar_im.reshape(groups, states),
                                          d_cf_re.reshape(groups, states), d_cf_im.reshape(groups, states)],
        [(groups, states)] * 2 + [(groups, 1)])
    (g_c_im,) = _elementwise(lambda a: (-a,), "s5_neg_c_im", [g_c_im_neg.reshape(groups * SSM_GROUP, states)], [F32])
    d_proj = jnp.concatenate([d_px, d_gates[0], d_q, d_k.astype(BF16), d_v.astype(BF16), d_gates[1], d_u, d_gates[2]], axis=1)
    big["w_in"] = _mm_carrying(carry, "in_proj_wgrad", sv["h"], d_proj, TN, F32, "in_proj_wgrad", out_quarters="cols", tn=WIDE_TILE)
    d_h = _mm_carrying(carry, "in_proj_dgrad", d_proj, gw["w_in"], NT, F32, "in_proj_dgrad", n=dm.d, b_quarters="k", tm=WIDE_TILE)

    def rms_bwd(dh, xx, dres, g):
        _, vjp = jax.vjp(_rms, xx, g)
        dx, dg = vjp(dh)
        dx = dx + dres
        return ((dx, dx) if want_bf else (dx,)), (dg,)

    d_xs, (g_ln_g,) = _rowwise(rms_bwd, "rms_bwd", [(d_h, dm.d, 0), (sv["x_in"], dm.d, 0), (d_out, dm.d, 0)], [p["ln_g"]],
                               [(dm.d, F32), (dm.d, BF16)] if want_bf else [(dm.d, F32)], [dm.d])
    small = dict(ln_g=g_ln_g, pool_scale=d_scales[0], lam_re=g_lam_re, lam_im=g_lam_im, log_dt=g_log_dt.reshape(1, groups),
                 b_re=g_b_re, b_im=g_b_im, c_re=g_c_re, c_im=g_c_im, d_skip=g_d_skip, b_glu=g_b_glu,
                 branch_g=jnp.concatenate(d_gs, axis=1))
    return d_xs[0], (d_xs[1] if want_bf else None), small


SMALL_ROWS = 8


def _pack(arrays):
    parts = []
    for a in arrays:
        flat = a.reshape(-1)
        pad = (-flat.shape[0]) % (SMALL_ROWS * LANES)
        parts.append(jnp.pad(flat, (0, pad)).reshape(-1, LANES))
    return jnp.concatenate(parts, axis=0)


def _unpack(buf, like):
    res, row = [], 0
    for a in like:
        size = math.prod(a.shape)
        rows = -(-size // (SMALL_ROWS * LANES)) * SMALL_ROWS
        res.append(buf[row:row + rows].reshape(-1)[:size].reshape(a.shape))
        row += rows
    return res


def kernel(x, ln_g, w_in, w_pool, pool_scale, lam_re, lam_im, log_dt, b_re, b_im, c_re, c_im, d_skip, w_glu, b_glu, branch_g, w_out, final_g, loss_target, m_ln_g, m_w_in, m_w_pool, m_pool_scale, m_lam_re, m_lam_im, m_log_dt, m_b_re, m_b_im, m_c_re, m_c_im, m_d_skip, m_w_glu, m_b_glu, m_branch_g, m_w_out, m_final_g, v_ln_g, v_w_in, v_w_pool, v_pool_scale, v_lam_re, v_lam_im, v_log_dt, v_b_re, v_b_im, v_c_re, v_c_im, v_d_skip, v_w_glu, v_b_glu, v_branch_g, v_w_out, v_final_g):
    given = dict(locals())
    weights = {n: given[n] for n in WEIGHTS}
    depth = ln_g.shape[0]
    _, length, d_model = x.shape
    dm = _Dims(d_model, length)
    x0, target = x[0], loss_target[0]
    c_idx = lax.axis_index("c")
    my_quarter = 2 * lax.axis_index("x") + lax.axis_index("y")

    shard2d = {(n, l): weights[n][l].reshape(-1, weights[n].shape[-1]) for l in range(depth) for n in SHARDED}
    keys = list(shard2d)
    halves16 = {k: w.astype(BF16).reshape(2, w.shape[0] // 2, -1) for k, w in shard2d.items()}
    gw = [dict() for _ in range(depth)]

    def gather(group, name):
        return _allgather_quarters([halves16[k] for k in group], name)

    def deliver_weights(group):
        def deliver(results):
            for (n, l), g in zip(group, results):
                rows, cols = shard2d[(n, l)].shape
                g = g.reshape(4, rows, cols)
                if n == "w_out":
                    g = g.reshape(4 * rows, cols)
                if n == "w_pool":
                    ngr = w_pool.shape[1]
                    g = g.reshape(4, ngr, rows // ngr, cols).transpose(1, 0, 2, 3).reshape(ngr, 4 * rows // ngr, cols)
                gw[l][n] = g
        return deliver

    first = [("w_in", 0), ("w_pool", 0), ("w_out", 0)]
    behind_in_proj = [("w_glu", 0)] + ([("w_in", 1)] if depth > 1 else [])
    behind_next_in_proj = [k for k in keys if k not in first + behind_in_proj]
    deliver_weights(first)(gather(first, "allgather_first").run())
    fwd_carry = [dict() for _ in range(depth)]
    fwd_carry[0]["in_proj"] = [(lambda: gather(behind_in_proj, "allgather_behind_in_proj"), deliver_weights(behind_in_proj))]
    if behind_next_in_proj:
        fwd_carry[1]["in_proj"] = [(lambda: gather(behind_next_in_proj, "allgather_behind_next_in_proj"),
                                    deliver_weights(behind_next_in_proj))]
    small_names = [n for n in WEIGHTS if n not in SHARDED and n != "final_g"]
    ps = [{n: (weights[n][l].reshape(1, -1) if weights[n][l].ndim == 1 else weights[n][l]) for n in small_names} for l in range(depth)]

    acts, saved = x0, []
    for l in range(depth):
        acts, sv = _layer_fwd(dm, acts, ps[l], gw[l], fwd_carry[l])
        saved.append(sv)

    def final(xx, tt, g):
        def loss_fn(xv, gv):
            err = _rms(xv, gv) - tt
            return 0.5 * jnp.sum(jnp.mean(err * err, axis=-1))
        val, (dx, dg) = jax.value_and_grad(loss_fn, argnums=(0, 1))(xx, g)
        return (dx, dx), (val.reshape(1, 1), dg)

    (d_act, d_act_bf), (loss_part, g_final_g) = _rowwise(
        final, "final_norm_loss", [(acts, dm.d, 0), (target, dm.d, 0)], [final_g.reshape(1, -1)], [(dm.d, F32), (dm.d, BF16)], [1, dm.d])
    loss = lax.psum(loss_part[0, 0], AXES)

    place = jnp.stack([c_idx, my_quarter]).astype(jnp.int32)
    big, small, mine, theirs = [dict() for _ in range(depth)], [None] * depth, {}, {}
    chip = {}

    def to_sibling_stage(group, tag):
        def deliver(from_sibling):
            for (n, l), r in zip(group, from_sibling):
                chip[(n, l)] = _chip_sum(big[l][n], r, place)
        return (lambda: _to_sibling([big[l][n] for n, l in group], "grads_to_sibling" + tag, other_half=True), deliver)

    def to_owner_stage(group, tag):
        def deliver(from_chips):
            for k, r in zip(group, from_chips):
                mine[k] = _owner_sum(chip[k][1], r)
        return (lambda: _to_owner_chips([chip[k][0] for k in group], "grads_to_owner_chips" + tag), deliver)

    def halves_stage(group, tag):
        def deliver(from_sibling):
            theirs.update(zip(group, from_sibling))
        return (lambda: _to_sibling([mine[k] for k in group], "reduced_half_to_sibling" + tag), deliver)

    deferred = {}
    for l in reversed(range(depth)):
        rest = [(n, l) for n in SHARDED if n != "w_in"]
        tag = "_%d" % l
        carry = {key: list(stages) for key, stages in deferred.items()}
        carry.setdefault("in_proj_wgrad", []).append(to_sibling_stage(rest, "_rest" + tag))
        carry.setdefault("in_proj_dgrad", []).extend([to_owner_stage(rest, "_rest" + tag), to_sibling_stage([("w_in", l)], "_w_in" + tag)])
        d_act, d_act_bf, small[l] = _layer_bwd(dm, d_act, d_act_bf, ps[l], gw[l], saved[l], want_bf=l > 0, carry=carry, big=big[l])
        deferred = {"in_proj_wgrad": [to_owner_stage([("w_in", l)], "_w_in" + tag)],
                    "in_proj_dgrad": [halves_stage([(n, l) for n in SHARDED], tag)]}
    for key in ("in_proj_wgrad", "in_proj_dgrad"):
        for make, deliver in deferred[key]:
            deliver(make().run())
    grad_x = d_act[None]

    small_all = [n for n in WEIGHTS if n not in SHARDED]
    packed = _pack([small[l][n] for l in range(depth) for n in small_names] + [g_final_g])
    summed = _unpack(_allreduce_small(packed), [weights[n][l] for l in range(depth) for n in small_names] + [final_g])
    g_small = {n: jnp.stack([summed[l * len(small_names) + i] for l in range(depth)]) for i, n in enumerate(small_names)}
    g_small["final_g"] = summed[-1]

    out_g, out_d, out_m, out_v = {}, {}, {}, {}
    for n in SHARDED:
        shape = weights[n].shape
        if n == "w_pool":
            ngr = shape[1]
            both = jnp.stack([jnp.where(c_idx == 0, jnp.stack([mine[(n, l)], theirs[(n, l)]]), jnp.stack([theirs[(n, l)], mine[(n, l)]]))
                              for l in range(depth)])
            g = both.reshape(depth, 2, ngr, -1, shape[-1]).transpose(0, 2, 1, 3, 4).reshape(shape)
            res = [g] + _elementwise(lambda *a: _adamw(*a), "adamw_" + n, [weights[n], g, given["m_" + n], given["v_" + n]], [F32] * 3)
        else:
            halves = lambda a: a.reshape(depth, 2, -1, shape[-1])
            res = _adamw_halves(halves(weights[n]), halves(given["m_" + n]), halves(given["v_" + n]),
                                [mine[(n, l)] for l in range(depth)], [theirs[(n, l)] for l in range(depth)], place, "adamw_" + n)
        out_g[n], out_d[n], out_m[n], out_v[n] = [r.reshape(shape) for r in res]
    d, m, v = _whole(_adamw, "adamw_small", [_pack([weights[n] for n in small_all]), _pack([g_small[n] for n in small_all]),
                                            _pack([given["m_" + n] for n in small_all]), _pack([given["v_" + n] for n in small_all])],
                     [_pack([weights[n] for n in small_all]).shape] * 3)
    like = [weights[n] for n in small_all]
    for n, dd, mm, vv in zip(small_all, _unpack(d, like), _unpack(m, like), _unpack(v, like)):
        out_g[n], out_d[n], out_m[n], out_v[n] = g_small[n], dd, mm, vv
    return (loss, grad_x, *[out_g[n] for n in WEIGHTS], *[out_d[n] for n in WEIGHTS],
            *[out_m[n] for n in WEIGHTS], *[out_v[n] for n in WEIGHTS])
```

```python
import functools
import math

import jax
import jax.numpy as jnp
from jax import lax
from jax.experimental import pallas as pl
from jax.experimental.pallas import tpu as pltpu

F32 = jnp.float32
BF16 = jnp.bfloat16
EPS = 1e-6
POOL_WINDOWS = (2, 4, 8, 16)
POOL_HALO = 16
HEAD_DIM = 128
SSM_GROUP = 16
SSM_STATE = 64
GROUPS_PER_CHUNK = 16
LANES = 128
VMEM_LIMIT_BYTES = 56 * 1024 * 1024
ROW_TILE_ELEMS = 2 * 1024 * 1024
WIDE_TILE = 1024
ADAM_LR, ADAM_B1, ADAM_B2, ADAM_EPS, ADAM_WD, ADAM_STEP = 0.001, 0.9, 0.999, 1e-08, 0.01, 10
MESH = pl.DeviceIdType.MESH
AXES = ("x", "y", "c")
WEIGHTS = ("ln_g", "w_in", "w_pool", "pool_scale", "lam_re", "lam_im", "log_dt", "b_re", "b_im",
           "c_re", "c_im", "d_skip", "w_glu", "b_glu", "branch_g", "w_out", "final_g")
SHARDED = ("w_in", "w_pool", "w_glu", "w_out")


def _params(*sem):
    return pltpu.CompilerParams(dimension_semantics=sem or None, vmem_limit_bytes=VMEM_LIMIT_BYTES)


def _dot(a, b, dims=((1,), (0,))):
    return lax.dot_general(a, b, (dims, ((), ())), preferred_element_type=F32)


NN, NT, TN = ((1,), (0,)), ((1,), (1,)), ((0,), (0,))


def _split(x):
    hi = x.astype(BF16)
    return hi, (x - hi.astype(F32)).astype(BF16)


def _dot_rounded(a, b, dims):
    return _dot(a.astype(BF16), b.astype(BF16), dims)


def _sigmoid(x):
    return 1.0 / (1.0 + jnp.exp(-x))


def _gelu(x):
    return 0.5 * x * (1.0 + jnp.tanh(0.7978845608028654 * (x + 0.044715 * x * x * x)))


def _rms(x, g):
    return x * lax.rsqrt(jnp.mean(x * x, axis=-1, keepdims=True) + EPS) * g


HBM_SPEC = pl.BlockSpec(memory_space=pltpu.HBM)


def _place():
    x, y, c = lax.axis_index("x"), lax.axis_index("y"), lax.axis_index("c")
    chips = [(1 - x, y), (x, 1 - y), (1 - x, 1 - y)]
    return x, y, c, chips


def _remote(src, dst, send_sem, recv_sem, target):
    return pltpu.make_async_remote_copy(src_ref=src, dst_ref=dst, send_sem=send_sem, recv_sem=recv_sem,
                                        device_id=target, device_id_type=MESH)


class _Exchange:
    def __init__(self, name, inputs, out_shapes, sems, start, finish):
        self.name, self.inputs, self.out_shapes, self.sems = name, list(inputs), list(out_shapes), sems
        self.start, self.finish, self.results = start, finish, None

    def run(self):
        n_in, n_out = len(self.inputs), len(self.out_shapes)

        def body(*refs):
            parts = (refs[:n_in], refs[n_in:n_in + n_out], refs[-2], refs[-1])
            self.start(*parts)
            self.finish(*parts)

        self.results = pl.pallas_call(
            body, name=self.name, in_specs=[HBM_SPEC] * n_in, out_specs=[HBM_SPEC] * n_out, out_shape=self.out_shapes,
            scratch_shapes=[pltpu.SemaphoreType.DMA(self.sems), pltpu.SemaphoreType.DMA(self.sems)],
            compiler_params=pltpu.CompilerParams(has_side_effects=True),
        )(*self.inputs)
        return self.results


def _allgather_quarters(shards, name):
    n, own = len(shards), 6

    def start(ins, outs, send_sems, recv_sems):
        x, y, c, chips = _place()
        mine = 2 * x + y
        for t in range(n):
            for k, (cx, cy) in enumerate(chips):
                _remote(ins[t].at[c], outs[t].at[mine, c], send_sems.at[t, k], recv_sems.at[t, k], (cx, cy, c)).start()
            _remote(ins[t], outs[t].at[mine], send_sems.at[t, own], recv_sems.at[t, own], (x, y, 1 - c)).start()

    def finish(ins, outs, send_sems, recv_sems):
        x, y, c, chips = _place()
        mine, sibling = 2 * x + y, (x, y, 1 - c)
        for t in range(n):
            for k, (cx, cy) in enumerate(chips):
                landed = outs[t].at[2 * cx + cy, c]
                _remote(landed, landed, send_sems.at[t, k], recv_sems.at[t, k], sibling).wait_recv()
                _remote(landed, landed, send_sems.at[t, 3 + k], recv_sems.at[t, 3 + k], sibling).start()
        for t in range(n):
            got = outs[t].at[mine]
            _remote(got, got, send_sems.at[t, own], recv_sems.at[t, own], sibling).wait_recv()
            _remote(ins[t], got, send_sems.at[t, own], recv_sems.at[t, own], sibling).wait_send()
            for k, (cx, cy) in enumerate(chips):
                got = outs[t].at[2 * cx + cy, 1 - c]
                _remote(got, got, send_sems.at[t, 3 + k], recv_sems.at[t, 3 + k], sibling).wait_recv()
                sent = outs[t].at[2 * cx + cy, c]
                _remote(sent, sent, send_sems.at[t, 3 + k], recv_sems.at[t, 3 + k], sibling).wait_send()
                _remote(ins[t].at[c], sent, send_sems.at[t, k], recv_sems.at[t, k], sibling).wait_send()

    return _Exchange(name, shards, [jax.ShapeDtypeStruct((4,) + s.shape, s.dtype) for s in shards], (n, 7), start, finish)


def _to_sibling(arrays, name, other_half=False):
    n = len(arrays)

    def copies(ins, outs, send_sems, recv_sems):
        x, y, c, _ = _place()
        return [_remote(ins[t].at[1 - c] if other_half else ins[t], outs[t], send_sems.at[t], recv_sems.at[t], (x, y, 1 - c))
                for t in range(n)]

    def start(*refs):
        for cp in copies(*refs):
            cp.start()

    def finish(*refs):
        for cp in copies(*refs):
            cp.wait()

    shapes = [jax.ShapeDtypeStruct(a.shape[1:] if other_half else a.shape, a.dtype) for a in arrays]
    return _Exchange(name, arrays, shapes, (n,), start, finish)


def _to_owner_chips(arrays, name):
    n = len(arrays)

    def copies(ins, outs, send_sems, recv_sems):
        x, y, c, chips = _place()
        return [_remote(ins[t].at[2 * cx + cy], outs[t].at[k], send_sems.at[t, k], recv_sems.at[t, k], (cx, cy, c))
                for t in range(n) for k, (cx, cy) in enumerate(chips)]

    def start(*refs):
        for cp in copies(*refs):
            cp.start()

    def finish(*refs):
        for cp in copies(*refs):
            cp.wait()

    shapes = [jax.ShapeDtypeStruct((3,) + a.shape[1:], a.dtype) for a in arrays]
    return _Exchange(name, arrays, shapes, (n, 3), start, finish)


def _fit(tile, dim):
    tile = min(tile, dim)
    step = LANES if tile >= LANES else 8
    tile -= tile % step
    while dim % tile:
        tile -= step
    return tile


def _mm(a, b, dims, out_dtype, name, *, n=None, b_quarters=None, out_quarters=None, add=None, comm=None, tm=512, tn=512, tk=4096):
    if dims == TN:
        k_dim, m_dim = a.shape
    else:
        m_dim, k_dim = a.shape
    if n is None:
        n = b.shape[0] if dims == NT else b.shape[1]
    m_unit = {None: m_dim, "cols": m_dim // 2, "rows": m_dim // 8}[out_quarters]
    n_unit = n // 4 if (b_quarters == "n" or out_quarters == "cols") else n
    k_unit = k_dim // 4 if b_quarters == "k" else k_dim
    tm, tn, tk = _fit(tm, m_unit), _fit(tn, n_unit), _fit(tk, k_unit)
    gm, gn, gk = m_dim // tm, n // tn, k_dim // tk
    mb, nb, kb = m_unit // tm, n_unit // tn, k_unit // tk
    if dims == TN:
        a_spec = pl.BlockSpec((tk, tm), lambda i, j, k: (k, i))
    else:
        a_spec = pl.BlockSpec((tm, tk), lambda i, j, k: (i, k))
    if b_quarters == "n":
        bspec = pl.BlockSpec((None, tk, tn), lambda i, j, k: (j // nb, k, j % nb))
    elif b_quarters == "k":
        bspec = pl.BlockSpec((None, tn, tk), lambda i, j, k: (k // kb, j, k % kb))
    elif dims == NT:
        bspec = pl.BlockSpec((tn, tk), lambda i, j, k: (j, k))
    else:
        bspec = pl.BlockSpec((tk, tn), lambda i, j, k: (k, j))
    if out_quarters == "cols":
        out_shape = (2, 4, m_unit, n_unit)
        out_spec = pl.BlockSpec((None, None, tm, tn), lambda i, j, k: (i // mb, j // nb, i % mb, j % nb))
    elif out_quarters == "rows":
        out_shape = (2, 4, m_unit, n)
        out_spec = pl.BlockSpec((None, None, tm, tn), lambda i, j, k: ((i // mb) % 2, i // (2 * mb), i % mb, j))
    else:
        out_shape, out_spec = (m_dim, n), pl.BlockSpec((tm, tn), lambda i, j, k: (i, j))
    in_specs, operands = [a_spec, bspec], [a, b]
    if add is not None:
        in_specs.append(pl.BlockSpec((tm, tn), lambda i, j, k: (i, j)))
        operands.append(add)

    n_in = len(operands)
    comms = list(comm) if comm else []
    n_cin, n_cout = sum(len(e.inputs) for e in comms), sum(len(e.out_shapes) for e in comms)
    n_acc = 1 if gk > 1 else 0

    def body(*refs):
        a_ref, b_ref = refs[0], refs[1]
        add_ref = refs[2] if add is not None else None
        o_ref = refs[n_in + n_cin]
        if comms:
            c_refs, i_at, o_at, s_at = [], n_in, n_in + n_cin + 1, n_in + n_cin + 1 + n_cout + n_acc
            for e in comms:
                c_refs.append((refs[i_at:i_at + len(e.inputs)], refs[o_at:o_at + len(e.out_shapes)], refs[s_at], refs[s_at + 1]))
                i_at, o_at, s_at = i_at + len(e.inputs), o_at + len(e.out_shapes), s_at + 2
            step = (pl.program_id(0) * gn + pl.program_id(1)) * gk + pl.program_id(2)

            @pl.when(step == 0)
            def _():
                for e, r in zip(comms, c_refs):
                    e.start(*r)

        def finish(r):
            if add_ref is not None:
                r = r + add_ref[...]
            o_ref[...] = r.astype(o_ref.dtype)

        if gk == 1:
            finish(_dot(a_ref[...], b_ref[...], dims))
        else:
            acc = refs[n_in + n_cin + 1 + n_cout]
            k = pl.program_id(2)

            @pl.when(k == 0)
            def _():
                acc[...] = jnp.zeros_like(acc)

            acc[...] += _dot(a_ref[...], b_ref[...], dims)

            @pl.when(k == gk - 1)
            def _():
                finish(acc[...])

        if comms:
            @pl.when(step == gm * gn * gk - 1)
            def _():
                for e, r in zip(comms, c_refs):
                    e.finish(*r)

    scratch = [pltpu.VMEM((tm, tn), F32)] if gk > 1 else []
    if not comms:
        return pl.pallas_call(
            body, name=name, grid=(gm, gn, gk), in_specs=in_specs, out_specs=out_spec,
            out_shape=jax.ShapeDtypeStruct(out_shape, out_dtype), scratch_shapes=scratch,
            compiler_params=_params("parallel", "parallel", "arbitrary"),
        )(*operands)
    for e in comms:
        scratch += [pltpu.SemaphoreType.DMA(e.sems), pltpu.SemaphoreType.DMA(e.sems)]
    res = pl.pallas_call(
        body, name=name, grid=(gm, gn, gk), in_specs=in_specs + [HBM_SPEC] * n_cin, out_specs=[out_spec] + [HBM_SPEC] * n_cout,
        out_shape=[jax.ShapeDtypeStruct(out_shape, out_dtype)] + [s for e in comms for s in e.out_shapes],
        scratch_shapes=scratch,
        compiler_params=pltpu.CompilerParams(dimension_semantics=("arbitrary",) * 3, vmem_limit_bytes=VMEM_LIMIT_BYTES,
                                             has_side_effects=True),
    )(*operands, *[x for e in comms for x in e.inputs])
    at = 1
    for e in comms:
        e.results = list(res[at:at + len(e.out_shapes)])
        at += len(e.out_shapes)
    return res[0]


def _mm3(a, b, dims, name, nc=None, tm=512, rows3d=()):
    rows = a.shape[0]
    tm = min(tm, rows)
    gm = rows // tm

    def row_spec(width, three_d):
        if three_d:
            return pl.BlockSpec((tm, width // LANES, LANES), lambda c, i: (i, c, 0))
        return pl.BlockSpec((tm, width), lambda c, i: (i, c))

    def load(ref, three_d):
        if not three_d:
            return ref[...]
        return jnp.concatenate([ref[:, r, :] for r in range(ref.shape[1])], axis=1)

    if dims == TN:
        ka = a.shape[1] * (a.shape[2] if "a" in rows3d else 1) // nc
        nb = b.shape[1] * (b.shape[2] if "b" in rows3d else 1) // nc

        def body(a_ref, b_ref, o_ref):
            @pl.when(pl.program_id(1) == 0)
            def _():
                o_ref[...] = jnp.zeros_like(o_ref)
            o_ref[...] += _dot_rounded(load(a_ref, "a" in rows3d), load(b_ref, "b" in rows3d), TN)

        return pl.pallas_call(
            body, name=name, grid=(nc, gm),
            in_specs=[row_spec(ka, "a" in rows3d), row_spec(nb, "b" in rows3d)],
            out_specs=pl.BlockSpec((None, ka, nb), lambda c, i: (c, 0, 0)),
            out_shape=jax.ShapeDtypeStruct((nc, ka, nb), F32),
            compiler_params=_params("parallel", "arbitrary"),
        )(a, b)
    nc, ka, nb = b.shape
    wa, wo = (ka, nb) if dims == NN else (nb, ka)

    def body(a_ref, b_ref, o_ref):
        res = _dot_rounded(load(a_ref, "a" in rows3d), b_ref[...], dims)
        if "o" in rows3d:
            for r in range(wo // LANES):
                o_ref[:, r, :] = res[:, r * LANES:(r + 1) * LANES]
        else:
            o_ref[...] = res

    out_shape = (rows, nc * wo // LANES, LANES) if "o" in rows3d else (rows, nc * wo)
    return pl.pallas_call(
        body, name=name, grid=(nc, gm),
        in_specs=[row_spec(wa, "a" in rows3d), pl.BlockSpec((None, ka, nb), lambda c, i: (c, 0, 0))],
        out_specs=row_spec(wo, "o" in rows3d),
        out_shape=jax.ShapeDtypeStruct(out_shape, F32),
        compiler_params=_params("parallel", "parallel"),
    )(a, b)


def _rowwise(fn, name, rows, vecs=(), outs=(), sums=()):
    length = rows[0][0].shape[0]
    total = sum(w for _, w, _ in rows) + sum(w for w, _ in outs)
    tile = 8
    while tile * 2 <= min(length, 512) and tile * 2 * total <= ROW_TILE_ELEMS:
        tile *= 2
    assert length % tile == 0
    n_r, n_v, n_o = len(rows), len(vecs), len(outs)

    def body(*refs):
        vals = [r[...] for r in refs[:n_r + n_v]]
        o_refs = refs[n_r + n_v:n_r + n_v + n_o]
        s_refs = refs[n_r + n_v + n_o:]
        res_o, res_s = fn(*vals)
        for ref, val in zip(o_refs, res_o):
            ref[...] = val.astype(ref.dtype)
        if s_refs:
            @pl.when(pl.program_id(0) == 0)
            def _():
                for ref in s_refs:
                    ref[...] = jnp.zeros_like(ref)
            for ref, val in zip(s_refs, res_s):
                ref[...] += val

    def row_spec(w, cb):
        return pl.BlockSpec((tile, w), lambda i: (i, cb))

    res = pl.pallas_call(
        body, name=name, grid=(length // tile,),
        in_specs=[row_spec(w, cb) for _, w, cb in rows] + [pl.BlockSpec(v.shape, lambda i: (0, 0)) for v in vecs],
        out_specs=[row_spec(w, 0) for w, _ in outs] + [pl.BlockSpec((1, w), lambda i: (0, 0)) for w in sums],
        out_shape=[jax.ShapeDtypeStruct((length, w), dt) for w, dt in outs]
        + [jax.ShapeDtypeStruct((1, w), F32) for w in sums],
        compiler_params=_params("arbitrary" if sums else "parallel"),
    )(*[a for a, _, _ in rows], *vecs)
    return res[:n_o], res[n_o:]


def _elementwise(fn, name, arrays, out_dtypes):
    shape = arrays[0].shape
    cols = shape[-1]
    flat = [a.reshape(-1, cols) for a in arrays]
    rows = flat[0].shape[0]
    tile = 8
    while tile * 2 <= rows and rows % (tile * 2) == 0 and tile * 2 * cols * (len(arrays) + len(out_dtypes)) <= ROW_TILE_ELEMS:
        tile *= 2
    assert rows % tile == 0
    n_in = len(flat)

    def body(*refs):
        res = fn(*[r[...] for r in refs[:n_in]])
        for ref, val in zip(refs[n_in:], res):
            ref[...] = val.astype(ref.dtype)

    spec = pl.BlockSpec((tile, cols), lambda i: (i, 0))
    res = pl.pallas_call(
        body, name=name, grid=(rows // tile,), in_specs=[spec] * n_in, out_specs=[spec] * len(out_dtypes),
        out_shape=[jax.ShapeDtypeStruct((rows, cols), dt) for dt in out_dtypes],
        compiler_params=_params("parallel"),
    )(*flat)
    return [r.reshape(shape) for r in res]


def _whole(fn, name, arrays, out_shapes):
    n_in = len(arrays)

    def body(*refs):
        res = fn(*[r[...] for r in refs[:n_in]])
        for ref, val in zip(refs[n_in:], res):
            ref[...] = val

    return pl.pallas_call(
        body, name=name, out_shape=[jax.ShapeDtypeStruct(s, F32) for s in out_shapes],
        compiler_params=_params(),
    )(*arrays)


def _grad_tile(rows, cols, arrays_per_step):
    tile = 8
    while tile * 2 <= rows and rows % (tile * 2) == 0 and tile * 2 * cols * arrays_per_step <= ROW_TILE_ELEMS:
        tile *= 2
    return tile


def _chip_sum(partial, from_sibling, place):
    _, _, rows, cols = partial.shape
    tile = _grad_tile(rows, cols, 4)

    def body(place_ref, a_ref, b_ref, o16_ref, o32_ref):
        total = a_ref[...] + b_ref[...]
        o16_ref[...] = total.astype(BF16)

        @pl.when(pl.program_id(1) == place_ref[1])
        def _():
            o32_ref[...] = total

    return pl.pallas_call(
        body, name="chip_sum",
        grid_spec=pltpu.PrefetchScalarGridSpec(
            num_scalar_prefetch=1, grid=(rows // tile, 4),
            in_specs=[pl.BlockSpec((None, None, tile, cols), lambda i, q, p: (p[0], q, i, 0)),
                      pl.BlockSpec((None, tile, cols), lambda i, q, p: (q, i, 0))],
            out_specs=[pl.BlockSpec((None, tile, cols), lambda i, q, p: (q, i, 0)),
                       pl.BlockSpec((tile, cols), lambda i, q, p: (i, 0))]),
        out_shape=[jax.ShapeDtypeStruct((4, rows, cols), BF16), jax.ShapeDtypeStruct((rows, cols), F32)],
        compiler_params=_params("parallel", "arbitrary"),
    )(place, partial, from_sibling)


def _owner_sum(own, from_chips):
    rows, cols = own.shape
    tile = _grad_tile(rows, cols, 4)

    def body(a_ref, r0_ref, r1_ref, r2_ref, o_ref):
        o_ref[...] = a_ref[...] + r0_ref[...].astype(F32) + r1_ref[...].astype(F32) + r2_ref[...].astype(F32)

    spec = pl.BlockSpec((tile, cols), lambda i: (i, 0))
    return pl.pallas_call(
        body, name="owner_sum", grid=(rows // tile,),
        in_specs=[spec] + [pl.BlockSpec((None, tile, cols), functools.partial(lambda i, k: (k, i, 0), k=k)) for k in range(3)],
        out_specs=spec, out_shape=jax.ShapeDtypeStruct((rows, cols), F32), compiler_params=_params("parallel"),
    )(own, from_chips, from_chips, from_chips)


def _adamw_halves(w, m, v, mine, theirs, place, name):
    depth, _, rows, cols = w.shape
    tile = _grad_tile(rows, cols, 9)

    def body(place_ref, w_ref, m_ref, v_ref, *refs):
        g_refs, outs = refs[:2 * depth], refs[2 * depth:]
        layer, half = pl.program_id(0), pl.program_id(1)
        g = None
        for d in range(depth):
            gd = jnp.where(half == place_ref[0], g_refs[2 * d][...], g_refs[2 * d + 1][...])
            g = gd if g is None else jnp.where(layer == d, gd, g)
        delta, m_new, v_new = _adamw(w_ref[...], g, m_ref[...], v_ref[...])
        for ref, val in zip(outs, (g, delta, m_new, v_new)):
            ref[...] = val

    full = pl.BlockSpec((None, None, tile, cols), lambda l, h, i, p: (l, h, i, 0))
    g_specs = []
    for d in range(depth):
        g_specs += [pl.BlockSpec((tile, cols), functools.partial(lambda l, h, i, p, d: (jnp.where(l == d, i, 0), 0), d=d))] * 2
    operands = [x for pair in zip(mine, theirs) for x in pair]
    return pl.pallas_call(
        body, name=name,
        grid_spec=pltpu.PrefetchScalarGridSpec(
            num_scalar_prefetch=1, grid=(depth, 2, rows // tile),
            in_specs=[full] * 3 + g_specs, out_specs=[full] * 4),
        out_shape=[jax.ShapeDtypeStruct(w.shape, F32)] * 4,
        compiler_params=_params("arbitrary", "arbitrary", "arbitrary"),
    )(place, w, m, v, *operands)


def _allreduce_small(flat):
    rows = flat.shape[0] + (-flat.shape[0]) % (2 * SUBLANES)
    half = rows // 2
    padded = jnp.pad(flat, ((0, rows - flat.shape[0]), (0, 0)))
    swap, final = 0, 4

    def body(in_ref, out_ref, sibling_ref, chips_ref, total_ref, send_sems, recv_sems):
        x, y, c, chips = _place()
        mine, sibling = 2 * x + y, (x, y, 1 - c)
        my_rows = pl.ds(pl.multiple_of(c * half, SUBLANES), half)
        their_rows = pl.ds(pl.multiple_of((1 - c) * half, SUBLANES), half)
        cp = _remote(in_ref, sibling_ref, send_sems.at[swap], recv_sems.at[swap], sibling)
        cp.start()
        cp.wait()
        chips_ref[mine] = in_ref[my_rows, :] + sibling_ref[my_rows, :]
        sends = [_remote(chips_ref.at[mine], chips_ref.at[mine], send_sems.at[1 + k], recv_sems.at[1 + k], (cx, cy, c))
                 for k, (cx, cy) in enumerate(chips)]
        for cp in sends:
            cp.start()
        for k, (cx, cy) in enumerate(chips):
            slot = chips_ref.at[2 * cx + cy]
            _remote(slot, slot, send_sems.at[1 + k], recv_sems.at[1 + k], sibling).wait_recv()
        for cp in sends:
            cp.wait_send()
        total_ref[my_rows, :] = (chips_ref[0] + chips_ref[1]) + (chips_ref[2] + chips_ref[3])
        cp = _remote(total_ref.at[my_rows, :], total_ref.at[my_rows, :], send_sems.at[final], recv_sems.at[final], sibling)
        cp.start()
        _remote(total_ref.at[their_rows, :], total_ref.at[their_rows, :], send_sems.at[final], recv_sems.at[final], sibling).wait_recv()
        cp.wait_send()
        out_ref[...] = total_ref[...]

    out = pl.pallas_call(
        body, name="allreduce_small_grads",
        in_specs=[pl.BlockSpec(memory_space=pltpu.VMEM)], out_specs=pl.BlockSpec(memory_space=pltpu.VMEM),
        out_shape=jax.ShapeDtypeStruct((rows, LANES), F32),
        scratch_shapes=[pltpu.VMEM((rows, LANES), F32), pltpu.VMEM((4, half, LANES), F32), pltpu.VMEM((rows, LANES), F32),
                        pltpu.SemaphoreType.DMA((5,)), pltpu.SemaphoreType.DMA((5,))],
        compiler_params=pltpu.CompilerParams(has_side_effects=True, vmem_limit_bytes=VMEM_LIMIT_BYTES),
    )(padded)
    return out[:flat.shape[0]]


def _pool_tile(length):
    return min(256, length)


def _pool_fwd(proj, w_pool):
    length = proj.shape[0]
    ngroups, ch, _ = w_pool.shape
    width, tile = ngroups * ch, _pool_tile(length)

    def body(cur_ref, prev_ref, w_ref, pooled_ref, mixed_ref):
        i = pl.program_id(0)
        cur = cur_ref[...]
        tail = jnp.where(i > 0, prev_ref[tile - POOL_HALO:tile, :], 0.0)
        padded = jnp.concatenate([tail, cur], axis=0)
        pos = (lax.broadcasted_iota(jnp.int32, (tile, 1), 0) + i * tile + 1).astype(F32)
        for g, window in enumerate(POOL_WINDOWS):
            cols = slice(g * ch, (g + 1) * ch)
            run, shift = padded[:, cols], 1
            while shift < window:
                run = run + pltpu.roll(run, shift, 0)
                shift *= 2
            pooled = (run[POOL_HALO:, :] / jnp.minimum(pos, float(window)) - cur[:, cols]).astype(BF16)
            pooled_ref[:, cols] = pooled
            mixed_ref[:, cols] = _dot(pooled, w_ref[g])

    return pl.pallas_call(
        body, name="pool_fwd", grid=(length // tile,),
        in_specs=[pl.BlockSpec((tile, width), lambda i: (i, 0)),
                  pl.BlockSpec((tile, width), lambda i: (jnp.maximum(i - 1, 0), 0)),
                  pl.BlockSpec(w_pool.shape, lambda i: (0, 0, 0))],
        out_specs=[pl.BlockSpec((tile, width), lambda i: (i, 0))] * 2,
        out_shape=[jax.ShapeDtypeStruct((length, width), BF16), jax.ShapeDtypeStruct((length, width), F32)],
        compiler_params=_params("parallel"),
    )(proj, proj, w_pool)


def _pool_bwd_mix(d_mixed, pooled, w_pool):
    length, width = d_mixed.shape
    ngroups, ch, _ = w_pool.shape
    tile = _pool_tile(length)

    def body(dm_ref, pooled_ref, w_ref, dp_ref, dw_ref):
        @pl.when(pl.program_id(0) == 0)
        def _():
            dw_ref[...] = jnp.zeros_like(dw_ref)
        for g in range(ngroups):
            cols = slice(g * ch, (g + 1) * ch)
            dm = dm_ref[:, cols].astype(BF16)
            dp_ref[:, cols] = _dot(dm, w_ref[g], NT)
            dw_ref[g] += _dot(pooled_ref[:, cols], dm, TN)

    return pl.pallas_call(
        body, name="pool_bwd_mix", grid=(length // tile,),
        in_specs=[pl.BlockSpec((tile, width), lambda i: (i, 0)), pl.BlockSpec((tile, width), lambda i: (i, 0)),
                  pl.BlockSpec(w_pool.shape, lambda i: (0, 0, 0))],
        out_specs=[pl.BlockSpec((tile, width), lambda i: (i, 0)), pl.BlockSpec(w_pool.shape, lambda i: (0, 0, 0))],
        out_shape=[jax.ShapeDtypeStruct((length, width), F32), jax.ShapeDtypeStruct(w_pool.shape, F32)],
        compiler_params=_params("arbitrary"),
    )(d_mixed, pooled, w_pool)


def _pool_bwd_window(d_pooled, ngroups):
    length, width = d_pooled.shape
    ch, tile = width // ngroups, _pool_tile(length)
    last = length // tile - 1

    def body(cur_ref, next_ref, dx_ref):
        i = pl.program_id(0)
        cur = cur_ref[...]
        head = jnp.where(i < last, next_ref[0:POOL_HALO, :], 0.0)
        padded = jnp.concatenate([cur, head], axis=0)
        rows = tile + POOL_HALO
        pos = (lax.broadcasted_iota(jnp.int32, (rows, 1), 0) + i * tile + 1).astype(F32)
        for g, window in enumerate(POOL_WINDOWS):
            cols = slice(g * ch, (g + 1) * ch)
            run, shift = padded[:, cols] / jnp.minimum(pos, float(window)), 1
            while shift < window:
                run = run + pltpu.roll(run, rows - shift, 0)
                shift *= 2
            dx_ref[:, cols] = (run[0:tile, :] - cur[:, cols]).astype(BF16)

    return pl.pallas_call(
        body, name="pool_bwd_window", grid=(length // tile,),
        in_specs=[pl.BlockSpec((tile, width), lambda i: (i, 0)),
                  pl.BlockSpec((tile, width), lambda i: (jnp.minimum(i + 1, last), 0))],
        out_specs=pl.BlockSpec((tile, width), lambda i: (i, 0)),
        out_shape=jax.ShapeDtypeStruct((length, width), BF16),
        compiler_params=_params("parallel"),
    )(d_pooled, d_pooled)


ATTN_TILE = 128
LOG_WEIGHT_FLOOR = -110.0


def _walk_back(n_chunks, chunk, carry):
    def cond(state):
        return jnp.logical_and(state[0] < n_chunks, jnp.max(state[1]) > LOG_WEIGHT_FLOOR)

    def step(state):
        return (state[0] + 1,) + tuple(chunk(n_chunks - 1 - state[0], tuple(state[1:])))

    return lax.while_loop(cond, step, (jnp.int32(0),) + tuple(carry))[1:]


def _stick_weights(q, kc, upper, run_log, mask):
    z = _dot(q, kc, NT)
    e = jnp.exp(-jnp.abs(z))
    softplus = jnp.maximum(z, 0.0) + jnp.log(1.0 + e)
    log_sig = z - softplus
    log_1m = -softplus if mask is None else jnp.where(mask, -softplus, 0.0)
    suffix = _dot(log_1m.astype(BF16), upper) + run_log
    w = jnp.exp(log_sig + suffix)
    if mask is not None:
        w = jnp.where(mask, w, 0.0)
    return w, log_sig, suffix[:, 0:1] + log_1m[:, 0:1]


def _attn_consts(tile):
    jj = lax.broadcasted_iota(jnp.int32, (tile, tile), 0)
    ss = lax.broadcasted_iota(jnp.int32, (tile, tile), 1)
    return (jj > ss).astype(BF16), (jj >= ss).astype(BF16), ss < jj


HEADS_PER_STEP = 2


def _heads_per_step(n_heads, *blocks):
    ok = n_heads % HEADS_PER_STEP == 0 and all(b % HEADS_PER_STEP == 0 for b in blocks)
    return HEADS_PER_STEP if ok else 1


def _slowest(run_logs):
    out = run_logs[0]
    for r in run_logs[1:]:
        out = jnp.maximum(out, r)
    return out


def _attn_fwd(proj, n_heads, q_blk, k_blk, v_blk):
    length = proj.shape[0]
    tile = min(ATTN_TILE, length)
    scale = HEAD_DIM ** -0.5

    hps = _heads_per_step(n_heads, q_blk, k_blk, v_blk)
    width = hps * HEAD_DIM

    def body(q_ref, k_ref, v_ref, o_ref):
        i = pl.program_id(1)
        cols = [slice(h * HEAD_DIM, (h + 1) * HEAD_DIM) for h in range(hps)]
        qs = [(q_ref[:, c] * scale).astype(BF16) for c in cols]
        upper, _, diag_mask = _attn_consts(tile)

        def chunk(j, carry, mask):
            start = pl.multiple_of(j * tile, tile)
            new = []
            for h, c in enumerate(cols):
                run_log, acc = carry[1 + 2 * h], carry[2 + 2 * h]
                kc = k_ref[pl.ds(start, tile), c].astype(BF16)
                vc = v_ref[pl.ds(start, tile), c].astype(BF16)
                w, _, run_log = _stick_weights(qs[h], kc, upper, run_log, mask)
                new += [run_log, acc + _dot(w.astype(BF16), vc)]
            return (_slowest(new[0::2]),) + tuple(new)

        zero = jnp.zeros((tile, 1), F32)
        carry = (zero,) + (zero, jnp.zeros((tile, HEAD_DIM), F32)) * hps
        carry = chunk(i, carry, diag_mask)
        carry = _walk_back(i, lambda j, cr: chunk(j, cr, None), carry)
        for h, c in enumerate(cols):
            o_ref[:, c] = carry[2 + 2 * h]

    return pl.pallas_call(
        body, name="attn_fwd", grid=(n_heads // hps, length // tile),
        in_specs=[pl.BlockSpec((tile, width), lambda h, i: (i, q_blk // hps + h)),
                  pl.BlockSpec((length, width), lambda h, i: (0, k_blk // hps + h)),
                  pl.BlockSpec((length, width), lambda h, i: (0, v_blk // hps + h))],
        out_specs=pl.BlockSpec((tile, width), lambda h, i: (i, h)),
        out_shape=jax.ShapeDtypeStruct((length, n_heads * HEAD_DIM), F32),
        compiler_params=_params("parallel", "parallel"),
    )(proj, proj, proj)


def _attn_bwd(proj, out, d_out, n_heads, q_blk, k_blk, v_blk):
    length = proj.shape[0]
    tile = min(ATTN_TILE, length)
    scale = HEAD_DIM ** -0.5

    hps = _heads_per_step(n_heads, q_blk, k_blk, v_blk)
    step_width = hps * HEAD_DIM

    def body(q_ref, k_ref, v_ref, o_ref, do_ref, dq_ref, dk_ref, dv_ref):
        i = pl.program_id(1)

        @pl.when(i == 0)
        def _():
            dk_ref[...] = jnp.zeros_like(dk_ref)
            dv_ref[...] = jnp.zeros_like(dv_ref)

        cols = [slice(h * HEAD_DIM, (h + 1) * HEAD_DIM) for h in range(hps)]
        qs = [(q_ref[:, c] * scale).astype(BF16) for c in cols]
        dos = [do_ref[:, c].astype(BF16) for c in cols]
        totals = [jnp.sum(do.astype(F32) * o_ref[:, c], axis=1, keepdims=True) for do, c in zip(dos, cols)]
        upper, upper_incl, diag_mask = _attn_consts(tile)

        def chunk(j, carry, mask):
            start = pl.multiple_of(j * tile, tile)
            new = []
            for h, c in enumerate(cols):
                run_log, run_g, dq = carry[1 + 3 * h:4 + 3 * h]
                q, do = qs[h], dos[h]
                kc = k_ref[pl.ds(start, tile), c].astype(BF16)
                vc = v_ref[pl.ds(start, tile), c].astype(BF16)
                w, log_sig, run_log = _stick_weights(q, kc, upper, run_log, mask)
                wb = w.astype(BF16)
                g = wb.astype(F32) * _dot(do, vc, NT)
                g_hi, g_lo = _split(g)
                g_suffix = _dot(g_hi, upper_incl) + _dot(g_lo, upper_incl) + run_g
                dz = g - jnp.exp(log_sig) * (g + (totals[h] - g_suffix))
                if mask is not None:
                    dz = jnp.where(mask, dz, 0.0)
                dzb = dz.astype(BF16)
                dk_ref[pl.ds(start, tile), c] += _dot(dzb, q, TN)
                dv_ref[pl.ds(start, tile), c] += _dot(wb, do, TN)
                new += [run_log, g_suffix[:, 0:1], dq + _dot(dzb, kc)]
            return (_slowest(new[0::3]),) + tuple(new)

        zero = jnp.zeros((tile, 1), F32)
        carry = (zero,) + (zero, zero, jnp.zeros((tile, HEAD_DIM), F32)) * hps
        carry = chunk(i, carry, diag_mask)
        carry = _walk_back(i, lambda j, cr: chunk(j, cr, None), carry)
        for h, c in enumerate(cols):
            dq_ref[:, c] = (carry[3 + 3 * h] * scale).astype(BF16)

    width = n_heads * HEAD_DIM
    tile_spec = pl.BlockSpec((tile, step_width), lambda h, i: (i, h))
    head_spec = pl.BlockSpec((length, step_width), lambda h, i: (0, h))
    return pl.pallas_call(
        body, name="attn_bwd", grid=(n_heads // hps, length // tile),
        in_specs=[pl.BlockSpec((tile, step_width), lambda h, i: (i, q_blk // hps + h)),
                  pl.BlockSpec((length, step_width), lambda h, i: (0, k_blk // hps + h)),
                  pl.BlockSpec((length, step_width), lambda h, i: (0, v_blk // hps + h)),
                  tile_spec, tile_spec],
        out_specs=[tile_spec, head_spec, head_spec],
        out_shape=[jax.ShapeDtypeStruct((length, width), BF16), jax.ShapeDtypeStruct((length, width), F32),
                   jax.ShapeDtypeStruct((length, width), F32)],
        compiler_params=_params("parallel", "arbitrary"),
    )(proj, proj, proj, out, d_out)


SCAN_CHUNK = 128


def _disc_lam(lam_re, lam_im, log_dt):
    dt = jnp.exp(log_dt)
    mag, phase = jnp.exp(lam_re * dt), lam_im * dt
    bar_re, bar_im = mag * jnp.cos(phase), mag * jnp.sin(phase)
    num_re, den = bar_re - 1.0, lam_re * lam_re + lam_im * lam_im
    return (bar_re, bar_im, (num_re * lam_re + bar_im * lam_im) / den, (bar_im * lam_re - num_re * lam_im) / den)


def _disc_b(cf_re, cf_im, b_re, b_im):
    return cf_re * b_re - cf_im * b_im, cf_re * b_im + cf_im * b_re


SCAN_UNROLL = 8
SUBLANES = 8


def _state_rows(per_group):
    return per_group.reshape(-1, SUBLANES, LANES)


def _swap_parts(x):
    pieces = []
    for k in range(x.shape[0] // (2 * SUBLANES)):
        base = 2 * SUBLANES * k
        pieces += [x[base + SUBLANES:base + 2 * SUBLANES], x[base:base + SUBLANES]]
    return jnp.concatenate(pieces, axis=0)


def _scan_coeffs(re_rows, im_rows, conj):
    same, cross = [], []
    for k in range(re_rows.shape[0]):
        same += [re_rows[k], re_rows[k]]
        cross += [im_rows[k], -im_rows[k]] if conj else [-im_rows[k], im_rows[k]]
    return jnp.concatenate(same, axis=0), jnp.concatenate(cross, axis=0)


def _scan_fwd(bu, bar_re, bar_im):
    length, groups, width = bu.shape
    chunk = min(SCAN_CHUNK, length)

    def body(bu_ref, re_ref, im_ref, st_ref, carry):
        @pl.when(pl.program_id(0) == 0)
        def _():
            carry[...] = jnp.zeros_like(carry)
        a_same, a_cross = _scan_coeffs(re_ref[...], im_ref[...], conj=False)

        def step(blk, x):
            for r in range(SCAN_UNROLL):
                t = blk * SCAN_UNROLL + r
                x = a_same * x + a_cross * _swap_parts(x) + bu_ref[t]
                st_ref[t] = x
            return x

        carry[...] = lax.fori_loop(0, chunk // SCAN_UNROLL, step, carry[...])

    blk = pl.BlockSpec((chunk, groups, width), lambda i: (i, 0, 0))
    par = pl.BlockSpec(bar_re.shape, lambda i: (0, 0, 0))
    return pl.pallas_call(
        body, name="s5_scan_fwd", grid=(length // chunk,), in_specs=[blk, par, par], out_specs=blk,
        out_shape=jax.ShapeDtypeStruct(bu.shape, F32), scratch_shapes=[pltpu.VMEM((groups, width), F32)],
        compiler_params=_params("arbitrary"),
    )(bu, bar_re, bar_im)


def _scan_bwd(d_states, states, bar_re, bar_im):
    length, groups, width = states.shape
    chunk = min(SCAN_CHUNK, length)
    last = length // chunk - 1

    def body(g_ref, st_ref, re_ref, im_ref, out_ref, same_ref, swap_ref, carry):
        @pl.when(pl.program_id(0) == 0)
        def _():
            carry[...] = jnp.zeros_like(carry)
            same_ref[...] = jnp.zeros_like(same_ref)
            swap_ref[...] = jnp.zeros_like(swap_ref)
        a_same, a_cross = _scan_coeffs(re_ref[...], im_ref[...], conj=True)

        def step(blk, cr):
            adj, acc_same, acc_swap = cr
            for r in range(SCAN_UNROLL):
                t = chunk - 1 - (blk * SCAN_UNROLL + r)
                s = st_ref[t]
                acc_same = acc_same + adj * s
                acc_swap = acc_swap + adj * _swap_parts(s)
                adj = g_ref[t] + a_same * adj + a_cross * _swap_parts(adj)
                out_ref[t] = adj
            return adj, acc_same, acc_swap

        adj, acc_same, acc_swap = lax.fori_loop(0, chunk // SCAN_UNROLL, step, (carry[...], same_ref[...], swap_ref[...]))
        carry[...] = adj
        same_ref[...] = acc_same
        swap_ref[...] = acc_swap

    blk = pl.BlockSpec((chunk, groups, width), lambda i: (last - i, 0, 0))
    par = pl.BlockSpec(bar_re.shape, lambda i: (0, 0, 0))
    acc = pl.BlockSpec((groups, width), lambda i: (0, 0))
    return pl.pallas_call(
        body, name="s5_scan_bwd", grid=(length // chunk,), in_specs=[blk, blk, par, par], out_specs=[blk, acc, acc],
        out_shape=[jax.ShapeDtypeStruct(states.shape, F32), jax.ShapeDtypeStruct((groups, width), F32),
                   jax.ShapeDtypeStruct((groups, width), F32)],
        scratch_shapes=[pltpu.VMEM((groups, width), F32)],
        compiler_params=_params("arbitrary"),
    )(d_states, states, bar_re, bar_im)


def _lam_bar_grad(acc_same, acc_swap):
    def fn(same, swap):
        g_re, g_im = [], []
        for k in range(same.shape[0] // (2 * SUBLANES)):
            re, im = slice(2 * SUBLANES * k, 2 * SUBLANES * k + SUBLANES), slice(2 * SUBLANES * k + SUBLANES, 2 * SUBLANES * (k + 1))
            g_re.append(same[re] + same[im])
            g_im.append(swap[im] - swap[re])
        return jnp.concatenate(g_re, axis=0), jnp.concatenate(g_im, axis=0)
    return _whole(fn, "s5_lam_bar_grad", [acc_same, acc_swap], [(acc_same.shape[0] // 2, LANES)] * 2)


def _block_diag(per_group):
    groups, a, b = per_group.shape
    nc = groups // GROUPS_PER_CHUNK
    eye = jnp.eye(GROUPS_PER_CHUNK, dtype=per_group.dtype)
    x = per_group.reshape(nc, GROUPS_PER_CHUNK, a, 1, b) * eye[None, :, None, :, None]
    return x.reshape(nc, GROUPS_PER_CHUNK * a, GROUPS_PER_CHUNK * b)


def _block_diag_part(chunks, a, b):
    nc = chunks.shape[0]
    x = chunks.reshape(nc, GROUPS_PER_CHUNK, a, GROUPS_PER_CHUNK, b)
    x = jnp.stack([x[:, g, :, g, :] for g in range(GROUPS_PER_CHUNK)], axis=1)
    return x.reshape(nc * GROUPS_PER_CHUNK, a, b)


def _epilogue(raw, gate, scale, g):
    return _rms(raw * scale, g) * (gate * _sigmoid(gate))


def _ssm_mid(y, u, d_skip):
    return _gelu(y + d_skip * u)


def _adamw(w, g, m, v):
    m = ADAM_B1 * m + (1.0 - ADAM_B1) * g
    v = ADAM_B2 * v + (1.0 - ADAM_B2) * (g * g)
    m_hat = m / (1.0 - ADAM_B1 ** ADAM_STEP)
    v_hat = v / (1.0 - ADAM_B2 ** ADAM_STEP)
    return -ADAM_LR * (m_hat / (jnp.sqrt(v_hat) + ADAM_EPS) + ADAM_WD * w), m, v


class _Dims:
    def __init__(self, d_model, length):
        self.d, self.length = d_model, length
        self.d_pool, self.d_attn = d_model // 4, d_model // 2
        self.d_ssm = d_model - self.d_pool - self.d_attn
        self.heads = self.d_attn // HEAD_DIM
        self.groups = self.d_ssm // SSM_GROUP
        self.d_in = 2 * self.d_pool + 4 * self.d_attn + 2 * self.d_ssm
        sizes = (self.d_pool, self.d_pool, self.d_attn, self.d_attn, self.d_attn, self.d_attn, self.d_ssm, self.d_ssm)
        offs = [0]
        for s in sizes[:-1]:
            offs.append(offs[-1] + s)
        (self.o_px, self.o_pgate, self.o_q, self.o_k, self.o_v, self.o_agate, self.o_u, self.o_sgate) = offs


def _ssm_operands(dm, p):
    groups, states = dm.groups, SSM_STATE
    bar_re, bar_im, cf_re, cf_im = _whole(_disc_lam, "s5_disc_lam", [p["lam_re"], p["lam_im"], p["log_dt"].reshape(groups, 1)],
                                          [(groups, states)] * 4)
    b_re2, b_im2 = p["b_re"].reshape(groups * states, SSM_GROUP), p["b_im"].reshape(groups * states, SSM_GROUP)
    bb_re, bb_im = _whole(_disc_b, "s5_disc_b", [cf_re.reshape(-1, 1), cf_im.reshape(-1, 1), b_re2, b_im2],
                          [(groups * states, SSM_GROUP)] * 2)
    per_group = lambda a: jnp.swapaxes(a.reshape(groups, states, SSM_GROUP), 1, 2)
    b_blk = jnp.concatenate([_block_diag(per_group(bb_re)), _block_diag(per_group(bb_im))], axis=2)
    c_blk = jnp.concatenate([_block_diag(jnp.swapaxes(p["c_re"], 1, 2)), _block_diag(jnp.swapaxes(-p["c_im"], 1, 2))], axis=1)
    return dict(bar_re=bar_re, bar_im=bar_im, bar_re_rows=_state_rows(bar_re), bar_im_rows=_state_rows(bar_im),
                cf_re=cf_re, cf_im=cf_im, b_re2=b_re2, b_im2=b_im2, b_blk=b_blk, c_blk=c_blk)


def _mm_carrying(carry, key, *args, **kwargs):
    stages = carry.get(key, [])
    exchanges = [make() for make, _ in stages]
    out = _mm(*args, comm=exchanges, **kwargs)
    for (_, deliver), exchange in zip(stages, exchanges):
        deliver(exchange.results)
    return out


def _layer_fwd(dm, x_in, p, gw, carry):
    length = dm.length
    blk = lambda off, w: off // w
    (h,), _ = _rowwise(lambda x, g: ((_rms(x, g),), ()), "rms_fwd", [(x_in, dm.d, 0)], [p["ln_g"]], [(dm.d, BF16)])
    proj = _mm_carrying(carry, "in_proj", h, gw["w_in"], NN, F32, "in_proj", n=dm.d_in, b_quarters="n", tn=WIDE_TILE)
    pooled, mixed = _pool_fwd(proj, gw["w_pool"])
    qb, kb, vb = dm.o_q // HEAD_DIM, dm.o_k // HEAD_DIM, dm.o_v // HEAD_DIM
    attn = _attn_fwd(proj, dm.heads, qb, kb, vb)
    so = _ssm_operands(dm, p)
    u_row = (proj, dm.d_ssm, blk(dm.o_u, dm.d_ssm))
    (u,), _ = _rowwise(lambda v: ((v,), ()), "take_u", [u_row], [], [(dm.d_ssm, F32)])
    bu = _mm3(u, so["b_blk"], NN, "s5_bu", rows3d="o")
    states = _scan_fwd(bu, so["bar_re_rows"], so["bar_im_rows"])
    y = _mm3(states, so["c_blk"], NN, "s5_y", rows3d="a")
    (hg,), _ = _rowwise(lambda yy, uu, dsk: ((_ssm_mid(yy, uu, dsk),), ()), "s5_mid_fwd",
                        [(y, dm.d_ssm, 0), (u, dm.d_ssm, 0)], [p["d_skip"]], [(dm.d_ssm, BF16)])
    z = _mm(hg, gw["w_glu"], NN, F32, "glu_proj", n=2 * dm.d_ssm, b_quarters="n")

    def glu(zz, bias):
        zz = zz + bias
        return (zz[:, :dm.d_ssm] * _sigmoid(zz[:, dm.d_ssm:]),), ()

    (ssm,), _ = _rowwise(glu, "glu_fwd", [(z, 2 * dm.d_ssm, 0)], [p["b_glu"]], [(dm.d_ssm, F32)])
    g_pool, g_attn, g_ssm = (p["branch_g"][:, :dm.d_pool], p["branch_g"][:, dm.d_pool:dm.d_pool + dm.d_attn],
                             p["branch_g"][:, dm.d_pool + dm.d_attn:])
    ones_attn, ones_ssm = jnp.ones((1, dm.d_attn), F32), jnp.ones((1, dm.d_ssm), F32)
    epi = lambda raw, gate, scale, g: ((_epilogue(raw, gate, scale, g),), ())
    branches = [("pool", mixed, dm.d_pool, dm.o_pgate, p["pool_scale"], g_pool),
                ("attn", attn, dm.d_attn, dm.o_agate, ones_attn, g_attn),
                ("ssm", ssm, dm.d_ssm, dm.o_sgate, ones_ssm, g_ssm)]
    ys = []
    for nm, raw, w, off, scale, g in branches:
        (yb,), _ = _rowwise(epi, "epilogue_fwd_" + nm, [(raw, w, 0), (proj, w, blk(off, w))], [scale, g], [(w, BF16)])
        ys.append(yb)
    y_cat = jnp.concatenate(ys, axis=1)
    x_out = _mm_carrying(carry, "out_proj", y_cat, gw["w_out"], NN, F32, "out_proj", add=x_in, tn=WIDE_TILE)
    saved = dict(x_in=x_in, h=h, proj=proj, pooled=pooled, mixed=mixed, attn=attn, so=so, u=u, states=states,
                 y=y, hg=hg, z=z, ssm=ssm, y_cat=y_cat, scales=(p["pool_scale"], ones_attn, ones_ssm), gs=(g_pool, g_attn, g_ssm))
    return x_out, saved


def _layer_bwd(dm, d_out, d_out_bf, p, gw, sv, want_bf, carry, big):
    length = dm.length
    blk = lambda off, w: off // w
    proj = sv["proj"]
    d_y = _mm_carrying(carry, "out_proj_dgrad", d_out_bf, gw["w_out"], NT, F32, "out_proj_dgrad", tn=WIDE_TILE)
    big["w_out"] = _mm(sv["y_cat"], d_out_bf, TN, F32, "out_proj_wgrad", out_quarters="rows", tn=WIDE_TILE)

    def epi_bwd(nseg):
        def fn(*vals):
            dys, (raw, gate, scale, g) = vals[:nseg], vals[nseg:]
            dyb = dys[0] if nseg == 1 else jnp.concatenate(dys, axis=1)
            _, vjp = jax.vjp(_epilogue, raw, gate, scale, g)
            d_raw, d_gate, d_scale, d_g = vjp(dyb)
            return (d_raw, d_gate), (d_scale, d_g)
        return fn

    branch = [("pool", sv["mixed"], dm.d_pool, dm.o_pgate, 0), ("attn", sv["attn"], dm.d_attn, dm.o_agate, dm.d_pool),
              ("ssm", sv["ssm"], dm.d_ssm, dm.o_sgate, dm.d_pool + dm.d_attn)]
    d_raws, d_gates, d_scales, d_gs = [], [], [], []
    for (nm, raw, w, off, yoff), scale, g in zip(branch, sv["scales"], sv["gs"]):
        seg = math.gcd(w, yoff) if yoff else w
        dy_rows = [(d_y, seg, yoff // seg + s) for s in range(w // seg)]
        (d_raw, d_gate), (d_scale, d_g) = _rowwise(
            epi_bwd(len(dy_rows)), "epilogue_bwd_" + nm, dy_rows + [(raw, w, 0), (proj, w, blk(off, w))], [scale, g],
            [(w, F32), (w, BF16)], [w, w])
        d_raws.append(d_raw); d_gates.append(d_gate); d_scales.append(d_scale); d_gs.append(d_g)
    d_pooled, g_w_pool = _pool_bwd_mix(d_raws[0], sv["pooled"], gw["w_pool"])
    ngr, ch = gw["w_pool"].shape[0], gw["w_pool"].shape[1]
    q_rows = ch // 4
    big["w_pool"] = g_w_pool.reshape(ngr, 4, 2, q_rows // 2, ch).transpose(2, 1, 0, 3, 4).reshape(2, 4, ngr * q_rows // 2, ch)
    d_px = _pool_bwd_window(d_pooled, len(POOL_WINDOWS))
    qb, kb, vb = dm.o_q // HEAD_DIM, dm.o_k // HEAD_DIM, dm.o_v // HEAD_DIM
    d_q, d_k, d_v = _attn_bwd(proj, sv["attn"], d_raws[1], dm.heads, qb, kb, vb)
    so = sv["so"]

    def glu_bwd(d_ssm, zz, bias):
        zz = zz + bias
        val, sg = zz[:, :dm.d_ssm], _sigmoid(zz[:, dm.d_ssm:])
        dz = jnp.concatenate([d_ssm * sg, d_ssm * val * sg * (1.0 - sg)], axis=1)
        return (dz,), (jnp.sum(dz, axis=0, keepdims=True),)

    (d_z,), (g_b_glu,) = _rowwise(glu_bwd, "glu_bwd", [(d_raws[2], dm.d_ssm, 0), (sv["z"], 2 * dm.d_ssm, 0)], [p["b_glu"]],
                                  [(2 * dm.d_ssm, BF16)], [2 * dm.d_ssm])
    d_hg = _mm(d_z, gw["w_glu"], NT, F32, "glu_dgrad", n=dm.d_ssm, b_quarters="k")
    big["w_glu"] = _mm(sv["hg"], d_z, TN, F32, "glu_wgrad", out_quarters="cols")

    def mid_bwd(dh, yy, uu, dsk):
        _, vjp = jax.vjp(_ssm_mid, yy, uu, dsk)
        dy_, du_, ddsk = vjp(dh)
        return (dy_, du_), (ddsk,)

    (d_yssm, d_u_direct), (g_d_skip,) = _rowwise(mid_bwd, "s5_mid_bwd", [(d_hg, dm.d_ssm, 0), (sv["y"], dm.d_ssm, 0), (sv["u"], dm.d_ssm, 0)],
                                                 [p["d_skip"]], [(dm.d_ssm, F32), (dm.d_ssm, F32)], [dm.d_ssm])
    d_states = _mm3(d_yssm, so["c_blk"], NT, "s5_y_dgrad", rows3d="o")
    d_c_blk = _mm3(sv["states"], d_yssm, TN, "s5_y_wgrad", nc=so["c_blk"].shape[0], rows3d="a")
    d_bu, acc_same, acc_swap = _scan_bwd(d_states, sv["states"], so["bar_re_rows"], so["bar_im_rows"])
    d_u_scan = _mm3(d_bu, so["b_blk"], NT, "s5_bu_dgrad", rows3d="a")
    d_b_blk = _mm3(sv["u"], d_bu, TN, "s5_bu_wgrad", nc=so["b_blk"].shape[0], rows3d="b")
    (d_u,) = _elementwise(lambda a, b: (a + b,), "s5_du", [d_u_direct, d_u_scan], [BF16])
    groups, states = dm.groups, SSM_STATE
    part = GROUPS_PER_CHUNK * states
    g_c_re = jnp.swapaxes(_block_diag_part(d_c_blk[:, :part], states, SSM_GROUP), 1, 2)
    g_c_im_neg = jnp.swapaxes(_block_diag_part(d_c_blk[:, part:], states, SSM_GROUP), 1, 2)
    d_bb_re = jnp.swapaxes(_block_diag_part(d_b_blk[:, :, :part], SSM_GROUP, states), 1, 2).reshape(-1, SSM_GROUP)
    d_bb_im = jnp.swapaxes(_block_diag_part(d_b_blk[:, :, part:], SSM_GROUP, states), 1, 2).reshape(-1, SSM_GROUP)

    def disc_b_bwd(cf_re, cf_im, b_re, b_im, g_re, g_im):
        _, vjp = jax.vjp(_disc_b, cf_re, cf_im, b_re, b_im)
        return vjp((g_re, g_im))

    d_cf_re, d_cf_im, g_b_re, g_b_im = _whole(
        disc_b_bwd, "s5_disc_b_bwd", [so["cf_re"].reshape(-1, 1), so["cf_im"].reshape(-1, 1), so["b_re2"], so["b_im2"], d_bb_re, d_bb_im],
        [(groups * states, 1)] * 2 + [(groups * states, SSM_GROUP)] * 2)

    def disc_lam_bwd(lam_re, lam_im, log_dt, g_bar_re, g_bar_im, g_cf_re, g_cf_im):
        _, vjp = jax.vjp(_disc_lam, lam_re, lam_im, log_dt)
        return vjp((g_bar_re, g_bar_im, g_cf_re, g_cf_im))

    g_bar_re, g_bar_im = _lam_bar_grad(acc_same, acc_swap)

    g_lam_re, g_lam_im, g_log_dt = _whole(
        disc_lam_bwd, "s5_disc_lam_bwd", [p["lam_re"], p["lam_im"], p["log_dt"].reshape(groups, 1), g_bar_re.reshape(groups, states), g_bar_im.reshape(groups, states),
                                          d_cf_re.reshape(groups, states), d_cf_im.reshape(groups, states)],
        [(groups, states)] * 2 + [(groups, 1)])
    (g_c_im,) = _elementwise(lambda a: (-a,), "s5_neg_c_im", [g_c_im_neg.reshape(groups * SSM_GROUP, states)], [F32])
    d_proj = jnp.concatenate([d_px, d_gates[0], d_q, d_k.astype(BF16), d_v.astype(BF16), d_gates[1], d_u, d_gates[2]], axis=1)
    big["w_in"] = _mm_carrying(carry, "in_proj_wgrad", sv["h"], d_proj, TN, F32, "in_proj_wgrad", out_quarters="cols", tn=WIDE_TILE)
    d_h = _mm_carrying(carry, "in_proj_dgrad", d_proj, gw["w_in"], NT, F32, "in_proj_dgrad", n=dm.d, b_quarters="k", tm=WIDE_TILE)

    def rms_bwd(dh, xx, dres, g):
        _, vjp = jax.vjp(_rms, xx, g)
        dx, dg = vjp(dh)
        dx = dx + dres
        return ((dx, dx) if want_bf else (dx,)), (dg,)

    d_xs, (g_ln_g,) = _rowwise(rms_bwd, "rms_bwd", [(d_h, dm.d, 0), (sv["x_in"], dm.d, 0), (d_out, dm.d, 0)], [p["ln_g"]],
                               [(dm.d, F32), (dm.d, BF16)] if want_bf else [(dm.d, F32)], [dm.d])
    small = dict(ln_g=g_ln_g, pool_scale=d_scales[0], lam_re=g_lam_re, lam_im=g_lam_im, log_dt=g_log_dt.reshape(1, groups),
                 b_re=g_b_re, b_im=g_b_im, c_re=g_c_re, c_im=g_c_im, d_skip=g_d_skip, b_glu=g_b_glu,
                 branch_g=jnp.concatenate(d_gs, axis=1))
    return d_xs[0], (d_xs[1] if want_bf else None), small


SMALL_ROWS = 8


def _pack(arrays):
    parts = []
    for a in arrays:
        flat = a.reshape(-1)
        pad = (-flat.shape[0]) % (SMALL_ROWS * LANES)
        parts.append(jnp.pad(flat, (0, pad)).reshape(-1, LANES))
    return jnp.concatenate(parts, axis=0)


def _unpack(buf, like):
    res, row = [], 0
    for a in like:
        size = math.prod(a.shape)
        rows = -(-size // (SMALL_ROWS * LANES)) * SMALL_ROWS
        res.append(buf[row:row + rows].reshape(-1)[:size].reshape(a.shape))
        row += rows
    return res


def kernel(x, ln_g, w_in, w_pool, pool_scale, lam_re, lam_im, log_dt, b_re, b_im, c_re, c_im, d_skip, w_glu, b_glu, branch_g, w_out, final_g, loss_target, m_ln_g, m_w_in, m_w_pool, m_pool_scale, m_lam_re, m_lam_im, m_log_dt, m_b_re, m_b_im, m_c_re, m_c_im, m_d_skip, m_w_glu, m_b_glu, m_branch_g, m_w_out, m_final_g, v_ln_g, v_w_in, v_w_pool, v_pool_scale, v_lam_re, v_lam_im, v_log_dt, v_b_re, v_b_im, v_c_re, v_c_im, v_d_skip, v_w_glu, v_b_glu, v_branch_g, v_w_out, v_final_g):
    given = dict(locals())
    weights = {n: given[n] for n in WEIGHTS}
    depth = ln_g.shape[0]
    _, length, d_model = x.shape
    dm = _Dims(d_model, length)
    x0, target = x[0], loss_target[0]
    c_idx = lax.axis_index("c")
    my_quarter = 2 * lax.axis_index("x") + lax.axis_index("y")

    shard2d = {(n, l): weights[n][l].reshape(-1, weights[n].shape[-1]) for l in range(depth) for n in SHARDED}
    keys = list(shard2d)
    halves16 = {k: w.astype(BF16).reshape(2, w.shape[0] // 2, -1) for k, w in shard2d.items()}
    gw = [dict() for _ in range(depth)]

    def gather(group, name):
        return _allgather_quarters([halves16[k] for k in group], name)

    def deliver_weights(group):
        def deliver(results):
            for (n, l), g in zip(group, results):
                rows, cols = shard2d[(n, l)].shape
                g = g.reshape(4, rows, cols)
                if n == "w_out":
                    g = g.reshape(4 * rows, cols)
                if n == "w_pool":
                    ngr = w_pool.shape[1]
                    g = g.reshape(4, ngr, rows // ngr, cols).transpose(1, 0, 2, 3).reshape(ngr, 4 * rows // ngr, cols)
                gw[l][n] = g
        return deliver

    first = [("w_in", 0), ("w_pool", 0), ("w_out", 0)]
    behind_in_proj = [("w_glu", 0)] + ([("w_in", 1)] if depth > 1 else [])
    behind_next_in_proj = [k for k in keys if k not in first + behind_in_proj]
    deliver_weights(first)(gather(first, "allgather_first").run())
    fwd_carry = [dict() for _ in range(depth)]
    fwd_carry[0]["in_proj"] = [(lambda: gather(behind_in_proj, "allgather_behind_in_proj"), deliver_weights(behind_in_proj))]
    if behind_next_in_proj:
        fwd_carry[1]["in_proj"] = [(lambda: gather(behind_next_in_proj, "allgather_behind_next_in_proj"),
                                    deliver_weights(behind_next_in_proj))]
    small_names = [n for n in WEIGHTS if n not in SHARDED and n != "final_g"]
    ps = [{n: (weights[n][l].reshape(1, -1) if weights[n][l].ndim == 1 else weights[n][l]) for n in small_names} for l in range(depth)]

    acts, saved = x0, []
    for l in range(depth):
        acts, sv = _layer_fwd(dm, acts, ps[l], gw[l], fwd_carry[l])
        saved.append(sv)

    def final(xx, tt, g):
        def loss_fn(xv, gv):
            err = _rms(xv, gv) - tt
            return 0.5 * jnp.sum(jnp.mean(err * err, axis=-1))
        val, (dx, dg) = jax.value_and_grad(loss_fn, argnums=(0, 1))(xx, g)
        return (dx, dx), (val.reshape(1, 1), dg)

    (d_act, d_act_bf), (loss_part, g_final_g) = _rowwise(
        final, "final_norm_loss", [(acts, dm.d, 0), (target, dm.d, 0)], [final_g.reshape(1, -1)], [(dm.d, F32), (dm.d, BF16)], [1, dm.d])
    loss = lax.psum(loss_part[0, 0], AXES)

    place = jnp.stack([c_idx, my_quarter]).astype(jnp.int32)
    big, small, mine, theirs = [dict() for _ in range(depth)], [None] * depth, {}, {}
    chip = {}

    def to_sibling_stage(group, tag):
        def deliver(from_sibling):
            for (n, l), r in zip(group, from_sibling):
                chip[(n, l)] = _chip_sum(big[l][n], r, place)
        return (lambda: _to_sibling([big[l][n] for n, l in group], "grads_to_sibling" + tag, other_half=True), deliver)

    def to_owner_stage(group, tag):
        def deliver(from_chips):
            for k, r in zip(group, from_chips):
                mine[k] = _owner_sum(chip[k][1], r)
        return (lambda: _to_owner_chips([chip[k][0] for k in group], "grads_to_owner_chips" + tag), deliver)

    def halves_stage(group, tag):
        def deliver(from_sibling):
            theirs.update(zip(group, from_sibling))
        return (lambda: _to_sibling([mine[k] for k in group], "reduced_half_to_sibling" + tag), deliver)

    deferred = {}
    for l in reversed(range(depth)):
        rest = [(n, l) for n in SHARDED if n != "w_in"]
        tag = "_%d" % l
        carry = {key: list(stages) for key, stages in deferred.items()}
        carry.setdefault("in_proj_wgrad", []).append(to_sibling_stage(rest, "_rest" + tag))
        carry.setdefault("in_proj_dgrad", []).extend([to_owner_stage(rest, "_rest" + tag), to_sibling_stage([("w_in", l)], "_w_in" + tag)])
        d_act, d_act_bf, small[l] = _layer_bwd(dm, d_act, d_act_bf, ps[l], gw[l], saved[l], want_bf=l > 0, carry=carry, big=big[l])
        deferred = {"in_proj_wgrad": [to_owner_stage([("w_in", l)], "_w_in" + tag)],
                    "in_proj_dgrad": [halves_stage([(n, l) for n in SHARDED], tag)]}
    for key in ("in_proj_wgrad", "in_proj_dgrad"):
        for make, deliver in deferred[key]:
            deliver(make().run())
    grad_x = d_act[None]

    small_all = [n for n in WEIGHTS if n not in SHARDED]
    packed = _pack([small[l][n] for l in range(depth) for n in small_names] + [g_final_g])
    summed = _unpack(_allreduce_small(packed), [weights[n][l] for l in range(depth) for n in small_names] + [final_g])
    g_small = {n: jnp.stack([summed[l * len(small_names) + i] for l in range(depth)]) for i, n in enumerate(small_names)}
    g_small["final_g"] = summed[-1]

    out_g, out_d, out_m, out_v = {}, {}, {}, {}
    for n in SHARDED:
        shape = weights[n].shape
        if n == "w_pool":
            ngr = shape[1]
            both = jnp.stack([jnp.where(c_idx == 0, jnp.stack([mine[(n, l)], theirs[(n, l)]]), jnp.stack([theirs[(n, l)], mine[(n, l)]]))
                              for l in range(depth)])
            g = both.reshape(depth, 2, ngr, -1, shape[-1]).transpose(0, 2, 1, 3, 4).reshape(shape)
            res = [g] + _elementwise(lambda *a: _adamw(*a), "adamw_" + n, [weights[n], g, given["m_" + n], given["v_" + n]], [F32] * 3)
        else:
            halves = lambda a: a.reshape(depth, 2, -1, shape[-1])
            res = _adamw_halves(halves(weights[n]), halves(given["m_" + n]), halves(given["v_" + n]),
                                [mine[(n, l)] for l in range(depth)], [theirs[(n, l)] for l in range(depth)], place, "adamw_" + n)
        out_g[n], out_d[n], out_m[n], out_v[n] = [r.reshape(shape) for r in res]
    d, m, v = _whole(_adamw, "adamw_small", [_pack([weights[n] for n in small_all]), _pack([g_small[n] for n in small_all]),
                                            _pack([given["m_" + n] for n in small_all]), _pack([given["v_" + n] for n in small_all])],
                     [_pack([weights[n] for n in small_all]).shape] * 3)
    like = [weights[n] for n in small_all]
    for n, dd, mm, vv in zip(small_all, _unpack(d, like), _unpack(m, like), _unpack(v, like)):
        out_g[n], out_d[n], out_m[n], out_v[n] = g_small[n], dd, mm, vv
    return (loss, grad_x, *[out_g[n] for n in WEIGHTS], *[out_d[n] for n in WEIGHTS],
            *[out_m[n] for n in WEIGHTS], *[out_v[n] for n in WEIGHTS])
```
